```python
import math
import jax
import jax.numpy as jnp
from jax import lax
import numpy as np

D_MODEL = 1024
BATCH = 8
SEQ = 8192
DEPTH = 2
DEC_BATCH = 8
DEC_SEQ = 64
PAST_LEN = 1024

CHUNK = 64
N_META = 16
N_MIXERS = 2
N_GDN_LAYERS = (DEPTH + 1) // 2
N_RWKV_LAYERS = DEPTH // 2
EPS = 1e-6
D_FF = -(-8 * D_MODEL // (3 * 256)) * 256

GDN_HEADS = 8
GDN_DK = 128
GDN_DV = 128
GDN_KDIM = GDN_HEADS * GDN_DK
GDN_VDIM = GDN_HEADS * GDN_DV
GDN_QKV = 2 * GDN_KDIM + GDN_VDIM
GDN_IN = GDN_QKV + GDN_VDIM + 2 * GDN_HEADS
CONV_W = 4

RWKV_N = 64
RWKV_HEADS = D_MODEL // RWKV_N
D_DECAY_LORA = max(32, int(round(1.8 * D_MODEL ** 0.5 / 32)) * 32)
D_AAA_LORA = max(32, int(round(1.8 * D_MODEL ** 0.5 / 32)) * 32)
D_GATE_LORA = max(32, int(round(0.6 * D_MODEL ** 0.8 / 32)) * 32)
N_MU = 6
GN_EPS = 64e-5

kernel_name = 'hybrid_gdn_rwkv7_stream_step'


def rms_norm(x, g, eps=EPS):
    x32 = x.astype(jnp.float32)
    y = x32 * lax.rsqrt(jnp.mean(x32 * x32, axis=-1, keepdims=True) + eps)
    return (y * g.astype(jnp.float32)).astype(x.dtype)


def l2_normalize(x, eps=1e-6):
    return x * lax.rsqrt(jnp.sum(x * x, axis=-1, keepdims=True) + eps)


def swiglu(x, w_gate, w_up, w_down):
    return (jax.nn.silu(x @ w_gate) * (x @ w_up)) @ w_down


def causal_short_conv(x, buf, w):
    T = x.shape[1]
    xe = jnp.concatenate([buf.astype(x.dtype), x], axis=1)
    y = sum(xe[:, j:j + T] * w[j] for j in range(CONV_W))
    return y, xe[:, -(CONV_W - 1):]


def chunk_segments(T, n_lead):
    segs = []
    if n_lead > 0:
        segs.append((0, n_lead, n_lead))
    rest = T - n_lead
    segs.append((n_lead, rest, min(CHUNK, rest)))
    return segs


def gdn_chunk_scan(q, k, v, g, beta, S):
    L = q.shape[3]
    G = jnp.cumsum(g, axis=-1)
    idx = jnp.arange(L)
    causal = idx[:, None] >= idx[None, :]
    strict = idx[:, None] > idx[None, :]
    decay_mat = jnp.exp(jnp.where(causal, G[..., :, None] - G[..., None, :], -jnp.inf))
    kk = jnp.einsum('bhcid,bhcjd->bhcij', k, k)
    A = jnp.where(strict, beta[..., :, None] * kk * decay_mat, 0.0)
    M = A + jnp.eye(L, dtype=jnp.float32)
    rhs = jnp.concatenate([v * beta[..., None], k * (beta * jnp.exp(G))[..., None]], axis=-1)
    sol = lax.linalg.triangular_solve(M, rhs, left_side=True, lower=True, unit_diagonal=True)
    u, w = sol[..., :GDN_DV], sol[..., GDN_DV:]
    qk = jnp.einsum('bhcid,bhcjd->bhcij', q, k) * decay_mat
    qg = q * jnp.exp(G)[..., None]
    kd = k * jnp.exp(G[..., -1:] - G)[..., None]
    gL = jnp.exp(G[..., -1])

    def step(S, xs):
        u_c, w_c, qk_c, qg_c, kd_c, gL_c = xs
        v_new = u_c - jnp.einsum('bhld,bhde->bhle', w_c, S)
        o_c = jnp.einsum('bhld,bhde->bhle', qg_c, S) + jnp.einsum('bhij,bhje->bhie', qk_c, v_new)
        S = S * gL_c[..., None, None] + jnp.einsum('bhld,bhle->bhde', kd_c, v_new)
        return S, o_c

    xs = tuple(jnp.moveaxis(t, 2, 0) for t in (u, w, qk, qg, kd, gL))
    S, o = lax.scan(step, S, xs)
    return jnp.moveaxis(o, 0, 2), S


def gdn_mixer(h, conv_buf, S0, w_in, conv_w, a_log, dt_bias, o_norm, w_out, n_lead):
    B, T, _ = h.shape
    f32 = jnp.float32
    proj = h @ w_in
    qkv, z, a_in, b_in = jnp.split(proj, [GDN_QKV, GDN_QKV + GDN_VDIM, GDN_QKV + GDN_VDIM + GDN_HEADS], axis=-1)
    qkv, new_buf = causal_short_conv(qkv, conv_buf, conv_w)
    qkv = jax.nn.silu(qkv.astype(f32))
    q, k, v = jnp.split(qkv, [GDN_KDIM, 2 * GDN_KDIM], axis=-1)
    q = l2_normalize(q.reshape(B, T, GDN_HEADS, GDN_DK)) * GDN_DK ** -0.5
    k = l2_normalize(k.reshape(B, T, GDN_HEADS, GDN_DK))
    v = v.reshape(B, T, GDN_HEADS, GDN_DV)
    beta = jax.nn.sigmoid(b_in.astype(f32))
    g = -jnp.exp(a_log.astype(f32)) * jax.nn.softplus(a_in.astype(f32) + dt_bias.astype(f32))
    q, k, v = (jnp.swapaxes(t, 1, 2) for t in (q, k, v))
    g, beta = jnp.swapaxes(g, 1, 2), jnp.swapaxes(beta, 1, 2)
    S = S0.astype(f32)
    outs = []
    for start, length, L in chunk_segments(T, n_lead):
        def blk(t):
            return t[:, :, start:start + length].reshape((B, GDN_HEADS, length // L, L) + t.shape[3:])
        o_seg, S = gdn_chunk_scan(blk(q), blk(k), blk(v), blk(g), blk(beta), S)
        outs.append(o_seg.reshape(B, GDN_HEADS, length, GDN_DV))
    o = jnp.swapaxes(jnp.concatenate(outs, axis=2), 1, 2)
    o = rms_norm(o, o_norm) * jax.nn.silu(z.astype(f32)).reshape(B, T, GDN_HEADS, GDN_DV)
    y = o.reshape(B, T, GDN_VDIM).astype(h.dtype) @ w_out
    return y, new_buf, S.astype(h.dtype)


def rwkv_mixer(h, shift_buf, S0, mu, w0, w1, w2, a0, a1, a2, g1, g2, k_k, k_a, r_k,
               w_r, w_k, w_v, w_o, ln_w, ln_b):
    B, T, D = h.shape
    f32 = jnp.float32
    h_prev = jnp.concatenate([shift_buf.astype(h.dtype), h[:, :-1]], axis=1)
    xx = h_prev - h
    xr, xw, xk, xv, xa, xg = h[None] + xx[None] * mu[:, None, None, :]
    r = (xr @ w_r).astype(f32)
    k = (xk @ w_k).astype(f32)
    v = (xv @ w_v).astype(f32)
    w_log = -jax.nn.softplus(-(w0 + jnp.tanh(xw @ w1) @ w2).astype(f32)) - 0.5
    a = jax.nn.sigmoid((a0 + (xa @ a1) @ a2).astype(f32))
    gate = (jax.nn.sigmoid(xg @ g1) @ g2).astype(f32)
    kk = l2_normalize((k * k_k.astype(f32)).reshape(B, T, RWKV_HEADS, RWKV_N))
    k = k * (1.0 + (a - 1.0) * k_a.astype(f32))
    decay = jnp.exp(-jnp.exp(w_log))

    def heads(t):
        return t.reshape(B, T, RWKV_HEADS, RWKV_N)

    r, k, v, a, decay = heads(r), heads(k), heads(v), heads(a), heads(decay)

    def step(S, inp):
        d_t, kk_t, b_t, v_t, k_t, r_t = inp
        sa = jnp.einsum('bhij,bhj->bhi', S, -kk_t)
        S = S * d_t[:, :, None, :] + sa[..., :, None] * b_t[:, :, None, :] + v_t[..., :, None] * k_t[:, :, None, :]
        return S, jnp.einsum('bhij,bhj->bhi', S, r_t)

    seq = tuple(jnp.moveaxis(t, 1, 0) for t in (decay, kk, kk * a, v, k, r))
    S, y = lax.scan(step, S0.astype(f32), seq)
    y = jnp.moveaxis(y, 0, 1)
    mean = jnp.mean(y, axis=-1, keepdims=True)
    var = jnp.mean(jnp.square(y - mean), axis=-1, keepdims=True)
    y = ((y - mean) * lax.rsqrt(var + GN_EPS)).reshape(B, T, D) * ln_w.astype(f32) + ln_b.astype(f32)
    bonus = jnp.sum(r * k * r_k.astype(f32), axis=-1, keepdims=True) * v
    y = y + bonus.reshape(B, T, D)
    y = (y * gate).astype(h.dtype) @ w_o
    return y, h[:, -1:], S.astype(h.dtype)


def setup_inputs(seed: int = 0) -> dict:
    key = jax.random.key(seed)
    keys = jax.random.split(key, 48)
    it = iter(range(48))

    def nrm(shape, scale):
        return jax.random.normal(keys[next(it)], shape, jnp.float32) * scale

    def uni(shape, lo, hi):
        return jax.random.uniform(keys[next(it)], shape, jnp.float32, lo, hi)

    NG, NR, D = N_GDN_LAYERS, N_RWKV_LAYERS, D_MODEL
    dt = jnp.exp(uni((NG, GDN_HEADS), math.log(1e-3), math.log(1e-1)))
    return {
        'x_prompt': nrm((BATCH, SEQ, D), 1.0),
        'x_sample': nrm((DEC_BATCH, DEC_SEQ, D), 1.0),
        'cache_gdn_conv': nrm((NG, DEC_BATCH, CONV_W - 1, GDN_QKV), 1.0),
        'state_gdn': nrm((NG, DEC_BATCH, GDN_HEADS, GDN_DK, GDN_DV), 0.1),
        'cache_rwkv_shift': nrm((NR, DEC_BATCH, 1, D), 1.0),
        'state_rwkv': nrm((NR, DEC_BATCH, RWKV_HEADS, RWKV_N, RWKV_N), 0.1),
        'meta_tokens': nrm((N_META, D), 1.0),
        'norm_mix': 1.0 + nrm((DEPTH, D), 0.02),
        'norm_ffn': 1.0 + nrm((DEPTH, D), 0.02),
        'norm_final': 1.0 + nrm((D,), 0.02),
        'gdn_w_in': nrm((NG, D, GDN_IN), D ** -0.5),
        'gdn_conv_w': nrm((NG, CONV_W, GDN_QKV), CONV_W ** -0.5),
        'gdn_a_log': jnp.log(uni((NG, GDN_HEADS), 1.0, 16.0)),
        'gdn_dt_bias': dt + jnp.log(-jnp.expm1(-dt)),
        'gdn_o_norm': 1.0 + nrm((NG, GDN_DV), 0.02),
        'gdn_w_out': nrm((NG, GDN_VDIM, D), GDN_VDIM ** -0.5),
        'rwkv_mu': uni((NR, N_MU, D), 0.0, 1.0),
        'rwkv_w0': uni((NR, D), -5.5, -0.5),
        'rwkv_w1': nrm((NR, D, D_DECAY_LORA), D ** -0.5),
        'rwkv_w2': nrm((NR, D_DECAY_LORA, D), 0.1 * D_DECAY_LORA ** -0.5),
        'rwkv_a0': nrm((NR, D), 0.1),
        'rwkv_a1': nrm((NR, D, D_AAA_LORA), D ** -0.5),
        'rwkv_a2': nrm((NR, D_AAA_LORA, D), 0.1 * D_AAA_LORA ** -0.5),
        'rwkv_g1': nrm((NR, D, D_GATE_LORA), D ** -0.5),
        'rwkv_g2': nrm((NR, D_GATE_LORA, D), D_GATE_LORA ** -0.5),
        'rwkv_k_k': 0.85 + nrm((NR, D), 0.02),
        'rwkv_k_a': 1.0 + nrm((NR, D), 0.02),
        'rwkv_r_k': nrm((NR, RWKV_HEADS, RWKV_N), 0.1),
        'rwkv_w_r': nrm((NR, D, D), D ** -0.5),
        'rwkv_w_k': nrm((NR, D, D), D ** -0.5),
        'rwkv_w_v': nrm((NR, D, D), D ** -0.5),
        'rwkv_w_o': nrm((NR, D, D), D ** -0.5),
        'rwkv_ln_w': 1.0 + nrm((NR, D), 0.02),
        'rwkv_ln_b': nrm((NR, D), 0.02),
        'ffn_w_gate': nrm((DEPTH, D, D_FF), D ** -0.5),
        'ffn_w_up': nrm((DEPTH, D, D_FF), D ** -0.5),
        'ffn_w_down': nrm((DEPTH, D_FF, D), D_FF ** -0.5),
    }


def reference(x_prompt, x_sample, cache_gdn_conv, state_gdn, cache_rwkv_shift, state_rwkv,
              meta_tokens, norm_mix, norm_ffn, norm_final,
              gdn_w_in, gdn_conv_w, gdn_a_log, gdn_dt_bias, gdn_o_norm, gdn_w_out,
              rwkv_mu, rwkv_w0, rwkv_w1, rwkv_w2, rwkv_a0, rwkv_a1, rwkv_a2, rwkv_g1, rwkv_g2,
              rwkv_k_k, rwkv_k_a, rwkv_r_k, rwkv_w_r, rwkv_w_k, rwkv_w_v, rwkv_w_o, rwkv_ln_w, rwkv_ln_b,
              ffn_w_gate, ffn_w_up, ffn_w_down):

    def run_trunk(h, n_lead, conv_bufs, gdn_states, shift_bufs, rwkv_states):
        new_conv, new_gdn, new_shift, new_rwkv = [], [], [], []
        for i in range(DEPTH):
            j = i // N_MIXERS
            hn = rms_norm(h, norm_mix[i])
            if i % N_MIXERS == 0:
                y, cb, S = gdn_mixer(hn, conv_bufs[j], gdn_states[j], gdn_w_in[j], gdn_conv_w[j],
                                     gdn_a_log[j], gdn_dt_bias[j], gdn_o_norm[j], gdn_w_out[j], n_lead)
                new_conv.append(cb)
                new_gdn.append(S)
            else:
                y, sb, S = rwkv_mixer(hn, shift_bufs[j], rwkv_states[j], rwkv_mu[j], rwkv_w0[j], rwkv_w1[j],
                                      rwkv_w2[j], rwkv_a0[j], rwkv_a1[j], rwkv_a2[j], rwkv_g1[j], rwkv_g2[j],
                                      rwkv_k_k[j], rwkv_k_a[j], rwkv_r_k[j], rwkv_w_r[j], rwkv_w_k[j],
                                      rwkv_w_v[j], rwkv_w_o[j], rwkv_ln_w[j], rwkv_ln_b[j])
                new_shift.append(sb)
                new_rwkv.append(S)
            h = h + y
            h = h + swiglu(rms_norm(h, norm_ffn[i]), ffn_w_gate[i], ffn_w_up[i], ffn_w_down[i])
        return (rms_norm(h, norm_final), jnp.stack(new_conv), jnp.stack(new_gdn),
                jnp.stack(new_shift), jnp.stack(new_rwkv))

    B, dt = x_prompt.shape[0], x_prompt.dtype
    meta = jnp.broadcast_to(meta_tokens.astype(dt)[None], (B, N_META, D_MODEL))
    h_p = jnp.concatenate([meta, x_prompt], axis=1)
    zero_conv = jnp.zeros((N_GDN_LAYERS, B, CONV_W - 1, GDN_QKV), dt)
    zero_gdn = jnp.zeros((N_GDN_LAYERS, B, GDN_HEADS, GDN_DK, GDN_DV), dt)
    zero_shift = jnp.zeros((N_RWKV_LAYERS, B, 1, D_MODEL), dt)
    zero_rwkv = jnp.zeros((N_RWKV_LAYERS, B, RWKV_HEADS, RWKV_N, RWKV_N), dt)
    out_p, p_gdn_conv, p_gdn_state, p_rwkv_shift, p_rwkv_state = run_trunk(
        h_p, N_META, zero_conv, zero_gdn, zero_shift, zero_rwkv)
    y_prompt = out_p[:, N_META:]

    y_sample, s_gdn_conv, s_gdn_state, s_rwkv_shift, s_rwkv_state = run_trunk(
        x_sample, 0, cache_gdn_conv, state_gdn, cache_rwkv_shift, state_rwkv)

    return (y_prompt, y_sample, p_gdn_conv, p_gdn_state, p_rwkv_shift, p_rwkv_state,
            s_gdn_conv, s_gdn_state, s_rwkv_shift, s_rwkv_state)
```

```python
import functools

import jax
import jax.numpy as jnp
from jax import lax
from jax.experimental import pallas as pl
from jax.experimental.pallas import tpu as pltpu

F32 = jnp.float32
BF16 = jnp.bfloat16

CHUNK = 64
N_META = 16
EPS = 1e-6
L2_EPS = 1e-6
GN_EPS = 64e-5

GDN_HEADS = 8
GDN_DK = 128
GDN_DV = 128
GDN_KDIM = GDN_HEADS * GDN_DK
GDN_QKV = 3 * GDN_KDIM
CONV_W = 4
RWKV_N = 64

LANES = 128
SUBLANES = 8
VMEM_LIMIT = 56 * 1024 * 1024

HIGHEST = lax.Precision.HIGHEST


def _dot(a, b):
    return jnp.dot(a.astype(BF16), b.astype(BF16), preferred_element_type=F32)


def _dot_nt(a, b):
    return lax.dot_general(a.astype(BF16), b.astype(BF16), (((1,), (1,)), ((), ())),
                           preferred_element_type=F32)


def _dot_tn(a, b):
    return lax.dot_general(a.astype(BF16), b.astype(BF16), (((0,), (0,)), ((), ())),
                           preferred_element_type=F32)


def _dot_exact_lhs(m01, x):
    hi = x.astype(BF16)
    r1 = x - hi.astype(F32)
    mid = r1.astype(BF16)
    lo = (r1 - mid.astype(F32)).astype(BF16)
    return _dot(m01, hi) + _dot(m01, mid) + _dot(m01, lo)


def _dot_exact_rhs(x, m01):
    hi = x.astype(BF16)
    r1 = x - hi.astype(F32)
    mid = r1.astype(BF16)
    lo = (r1 - mid.astype(F32)).astype(BF16)
    return _dot(hi, m01) + _dot(mid, m01) + _dot(lo, m01)


def _rms(x, g, eps=EPS):
    return x * lax.rsqrt(jnp.mean(x * x, axis=-1, keepdims=True) + eps) * g


def _neumann(x, eye, size):
    t = eye + x
    p = x
    n = 2
    while n < size:
        p = _dot(p, p)
        t = t + _dot(t, p)
        n *= 2
    return t


def _pick_tile(n, candidates):
    for c in candidates:
        if n % c == 0:
            return c
    raise ValueError(f"no tile for {n}")


def _const_spec(shape):
    nd = len(shape)
    return pl.BlockSpec(shape, lambda *_: (0,) * nd)


def _norm_proj_kernel(x_ref, g_ref, w_ref, o_ref):
    xn = _rms(x_ref[...], g_ref[...]).astype(BF16)
    o_ref[...] = _dot(xn, w_ref[...])


def _norm_proj(x2d, g_row, w_bf16):
    rows, d = x2d.shape
    n = w_bf16.shape[1]
    tm = _pick_tile(rows, (256, 128, 64))
    return pl.pallas_call(
        _norm_proj_kernel,
        grid=(rows // tm,),
        in_specs=[pl.BlockSpec((tm, d), lambda i: (i, 0)), _const_spec((1, d)), _const_spec((d, n))],
        out_specs=pl.BlockSpec((tm, n), lambda i: (i, 0)),
        out_shape=jax.ShapeDtypeStruct((rows, n), F32),
        compiler_params=pltpu.CompilerParams(dimension_semantics=("arbitrary",), vmem_limit_bytes=VMEM_LIMIT),
        name="gdn_in_proj",
    )(x2d, g_row, w_bf16)


def _gdn_scan_kernel(qkv_ref, a_ref, b_ref, cbuf_ref, s0_ref, cw_ref, alog_ref, dtb_ref, onorm_ref,
                     o_ref, s_ref, xs_ref):
    c = pl.program_id(1)
    L = CHUNK
    hist = SUBLANES

    @pl.when(c == 0)
    def _():
        xs_ref[0:hist, :] = cbuf_ref[...]
        s_ref[...] = s0_ref[...]

    xs_ref[hist:hist + L, :] = qkv_ref[...]

    row = lax.broadcasted_iota(jnp.int32, (L, L), 0)
    col = lax.broadcasted_iota(jnp.int32, (L, L), 1)
    causal = row >= col
    strict = row > col
    tril = causal.astype(BF16)
    eye = (row == col).astype(F32)

    g = -jnp.exp(alog_ref[...]) * jax.nn.softplus(a_ref[...] + dtb_ref[...])
    beta = jax.nn.sigmoid(b_ref[...])
    G = _dot_exact_lhs(tril, g)
    Gt = G.T
    GL = G[L - 1:L, :]
    exp_g = jnp.exp(G)
    exp_gr = jnp.exp(GL - G)
    exp_gl = jnp.exp(GL)

    def conv(off):
        base = hist - (CONV_W - 1)
        acc = cw_ref[0:1, off:off + LANES] * xs_ref[base:base + L, off:off + LANES]
        for j in range(1, CONV_W):
            acc = acc + cw_ref[j:j + 1, off:off + LANES] * xs_ref[base + j:base + j + L, off:off + LANES]
        return jax.nn.silu(acc)

    for h in range(GDN_HEADS):
        q = conv(h * GDN_DK)
        k = conv(GDN_KDIM + h * GDN_DK)
        v = conv(2 * GDN_KDIM + h * GDN_DV)
        q = q * lax.rsqrt(jnp.sum(q * q, axis=-1, keepdims=True) + L2_EPS) * (GDN_DK ** -0.5)
        k = k * lax.rsqrt(jnp.sum(k * k, axis=-1, keepdims=True) + L2_EPS)
        bcol = beta[:, h:h + 1]
        dlt = G[:, h:h + 1] - Gt[h:h + 1, :]
        dec = jnp.where(causal, jnp.exp(jnp.where(causal, dlt, 0.0)), 0.0)
        kk = _dot_nt(k, k)
        a_mat = jnp.where(strict, bcol * kk * dec, 0.0)
        t_inv = _neumann(-a_mat, eye, L)
        rhs = jnp.concatenate([v * bcol, k * (bcol * exp_g[:, h:h + 1])], axis=1)
        sol = _dot(t_inv, rhs)
        u = sol[:, :GDN_DV]
        w = sol[:, GDN_DV:]
        qk = _dot_nt(q, k) * dec
        s = s_ref[h]
        v_new = u - _dot(w, s)
        o = _dot(q * exp_g[:, h:h + 1], s) + _dot(qk, v_new)
        s_ref[h] = s * exp_gl[:, h:h + 1] + _dot_tn(k * exp_gr[:, h:h + 1], v_new)
        o = o * lax.rsqrt(jnp.mean(o * o, axis=-1, keepdims=True) + EPS) * onorm_ref[...]
        o_ref[:, h * GDN_DV:(h + 1) * GDN_DV] = o

    xs_ref[0:hist, :] = xs_ref[L:L + hist, :]


def _gdn_scan(proj3, cbuf8, s0, conv_w, alog_row, dtb_row, onorm_row):
    b, tp, _ = proj3.shape
    nc = tp // CHUNK
    a_blk = (GDN_QKV + GDN_KDIM) // LANES
    return pl.pallas_call(
        _gdn_scan_kernel,
        grid=(b, nc),
        in_specs=[
            pl.BlockSpec((None, CHUNK, GDN_QKV), lambda i, c: (i, c, 0)),
            pl.BlockSpec((None, CHUNK, LANES), lambda i, c: (i, c, a_blk)),
            pl.BlockSpec((None, CHUNK, LANES), lambda i, c: (i, c, a_blk + 1)),
            pl.BlockSpec((None, SUBLANES, GDN_QKV), lambda i, c: (i, 0, 0)),
            pl.BlockSpec((None, GDN_HEADS, GDN_DK, GDN_DV), lambda i, c: (i, 0, 0, 0)),
            _const_spec((CONV_W, GDN_QKV)),
            _const_spec((1, LANES)),
            _const_spec((1, LANES)),
            _const_spec((1, GDN_DV)),
        ],
        out_specs=[
            pl.BlockSpec((None, CHUNK, GDN_KDIM), lambda i, c: (i, c, 0)),
            pl.BlockSpec((None, GDN_HEADS, GDN_DK, GDN_DV), lambda i, c: (i, 0, 0, 0)),
        ],
        out_shape=[
            jax.ShapeDtypeStruct((b, tp, GDN_KDIM), F32),
            jax.ShapeDtypeStruct((b, GDN_HEADS, GDN_DK, GDN_DV), F32),
        ],
        scratch_shapes=[pltpu.VMEM((CHUNK + 2 * SUBLANES, GDN_QKV), F32)],
        compiler_params=pltpu.CompilerParams(dimension_semantics=("arbitrary", "arbitrary"),
                                             vmem_limit_bytes=VMEM_LIMIT),
        name="gdn_scan",
    )(proj3, proj3, proj3, cbuf8, s0, conv_w, alog_row, dtb_row, onorm_row)


def _mix_ffn_kernel(*refs, silu_gate, final_norm):
    if final_norm:
        y_ref, gm_ref, h_ref, wo_ref, nf_ref, wg_ref, wu_ref, wd_ref, fn_ref, o_ref = refs
    else:
        y_ref, gm_ref, h_ref, wo_ref, nf_ref, wg_ref, wu_ref, wd_ref, o_ref = refs
    gm = gm_ref[...]
    if silu_gate:
        gm = jax.nn.silu(gm)
    h1 = h_ref[...] + _dot((y_ref[...] * gm).astype(BF16), wo_ref[...])
    n = _rms(h1, nf_ref[...]).astype(BF16)
    act = (jax.nn.silu(_dot(n, wg_ref[...])) * _dot(n, wu_ref[...])).astype(BF16)
    h2 = h1 + _dot(act, wd_ref[...])
    if final_norm:
        h2 = _rms(h2, fn_ref[...])
    o_ref[...] = h2


def _mix_ffn(y2d, gm2d, gm_blk, h2d, wo, nf_row, wg, wu, wd, fn_row, *, silu_gate, name):
    rows, d = h2d.shape
    dff = wg.shape[1]
    tm = _pick_tile(rows, (256, 128, 64))
    final_norm = fn_row is not None
    in_specs = [
        pl.BlockSpec((tm, d), lambda i: (i, 0)),
        pl.BlockSpec((tm, d), lambda i: (i, gm_blk)),
        pl.BlockSpec((tm, d), lambda i: (i, 0)),
        _const_spec((d, d)), _const_spec((1, d)),
        _const_spec((d, dff)), _const_spec((d, dff)), _const_spec((dff, d)),
    ]
    args = [y2d, gm2d, h2d, wo, nf_row, wg, wu, wd]
    if final_norm:
        in_specs.append(_const_spec((1, d)))
        args.append(fn_row)
    return pl.pallas_call(
        functools.partial(_mix_ffn_kernel, silu_gate=silu_gate, final_norm=final_norm),
        grid=(rows // tm,),
        in_specs=in_specs,
        out_specs=pl.BlockSpec((tm, d), lambda i: (i, 0)),
        out_shape=jax.ShapeDtypeStruct((rows, d), F32),
        compiler_params=pltpu.CompilerParams(dimension_semantics=("arbitrary",), vmem_limit_bytes=VMEM_LIMIT),
        name=name,
    )(*args)


def _rwkv_proj_kernel(h_ref, sb_ref, vec_ref, mu_ref, wr_ref, wk_ref, wv_ref,
                      w1_ref, w2_ref, a1_ref, a2_ref, g1_ref, g2_ref,
                      r_o, k_o, v_o, kk_o, a_o, w_o, g_o, last_o, hs_ref):
    t = pl.program_id(1)
    tt = h_ref.shape[0]
    hist = SUBLANES
    hn = _rms(h_ref[...], vec_ref[0:1, :])

    @pl.when(t == 0)
    def _():
        hs_ref[hist - 1:hist, :] = sb_ref[...]

    hs_ref[hist:hist + tt, :] = hn
    prev = hs_ref[hist - 1:hist - 1 + tt, :]
    hs_ref[hist - 1:hist, :] = hn[tt - 1:tt, :]
    last_o[...] = hn[tt - 1:tt, :]
    xx = prev - hn

    def mix(i):
        return (hn + xx * mu_ref[i:i + 1, :]).astype(BF16)

    r = _dot(mix(0), wr_ref[...])
    k = _dot(mix(2), wk_ref[...])
    v = _dot(mix(3), wv_ref[...])
    wl = vec_ref[1:2, :] + _dot(jnp.tanh(_dot(mix(1), w1_ref[...])).astype(BF16), w2_ref[...])
    w_log = -jax.nn.softplus(-wl) - 0.5
    a = jax.nn.sigmoid(vec_ref[2:3, :] + _dot(_dot(mix(4), a1_ref[...]).astype(BF16), a2_ref[...]))
    gate = _dot(jax.nn.sigmoid(_dot(mix(5), g1_ref[...])).astype(BF16), g2_ref[...])
    r_o[...] = r
    k_o[...] = k * (1.0 + (a - 1.0) * vec_ref[4:5, :])
    v_o[...] = v
    kk_o[...] = k * vec_ref[3:4, :]
    a_o[...] = a
    w_o[...] = -jnp.exp(w_log)
    g_o[...] = gate


def _rwkv_proj(h3, sbuf, vec, mu8, wr, wk, wv, w1, w2, a1, a2, g1, g2):
    b, tp, d = h3.shape
    tt = _pick_tile(tp, (256, 192, 128, 64))
    tok = pl.BlockSpec((None, tt, d), lambda i, t: (i, t, 0))
    weights = [wr, wk, wv, w1, w2, a1, a2, g1, g2]
    return pl.pallas_call(
        _rwkv_proj_kernel,
        grid=(b, tp // tt),
        in_specs=[tok, pl.BlockSpec((None, 1, d), lambda i, t: (i, 0, 0)),
                  _const_spec(vec.shape), _const_spec(mu8.shape)] + [_const_spec(w.shape) for w in weights],
        out_specs=[tok] * 7 + [pl.BlockSpec((None, 1, d), lambda i, t: (i, 0, 0))],
        out_shape=[jax.ShapeDtypeStruct((b, tp, d), F32)] * 7 + [jax.ShapeDtypeStruct((b, 1, d), F32)],
        scratch_shapes=[pltpu.VMEM((tt + SUBLANES, d), F32)],
        compiler_params=pltpu.CompilerParams(dimension_semantics=("arbitrary", "arbitrary"),
                                             vmem_limit_bytes=VMEM_LIMIT),
        name="rwkv_proj",
    )(h3, sbuf, vec, mu8, *weights)


def _rwkv_scan_kernel(r_ref, k_ref, v_ref, kk_ref, a_ref, w_ref, s0_ref, rk_ref, lnw_ref, lnb_ref,
                      yo_ref, s_ref):
    c = pl.program_id(1)
    L = CHUNK
    N = RWKV_N
    P2 = 2 * N
    d = r_ref.shape[1]

    @pl.when(c == 0)
    def _():
        s_ref[...] = s0_ref[...]

    row = lax.broadcasted_iota(jnp.int32, (L, L), 0)
    col = lax.broadcasted_iota(jnp.int32, (L, L), 1)
    tril = (row >= col).astype(BF16)
    r2 = lax.broadcasted_iota(jnp.int32, (P2, P2), 0)
    c2 = lax.broadcasted_iota(jnp.int32, (P2, P2), 1)
    same = (r2 < N) == (c2 < N)
    strict2 = same & ((r2 & (N - 1)) > (c2 & (N - 1)))
    incl2 = same & ((r2 & (N - 1)) >= (c2 & (N - 1)))
    eye2 = (r2 == c2).astype(F32)
    ones_blk = same.astype(BF16)
    first = lax.broadcasted_iota(jnp.int32, (L, P2), 1) < N

    def stack(x):
        return jnp.concatenate([jnp.where(first, x, 0.0), jnp.where(first, 0.0, x)], axis=0)

    w_all = w_ref[...]
    cw_all = _dot_exact_lhs(tril, w_all)

    for p in range(d // P2):
        sl = slice(p * P2, (p + 1) * P2)
        cw = cw_all[:, sl]
        cwl = cw[L - 1:L, :]
        dec_in = jnp.exp(cw)
        dec_ex = jnp.exp(cw - w_all[:, sl])
        dec_inv = jnp.exp(-cw)
        dec_rem = jnp.exp(cwl - cw)
        dec_all = jnp.exp(cwl)

        kkr = kk_ref[:, sl]
        kkn = kkr * lax.rsqrt(_dot_exact_rhs(kkr * kkr, ones_blk) + L2_EPS)
        rp = r_ref[:, sl]
        kp = k_ref[:, sl]
        vp = v_ref[:, sl]
        bb = kkn * a_ref[:, sl]

        lhs = jnp.concatenate([stack(-kkn * dec_ex), stack(rp * dec_in)], axis=0)
        rhs = jnp.concatenate([stack(bb * dec_inv), stack(kp * dec_inv)], axis=0)
        m = _dot_nt(lhs, rhs)
        l_ab = jnp.where(strict2, m[:P2, :P2], 0.0)
        l_ak = jnp.where(strict2, m[:P2, P2:], 0.0)
        m_rb = jnp.where(incl2, m[P2:, :P2], 0.0)
        m_rk = jnp.where(incl2, m[P2:, P2:], 0.0)

        s = s_ref[p]
        from_s = _dot_nt(lhs, s)
        vs = stack(vp)
        t_inv = _neumann(l_ab, eye2, L)
        u = _dot(t_inv, from_s[:P2] + _dot(l_ak, vs))
        uv = jnp.concatenate([u, vs], axis=0)
        ys = from_s[P2:] + _dot(jnp.concatenate([m_rb, m_rk], axis=1), uv)
        y = ys[:L] + ys[L:]
        upd = jnp.concatenate([stack(bb * dec_rem), stack(kp * dec_rem)], axis=0)
        s_ref[p] = s * dec_all + _dot_tn(uv, upd)

        mean = _dot_exact_rhs(y, ones_blk) * (1.0 / N)
        cy = y - mean
        var = _dot_exact_rhs(cy * cy, ones_blk) * (1.0 / N)
        yn = cy * lax.rsqrt(var + GN_EPS) * lnw_ref[:, sl] + lnb_ref[:, sl]
        bonus = _dot_exact_rhs(rp * kp * rk_ref[:, sl], ones_blk) * vp
        yo_ref[:, sl] = yn + bonus


def _rwkv_scan(r, k, v, kk, a, w, s0_pairs, rk_row, lnw_row, lnb_row):
    b, tp, d = r.shape
    npairs = d // (2 * RWKV_N)
    tok = pl.BlockSpec((None, CHUNK, d), lambda i, c: (i, c, 0))
    st = pl.BlockSpec((None, npairs, 2 * RWKV_N, 2 * RWKV_N), lambda i, c: (i, 0, 0, 0))
    return pl.pallas_call(
        _rwkv_scan_kernel,
        grid=(b, tp // CHUNK),
        in_specs=[tok] * 6 + [st, _const_spec((1, d)), _const_spec((1, d)), _const_spec((1, d))],
        out_specs=[tok, st],
        out_shape=[jax.ShapeDtypeStruct((b, tp, d), F32),
                   jax.ShapeDtypeStruct((b, npairs, 2 * RWKV_N, 2 * RWKV_N), F32)],
        compiler_params=pltpu.CompilerParams(dimension_semantics=("arbitrary", "arbitrary"),
                                             vmem_limit_bytes=VMEM_LIMIT),
        name="rwkv_scan",
    )(r, k, v, kk, a, w, s0_pairs, rk_row, lnw_row, lnb_row)


def _pad_cols(w, n):
    return jnp.pad(w, ((0, 0), (0, n - w.shape[1])))


def _pad_rows(w, n):
    return jnp.pad(w, ((0, n - w.shape[0]), (0, 0)))


def _pairs_from_state(s):
    b, h, n, _ = s.shape
    s = s.reshape(b, h // 2, 2, n, n)
    z = jnp.zeros_like(s[:, :, 0])
    top = jnp.concatenate([s[:, :, 0], z], axis=-1)
    bot = jnp.concatenate([z, s[:, :, 1]], axis=-1)
    return jnp.concatenate([top, bot], axis=-2)


def _state_from_pairs(sp):
    b, hp, n2, _ = sp.shape
    n = n2 // 2
    return jnp.stack([sp[:, :, :n, :n], sp[:, :, n:, n:]], axis=2).reshape(b, 2 * hp, n, n)


def _run_trunk(h, conv_bufs, gdn_states, shift_bufs, rwkv_states, wts):
    b, tp, d = h.shape
    rows = b * tp
    depth = wts["norm_mix"].shape[0]
    new_conv, new_gdn, new_shift, new_rwkv = [], [], [], []
    out = None
    for i in range(depth):
        j = i // 2
        last = i == depth - 1
        fn_row = wts["norm_final"] if last else None
        if i % 2 == 0:
            g = wts["gdn"][j]
            proj = _norm_proj(h.reshape(rows, d), wts["norm_mix"][i:i + 1], g["w_in"])
            proj3 = proj.reshape(b, tp, -1)
            cbuf8 = jnp.pad(conv_bufs[j], ((0, 0), (SUBLANES - (CONV_W - 1), 0), (0, 0)))
            o, s_new = _gdn_scan(proj3, cbuf8, gdn_states[j], g["conv_w"], g["a_log"], g["dt_bias"], g["o_norm"])
            new_conv.append(proj3[:, tp - (CONV_W - 1):, :GDN_QKV])
            new_gdn.append(s_new)
            h2 = _mix_ffn(o.reshape(rows, d), proj, GDN_QKV // d, h.reshape(rows, d), g["w_out"],
                          wts["norm_ffn"][i:i + 1], wts["ffn_gate"][i], wts["ffn_up"][i], wts["ffn_down"][i],
                          fn_row, silu_gate=True, name="gdn_out_ffn")
        else:
            rw = wts["rwkv"][j]
            vec = jnp.concatenate([wts["norm_mix"][i:i + 1], rw["vec"]], axis=0)
            r, k, v, kk, a, w, gate, last_hn = _rwkv_proj(
                h, shift_bufs[j], vec, rw["mu"], rw["w_r"], rw["w_k"], rw["w_v"],
                rw["w1"], rw["w2"], rw["a1"], rw["a2"], rw["g1"], rw["g2"])
            yo, sp_new = _rwkv_scan(r, k, v, kk, a, w, _pairs_from_state(rwkv_states[j]),
                                    rw["r_k"], rw["ln_w"], rw["ln_b"])
            new_shift.append(last_hn)
            new_rwkv.append(_state_from_pairs(sp_new))
            h2 = _mix_ffn(yo.reshape(rows, d), gate.reshape(rows, d), 0, h.reshape(rows, d), rw["w_o"],
                          wts["norm_ffn"][i:i + 1], wts["ffn_gate"][i], wts["ffn_up"][i], wts["ffn_down"][i],
                          fn_row, silu_gate=False, name="rwkv_out_ffn")
        h = h2.reshape(b, tp, d)
        out = h
    return out, jnp.stack(new_conv), jnp.stack(new_gdn), jnp.stack(new_shift), jnp.stack(new_rwkv)


def _prepare_weights(norm_mix, norm_ffn, norm_final, gdn_w_in, gdn_conv_w, gdn_a_log, gdn_dt_bias, gdn_o_norm,
                     gdn_w_out, rwkv_mu, rwkv_w0, rwkv_w1, rwkv_w2, rwkv_a0, rwkv_a1, rwkv_a2, rwkv_g1, rwkv_g2,
                     rwkv_k_k, rwkv_k_a, rwkv_r_k, rwkv_w_r, rwkv_w_k, rwkv_w_v, rwkv_w_o, rwkv_ln_w,
                     rwkv_ln_b, ffn_w_gate, ffn_w_up, ffn_w_down):
    d = norm_mix.shape[1]
    gdn = []
    for j in range(gdn_w_in.shape[0]):
        w = gdn_w_in[j]
        qkvz = w[:, :GDN_QKV + GDN_KDIM]
        a_cols = _pad_cols(w[:, GDN_QKV + GDN_KDIM:GDN_QKV + GDN_KDIM + GDN_HEADS], LANES)
        b_cols = _pad_cols(w[:, GDN_QKV + GDN_KDIM + GDN_HEADS:], LANES)
        gdn.append(dict(
            w_in=jnp.concatenate([qkvz, a_cols, b_cols], axis=1).astype(BF16),
            conv_w=gdn_conv_w[j],
            a_log=_pad_cols(gdn_a_log[j][None], LANES),
            dt_bias=_pad_cols(gdn_dt_bias[j][None], LANES),
            o_norm=gdn_o_norm[j][None],
            w_out=gdn_w_out[j].astype(BF16),
        ))
    rwkv = []
    for j in range(rwkv_mu.shape[0]):
        lw = -(-rwkv_w1.shape[2] // LANES) * LANES
        la = -(-rwkv_a1.shape[2] // LANES) * LANES
        lg = -(-rwkv_g1.shape[2] // LANES) * LANES
        rwkv.append(dict(
            vec=jnp.stack([rwkv_w0[j], rwkv_a0[j], rwkv_k_k[j], rwkv_k_a[j],
                           jnp.zeros_like(rwkv_w0[j]), jnp.zeros_like(rwkv_w0[j]), jnp.zeros_like(rwkv_w0[j])]),
            mu=_pad_rows(rwkv_mu[j], SUBLANES),
            w_r=rwkv_w_r[j].astype(BF16), w_k=rwkv_w_k[j].astype(BF16), w_v=rwkv_w_v[j].astype(BF16),
            w_o=rwkv_w_o[j].astype(BF16),
            w1=_pad_cols(rwkv_w1[j], lw).astype(BF16), w2=_pad_rows(rwkv_w2[j], lw).astype(BF16),
            a1=_pad_cols(rwkv_a1[j], la).astype(BF16), a2=_pad_rows(rwkv_a2[j], la).astype(BF16),
            g1=_pad_cols(rwkv_g1[j], lg).astype(BF16), g2=_pad_rows(rwkv_g2[j], lg).astype(BF16),
            r_k=rwkv_r_k[j].reshape(1, d), ln_w=rwkv_ln_w[j][None], ln_b=rwkv_ln_b[j][None],
        ))
    return dict(norm_mix=norm_mix, norm_ffn=norm_ffn, norm_final=norm_final[None], gdn=gdn, rwkv=rwkv,
                ffn_gate=ffn_w_gate.astype(BF16), ffn_up=ffn_w_up.astype(BF16), ffn_down=ffn_w_down.astype(BF16))


def kernel(x_prompt, x_sample, cache_gdn_conv, state_gdn, cache_rwkv_shift, state_rwkv, meta_tokens, norm_mix, norm_ffn, norm_final, gdn_w_in, gdn_conv_w, gdn_a_log, gdn_dt_bias, gdn_o_norm, gdn_w_out, rwkv_mu, rwkv_w0, rwkv_w1, rwkv_w2, rwkv_a0, rwkv_a1, rwkv_a2, rwkv_g1, rwkv_g2, rwkv_k_k, rwkv_k_a, rwkv_r_k, rwkv_w_r, rwkv_w_k, rwkv_w_v, rwkv_w_o, rwkv_ln_w, rwkv_ln_b, ffn_w_gate, ffn_w_up, ffn_w_down):
    wts = _prepare_weights(norm_mix, norm_ffn, norm_final, gdn_w_in, gdn_conv_w, gdn_a_log, gdn_dt_bias,
                           gdn_o_norm, gdn_w_out, rwkv_mu, rwkv_w0, rwkv_w1, rwkv_w2, rwkv_a0, rwkv_a1, rwkv_a2,
                           rwkv_g1, rwkv_g2, rwkv_k_k, rwkv_k_a, rwkv_r_k, rwkv_w_r, rwkv_w_k, rwkv_w_v,
                           rwkv_w_o, rwkv_ln_w, rwkv_ln_b, ffn_w_gate, ffn_w_up, ffn_w_down)
    n_gdn = gdn_w_in.shape[0]
    n_rwkv = rwkv_mu.shape[0]
    b, seq, d = x_prompt.shape
    dt = x_prompt.dtype

    lead_pad = (-N_META) % CHUNK
    h_p = jnp.concatenate([jnp.zeros((b, lead_pad, d), dt),
                           jnp.broadcast_to(meta_tokens.astype(dt)[None], (b, N_META, d)), x_prompt], axis=1)
    zero_conv = jnp.zeros((n_gdn, b, CONV_W - 1, GDN_QKV), dt)
    zero_gdn = jnp.zeros((n_gdn, b, GDN_HEADS, GDN_DK, GDN_DV), dt)
    zero_shift = jnp.zeros((n_rwkv, b, 1, d), dt)
    zero_rwkv = jnp.zeros((n_rwkv, b, d // RWKV_N, RWKV_N, RWKV_N), dt)
    out_p, p_conv, p_gdn, p_shift, p_rwkv = _run_trunk(h_p, zero_conv, zero_gdn, zero_shift, zero_rwkv, wts)
    y_prompt = out_p[:, lead_pad + N_META:]

    y_sample, s_conv, s_gdn, s_shift, s_rwkv = _run_trunk(
        x_sample, cache_gdn_conv, state_gdn, cache_rwkv_shift, state_rwkv, wts)

    return (y_prompt, y_sample, p_conv, p_gdn, p_shift, p_rwkv, s_conv, s_gdn, s_shift, s_rwkv)
```

```python
import functools

import jax
import jax.numpy as jnp
from jax import lax
from jax.experimental import pallas as pl
from jax.experimental.pallas import tpu as pltpu

F32 = jnp.float32
BF16 = jnp.bfloat16

CHUNK = 64
N_META = 16
EPS = 1e-6
L2_EPS = 1e-6
GN_EPS = 64e-5

GDN_HEADS = 8
GDN_DK = 128
GDN_DV = 128
GDN_KDIM = GDN_HEADS * GDN_DK
GDN_QKV = 3 * GDN_KDIM
CONV_W = 4
RWKV_N = 64

LANES = 128
SUBLANES = 8
VMEM_LIMIT = 56 * 1024 * 1024

HIGHEST = lax.Precision.HIGHEST


def _dot(a, b):
    return jnp.dot(a.astype(BF16), b.astype(BF16), preferred_element_type=F32)


def _dot_nt(a, b):
    return lax.dot_general(a.astype(BF16), b.astype(BF16), (((1,), (1,)), ((), ())),
                           preferred_element_type=F32)


def _dot_tn(a, b):
    return lax.dot_general(a.astype(BF16), b.astype(BF16), (((0,), (0,)), ((), ())),
                           preferred_element_type=F32)


def _dot_exact_lhs(m01, x):
    hi = x.astype(BF16)
    r1 = x - hi.astype(F32)
    mid = r1.astype(BF16)
    lo = (r1 - mid.astype(F32)).astype(BF16)
    return _dot(m01, hi) + _dot(m01, mid) + _dot(m01, lo)


def _dot_exact_rhs(x, m01):
    hi = x.astype(BF16)
    r1 = x - hi.astype(F32)
    mid = r1.astype(BF16)
    lo = (r1 - mid.astype(F32)).astype(BF16)
    return _dot(hi, m01) + _dot(mid, m01) + _dot(lo, m01)


def _rms(x, g, eps=EPS):
    return x * lax.rsqrt(jnp.mean(x * x, axis=-1, keepdims=True) + eps) * g


def _bdot(a, b):
    return lax.dot_general(a.astype(BF16), b.astype(BF16), (((2,), (1,)), ((0,), (0,))),
                           preferred_element_type=F32)


def _bdot_nt(a, b):
    return lax.dot_general(a.astype(BF16), b.astype(BF16), (((2,), (2,)), ((0,), (0,))),
                           preferred_element_type=F32)


def _bdot_tn(a, b):
    return lax.dot_general(a.astype(BF16), b.astype(BF16), (((1,), (1,)), ((0,), (0,))),
                           preferred_element_type=F32)


def _neumann(x, eye, size):
    t = eye + x
    p = x
    n = 2
    while n < size:
        p = _bdot(p, p)
        t = t + _bdot(t, p)
        n *= 2
    return t


def _pick_tile(n, candidates):
    for c in candidates:
        if n % c == 0:
            return c
    raise ValueError(f"no tile for {n}")


def _const_spec(shape):
    nd = len(shape)
    return pl.BlockSpec(shape, lambda *_: (0,) * nd)


def _norm_proj_kernel(x_ref, g_ref, w_ref, o_ref):
    xn = _rms(x_ref[...], g_ref[...]).astype(BF16)
    o_ref[...] = _dot(xn, w_ref[...])


def _norm_proj(x2d, g_row, w_bf16):
    rows, d = x2d.shape
    n = w_bf16.shape[1]
    tm = _pick_tile(rows, (256, 128, 64))
    return pl.pallas_call(
        _norm_proj_kernel,
        grid=(rows // tm,),
        in_specs=[pl.BlockSpec((tm, d), lambda i: (i, 0)), _const_spec((1, d)), _const_spec((d, n))],
        out_specs=pl.BlockSpec((tm, n), lambda i: (i, 0)),
        out_shape=jax.ShapeDtypeStruct((rows, n), F32),
        compiler_params=pltpu.CompilerParams(dimension_semantics=("arbitrary",), vmem_limit_bytes=VMEM_LIMIT),
        name="gdn_in_proj",
    )(x2d, g_row, w_bf16)


def _gdn_scan_kernel(qkv_ref, a_ref, b_ref, cbuf_ref, s0_ref, cw_ref, alog_ref, dtb_ref, onorm_ref,
                     o_ref, s_ref, xs_ref):
    c = pl.program_id(1)
    L = CHUNK
    hist = SUBLANES

    @pl.when(c == 0)
    def _():
        xs_ref[0:hist, :] = cbuf_ref[...]
        s_ref[...] = s0_ref[...]

    xs_ref[hist:hist + L, :] = qkv_ref[...]

    row = lax.broadcasted_iota(jnp.int32, (L, L), 0)
    col = lax.broadcasted_iota(jnp.int32, (L, L), 1)
    causal = row >= col
    strict = row > col
    tril = causal.astype(BF16)
    eye = (row == col).astype(F32)

    g = -jnp.exp(alog_ref[...]) * jax.nn.softplus(a_ref[...] + dtb_ref[...])
    beta = jax.nn.sigmoid(b_ref[...])
    G = _dot_exact_lhs(tril, g)
    Gt = G.T
    GL = G[L - 1:L, :]
    exp_g = jnp.exp(G)
    exp_gr = jnp.exp(GL - G)
    exp_gl = jnp.exp(GL)

    base = hist - (CONV_W - 1)
    acc = cw_ref[0:1, :] * xs_ref[base:base + L, :]
    for j in range(1, CONV_W):
        acc = acc + cw_ref[j:j + 1, :] * xs_ref[base + j:base + j + L, :]
    act = jax.nn.silu(acc)

    heads = range(GDN_HEADS)

    def per_head(x2d, off, width):
        return jnp.stack([x2d[:, off + h * width:off + (h + 1) * width] for h in heads])

    def head_cols(x2d):
        return jnp.stack([x2d[:, h:h + 1] for h in heads])

    q = per_head(act, 0, GDN_DK)
    k = per_head(act, GDN_KDIM, GDN_DK)
    v = per_head(act, 2 * GDN_KDIM, GDN_DV)
    q = q * lax.rsqrt(jnp.sum(q * q, axis=-1, keepdims=True) + L2_EPS) * (GDN_DK ** -0.5)
    k = k * lax.rsqrt(jnp.sum(k * k, axis=-1, keepdims=True) + L2_EPS)
    bcol = head_cols(beta)
    g_col = head_cols(G)
    g_row = jnp.stack([Gt[h:h + 1, :] for h in heads])
    eg_col = head_cols(exp_g)
    egr_col = head_cols(exp_gr)
    egl = jnp.stack([exp_gl[:, h:h + 1] for h in heads])

    dlt = g_col - g_row
    dec = jnp.where(causal, jnp.exp(jnp.where(causal, dlt, 0.0)), 0.0)
    kk = _bdot_nt(k, k)
    a_mat = jnp.where(strict, bcol * kk * dec, 0.0)
    t_inv = _neumann(-a_mat, eye, L)
    rhs = jnp.concatenate([v * bcol, k * (bcol * eg_col)], axis=-1)
    sol = _bdot(t_inv, rhs)
    u = sol[..., :GDN_DV]
    w = sol[..., GDN_DV:]
    qk = _bdot_nt(q, k) * dec
    s = s_ref[...]
    v_new = u - _bdot(w, s)
    o = _bdot(q * eg_col, s) + _bdot(qk, v_new)
    s_ref[...] = s * egl + _bdot_tn(k * egr_col, v_new)
    o = o * lax.rsqrt(jnp.mean(o * o, axis=-1, keepdims=True) + EPS) * onorm_ref[...]
    for h in heads:
        o_ref[:, h * GDN_DV:(h + 1) * GDN_DV] = o[h]

    xs_ref[0:hist, :] = xs_ref[L:L + hist, :]


def _gdn_scan(proj3, cbuf8, s0, conv_w, alog_row, dtb_row, onorm_row):
    b, tp, _ = proj3.shape
    nc = tp // CHUNK
    a_blk = (GDN_QKV + GDN_KDIM) // LANES
    return pl.pallas_call(
        _gdn_scan_kernel,
        grid=(b, nc),
        in_specs=[
            pl.BlockSpec((None, CHUNK, GDN_QKV), lambda i, c: (i, c, 0)),
            pl.BlockSpec((None, CHUNK, LANES), lambda i, c: (i, c, a_blk)),
            pl.BlockSpec((None, CHUNK, LANES), lambda i, c: (i, c, a_blk + 1)),
            pl.BlockSpec((None, SUBLANES, GDN_QKV), lambda i, c: (i, 0, 0)),
            pl.BlockSpec((None, GDN_HEADS, GDN_DK, GDN_DV), lambda i, c: (i, 0, 0, 0)),
            _const_spec((CONV_W, GDN_QKV)),
            _const_spec((1, LANES)),
            _const_spec((1, LANES)),
            _const_spec((1, GDN_DV)),
        ],
        out_specs=[
            pl.BlockSpec((None, CHUNK, GDN_KDIM), lambda i, c: (i, c, 0)),
            pl.BlockSpec((None, GDN_HEADS, GDN_DK, GDN_DV), lambda i, c: (i, 0, 0, 0)),
        ],
        out_shape=[
            jax.ShapeDtypeStruct((b, tp, GDN_KDIM), F32),
            jax.ShapeDtypeStruct((b, GDN_HEADS, GDN_DK, GDN_DV), F32),
        ],
        scratch_shapes=[pltpu.VMEM((CHUNK + 2 * SUBLANES, GDN_QKV), F32)],
        compiler_params=pltpu.CompilerParams(dimension_semantics=("arbitrary", "arbitrary"),
                                             vmem_limit_bytes=VMEM_LIMIT),
        name="gdn_scan",
    )(proj3, proj3, proj3, cbuf8, s0, conv_w, alog_row, dtb_row, onorm_row)


def _mix_ffn_kernel(*refs, silu_gate, final_norm):
    if final_norm:
        y_ref, gm_ref, h_ref, wo_ref, nf_ref, wg_ref, wu_ref, wd_ref, fn_ref, o_ref = refs
    else:
        y_ref, gm_ref, h_ref, wo_ref, nf_ref, wg_ref, wu_ref, wd_ref, o_ref = refs
    gm = gm_ref[...]
    if silu_gate:
        gm = jax.nn.silu(gm)
    h1 = h_ref[...] + _dot((y_ref[...] * gm).astype(BF16), wo_ref[...])
    n = _rms(h1, nf_ref[...]).astype(BF16)
    act = (jax.nn.silu(_dot(n, wg_ref[...])) * _dot(n, wu_ref[...])).astype(BF16)
    h2 = h1 + _dot(act, wd_ref[...])
    if final_norm:
        h2 = _rms(h2, fn_ref[...])
    o_ref[...] = h2


def _mix_ffn(y2d, gm2d, gm_blk, h2d, wo, nf_row, wg, wu, wd, fn_row, *, silu_gate, name):
    rows, d = h2d.shape
    dff = wg.shape[1]
    tm = _pick_tile(rows, (256, 128, 64))
    final_norm = fn_row is not None
    in_specs = [
        pl.BlockSpec((tm, d), lambda i: (i, 0)),
        pl.BlockSpec((tm, d), lambda i: (i, gm_blk)),
        pl.BlockSpec((tm, d), lambda i: (i, 0)),
        _const_spec((d, d)), _const_spec((1, d)),
        _const_spec((d, dff)), _const_spec((d, dff)), _const_spec((dff, d)),
    ]
    args = [y2d, gm2d, h2d, wo, nf_row, wg, wu, wd]
    if final_norm:
        in_specs.append(_const_spec((1, d)))
        args.append(fn_row)
    return pl.pallas_call(
        functools.partial(_mix_ffn_kernel, silu_gate=silu_gate, final_norm=final_norm),
        grid=(rows // tm,),
        in_specs=in_specs,
        out_specs=pl.BlockSpec((tm, d), lambda i: (i, 0)),
        out_shape=jax.ShapeDtypeStruct((rows, d), F32),
        compiler_params=pltpu.CompilerParams(dimension_semantics=("arbitrary",), vmem_limit_bytes=VMEM_LIMIT),
        name=name,
    )(*args)


def _rwkv_proj_kernel(h_ref, sb_ref, vec_ref, mu_ref, wr_ref, wk_ref, wv_ref,
                      w1_ref, w2_ref, a1_ref, a2_ref, g1_ref, g2_ref,
                      r_o, k_o, v_o, kk_o, a_o, w_o, g_o, last_o, hs_ref):
    t = pl.program_id(1)
    tt = h_ref.shape[0]
    hist = SUBLANES
    hn = _rms(h_ref[...], vec_ref[0:1, :])

    @pl.when(t == 0)
    def _():
        hs_ref[hist - 1:hist, :] = sb_ref[...]

    hs_ref[hist:hist + tt, :] = hn
    prev = hs_ref[hist - 1:hist - 1 + tt, :]
    hs_ref[hist - 1:hist, :] = hn[tt - 1:tt, :]
    last_o[...] = hn[tt - 1:tt, :]
    xx = prev - hn

    def mix(i):
        return (hn + xx * mu_ref[i:i + 1, :]).astype(BF16)

    r = _dot(mix(0), wr_ref[...])
    k = _dot(mix(2), wk_ref[...])
    v = _dot(mix(3), wv_ref[...])
    wl = vec_ref[1:2, :] + _dot(jnp.tanh(_dot(mix(1), w1_ref[...])).astype(BF16), w2_ref[...])
    w_log = -jax.nn.softplus(-wl) - 0.5
    a = jax.nn.sigmoid(vec_ref[2:3, :] + _dot(_dot(mix(4), a1_ref[...]).astype(BF16), a2_ref[...]))
    gate = _dot(jax.nn.sigmoid(_dot(mix(5), g1_ref[...])).astype(BF16), g2_ref[...])
    r_o[...] = r
    k_o[...] = k * (1.0 + (a - 1.0) * vec_ref[4:5, :])
    v_o[...] = v
    kk_o[...] = k * vec_ref[3:4, :]
    a_o[...] = a
    w_o[...] = -jnp.exp(w_log)
    g_o[...] = gate


def _rwkv_proj(h3, sbuf, vec, mu8, wr, wk, wv, w1, w2, a1, a2, g1, g2):
    b, tp, d = h3.shape
    tt = _pick_tile(tp, (256, 192, 128, 64))
    tok = pl.BlockSpec((None, tt, d), lambda i, t: (i, t, 0))
    weights = [wr, wk, wv, w1, w2, a1, a2, g1, g2]
    return pl.pallas_call(
        _rwkv_proj_kernel,
        grid=(b, tp // tt),
        in_specs=[tok, pl.BlockSpec((None, 1, d), lambda i, t: (i, 0, 0)),
                  _const_spec(vec.shape), _const_spec(mu8.shape)] + [_const_spec(w.shape) for w in weights],
        out_specs=[tok] * 7 + [pl.BlockSpec((None, 1, d), lambda i, t: (i, 0, 0))],
        out_shape=[jax.ShapeDtypeStruct((b, tp, d), F32)] * 7 + [jax.ShapeDtypeStruct((b, 1, d), F32)],
        scratch_shapes=[pltpu.VMEM((tt + SUBLANES, d), F32)],
        compiler_params=pltpu.CompilerParams(dimension_semantics=("arbitrary", "arbitrary"),
                                             vmem_limit_bytes=VMEM_LIMIT),
        name="rwkv_proj",
    )(h3, sbuf, vec, mu8, *weights)


def _rwkv_scan_kernel(r_ref, k_ref, v_ref, kk_ref, a_ref, w_ref, s0_ref, rk_ref, lnw_ref, lnb_ref,
                      yo_ref, s_ref):
    c = pl.program_id(1)
    L = CHUNK
    N = RWKV_N
    P2 = 2 * N
    d = r_ref.shape[1]

    @pl.when(c == 0)
    def _():
        s_ref[...] = s0_ref[...]

    row = lax.broadcasted_iota(jnp.int32, (L, L), 0)
    col = lax.broadcasted_iota(jnp.int32, (L, L), 1)
    tril = (row >= col).astype(BF16)
    r2 = lax.broadcasted_iota(jnp.int32, (P2, P2), 0)
    c2 = lax.broadcasted_iota(jnp.int32, (P2, P2), 1)
    same = (r2 < N) == (c2 < N)
    strict2 = same & ((r2 & (N - 1)) > (c2 & (N - 1)))
    incl2 = same & ((r2 & (N - 1)) >= (c2 & (N - 1)))
    eye2 = (r2 == c2).astype(F32)
    ones_blk = same.astype(BF16)
    first = lax.broadcasted_iota(jnp.int32, (L, P2), 1) < N

    npairs = d // P2
    pairs = range(npairs)

    def per_pair(x2d):
        return jnp.stack([x2d[:, p * P2:(p + 1) * P2] for p in pairs])

    def lane_sums(x3):
        n, rws, _ = x3.shape
        return _dot_exact_rhs(x3.reshape(n * rws, P2), ones_blk).reshape(n, rws, P2)

    w_all = w_ref[...]
    cw_all = _dot_exact_lhs(tril, w_all)
    cwl = cw_all[L - 1:L, :]
    dec_in = jnp.exp(cw_all)
    dec_ex = jnp.exp(cw_all - w_all)
    dec_inv = jnp.exp(-cw_all)
    dec_rem = jnp.exp(cwl - cw_all)
    dec_all = jnp.exp(cwl)

    kkr = per_pair(kk_ref[...])
    kkn = kkr * lax.rsqrt(lane_sums(kkr * kkr) + L2_EPS)
    rp = per_pair(r_ref[...])
    kp = per_pair(k_ref[...])
    vp = per_pair(v_ref[...])
    bb = kkn * per_pair(a_ref[...])
    p_in = per_pair(dec_in)
    p_ex = per_pair(dec_ex)
    p_inv = per_pair(dec_inv)
    p_rem = per_pair(dec_rem)
    p_all = jnp.stack([dec_all[:, p * P2:(p + 1) * P2] for p in pairs])

    def stack(x):
        return jnp.concatenate([jnp.where(first, x, 0.0), jnp.where(first, 0.0, x)], axis=1)

    lhs = jnp.concatenate([stack(-kkn * p_ex), stack(rp * p_in)], axis=1)
    rhs = jnp.concatenate([stack(bb * p_inv), stack(kp * p_inv)], axis=1)
    m = _bdot_nt(lhs, rhs)
    l_ab = jnp.where(strict2, m[:, :P2, :P2], 0.0)
    l_ak = jnp.where(strict2, m[:, :P2, P2:], 0.0)
    m_rb = jnp.where(incl2, m[:, P2:, :P2], 0.0)
    m_rk = jnp.where(incl2, m[:, P2:, P2:], 0.0)

    s = s_ref[...]
    from_s = _bdot_nt(lhs, s)
    vs = stack(vp)
    t_inv = _neumann(l_ab, eye2, L)
    u = _bdot(t_inv, from_s[:, :P2] + _bdot(l_ak, vs))
    uv = jnp.concatenate([u, vs], axis=1)
    ys = from_s[:, P2:] + _bdot(jnp.concatenate([m_rb, m_rk], axis=2), uv)
    y = ys[:, :L] + ys[:, L:]
    upd = jnp.concatenate([stack(bb * p_rem), stack(kp * p_rem)], axis=1)
    s_ref[...] = s * p_all + _bdot_tn(uv, upd)

    mean = lane_sums(y) * (1.0 / N)
    cy = y - mean
    var = lane_sums(cy * cy) * (1.0 / N)
    rk3 = jnp.stack([rk_ref[:, p * P2:(p + 1) * P2] for p in pairs])
    bonus = lane_sums(rp * kp * rk3) * vp
    for p in pairs:
        sl = slice(p * P2, (p + 1) * P2)
        yn = cy[p] * lax.rsqrt(var[p] + GN_EPS) * lnw_ref[:, sl] + lnb_ref[:, sl]
        yo_ref[:, sl] = yn + bonus[p]


def _rwkv_scan(r, k, v, kk, a, w, s0_pairs, rk_row, lnw_row, lnb_row):
    b, tp, d = r.shape
    npairs = d // (2 * RWKV_N)
    tok = pl.BlockSpec((None, CHUNK, d), lambda i, c: (i, c, 0))
    st = pl.BlockSpec((None, npairs, 2 * RWKV_N, 2 * RWKV_N), lambda i, c: (i, 0, 0, 0))
    return pl.pallas_call(
        _rwkv_scan_kernel,
        grid=(b, tp // CHUNK),
        in_specs=[tok] * 6 + [st, _const_spec((1, d)), _const_spec((1, d)), _const_spec((1, d))],
        out_specs=[tok, st],
        out_shape=[jax.ShapeDtypeStruct((b, tp, d), F32),
                   jax.ShapeDtypeStruct((b, npairs, 2 * RWKV_N, 2 * RWKV_N), F32)],
        compiler_params=pltpu.CompilerParams(dimension_semantics=("arbitrary", "arbitrary"),
                                             vmem_limit_bytes=VMEM_LIMIT),
        name="rwkv_scan",
    )(r, k, v, kk, a, w, s0_pairs, rk_row, lnw_row, lnb_row)


def _pad_cols(w, n):
    return jnp.pad(w, ((0, 0), (0, n - w.shape[1])))


def _pad_rows(w, n):
    return jnp.pad(w, ((0, n - w.shape[0]), (0, 0)))


def _pairs_from_state(s):
    b, h, n, _ = s.shape
    s = s.reshape(b, h // 2, 2, n, n)
    z = jnp.zeros_like(s[:, :, 0])
    top = jnp.concatenate([s[:, :, 0], z], axis=-1)
    bot = jnp.concatenate([z, s[:, :, 1]], axis=-1)
    return jnp.concatenate([top, bot], axis=-2)


def _state_from_pairs(sp):
    b, hp, n2, _ = sp.shape
    n = n2 // 2
    return jnp.stack([sp[:, :, :n, :n], sp[:, :, n:, n:]], axis=2).reshape(b, 2 * hp, n, n)


def _run_trunk(h, conv_bufs, gdn_states, shift_bufs, rwkv_states, wts):
    b, tp, d = h.shape
    rows = b * tp
    depth = wts["norm_mix"].shape[0]
    new_conv, new_gdn, new_shift, new_rwkv = [], [], [], []
    out = None
    for i in range(depth):
        j = i // 2
        last = i == depth - 1
        fn_row = wts["norm_final"] if last else None
        if i % 2 == 0:
            g = wts["gdn"][j]
            proj = _norm_proj(h.reshape(rows, d), wts["norm_mix"][i:i + 1], g["w_in"])
            proj3 = proj.reshape(b, tp, -1)
            cbuf8 = jnp.pad(conv_bufs[j], ((0, 0), (SUBLANES - (CONV_W - 1), 0), (0, 0)))
            o, s_new = _gdn_scan(proj3, cbuf8, gdn_states[j], g["conv_w"], g["a_log"], g["dt_bias"], g["o_norm"])
            new_conv.append(proj3[:, tp - (CONV_W - 1):, :GDN_QKV])
            new_gdn.append(s_new)
            h2 = _mix_ffn(o.reshape(rows, d), proj, GDN_QKV // d, h.reshape(rows, d), g["w_out"],
                          wts["norm_ffn"][i:i + 1], wts["ffn_gate"][i], wts["ffn_up"][i], wts["ffn_down"][i],
                          fn_row, silu_gate=True, name="gdn_out_ffn")
        else:
            rw = wts["rwkv"][j]
            vec = jnp.concatenate([wts["norm_mix"][i:i + 1], rw["vec"]], axis=0)
            r, k, v, kk, a, w, gate, last_hn = _rwkv_proj(
                h, shift_bufs[j], vec, rw["mu"], rw["w_r"], rw["w_k"], rw["w_v"],
                rw["w1"], rw["w2"], rw["a1"], rw["a2"], rw["g1"], rw["g2"])
            yo, sp_new = _rwkv_scan(r, k, v, kk, a, w, _pairs_from_state(rwkv_states[j]),
                                    rw["r_k"], rw["ln_w"], rw["ln_b"])
            new_shift.append(last_hn)
            new_rwkv.append(_state_from_pairs(sp_new))
            h2 = _mix_ffn(yo.reshape(rows, d), gate.reshape(rows, d), 0, h.reshape(rows, d), rw["w_o"],
                          wts["norm_ffn"][i:i + 1], wts["ffn_gate"][i], wts["ffn_up"][i], wts["ffn_down"][i],
                          fn_row, silu_gate=False, name="rwkv_out_ffn")
        h = h2.reshape(b, tp, d)
        out = h
    return out, jnp.stack(new_conv), jnp.stack(new_gdn), jnp.stack(new_shift), jnp.stack(new_rwkv)


def _prepare_weights(norm_mix, norm_ffn, norm_final, gdn_w_in, gdn_conv_w, gdn_a_log, gdn_dt_bias, gdn_o_norm,
                     gdn_w_out, rwkv_mu, rwkv_w0, rwkv_w1, rwkv_w2, rwkv_a0, rwkv_a1, rwkv_a2, rwkv_g1, rwkv_g2,
                     rwkv_k_k, rwkv_k_a, rwkv_r_k, rwkv_w_r, rwkv_w_k, rwkv_w_v, rwkv_w_o, rwkv_ln_w,
                     rwkv_ln_b, ffn_w_gate, ffn_w_up, ffn_w_down):
    d = norm_mix.shape[1]
    gdn = []
    for j in range(gdn_w_in.shape[0]):
        w = gdn_w_in[j]
        qkvz = w[:, :GDN_QKV + GDN_KDIM]
        a_cols = _pad_cols(w[:, GDN_QKV + GDN_KDIM:GDN_QKV + GDN_KDIM + GDN_HEADS], LANES)
        b_cols = _pad_cols(w[:, GDN_QKV + GDN_KDIM + GDN_HEADS:], LANES)
        gdn.append(dict(
            w_in=jnp.concatenate([qkvz, a_cols, b_cols], axis=1).astype(BF16),
            conv_w=gdn_conv_w[j],
            a_log=_pad_cols(gdn_a_log[j][None], LANES),
            dt_bias=_pad_cols(gdn_dt_bias[j][None], LANES),
            o_norm=gdn_o_norm[j][None],
            w_out=gdn_w_out[j].astype(BF16),
        ))
    rwkv = []
    for j in range(rwkv_mu.shape[0]):
        lw = -(-rwkv_w1.shape[2] // LANES) * LANES
        la = -(-rwkv_a1.shape[2] // LANES) * LANES
        lg = -(-rwkv_g1.shape[2] // LANES) * LANES
        rwkv.append(dict(
            vec=jnp.stack([rwkv_w0[j], rwkv_a0[j], rwkv_k_k[j], rwkv_k_a[j],
                           jnp.zeros_like(rwkv_w0[j]), jnp.zeros_like(rwkv_w0[j]), jnp.zeros_like(rwkv_w0[j])]),
            mu=_pad_rows(rwkv_mu[j], SUBLANES),
            w_r=rwkv_w_r[j].astype(BF16), w_k=rwkv_w_k[j].astype(BF16), w_v=rwkv_w_v[j].astype(BF16),
            w_o=rwkv_w_o[j].astype(BF16),
            w1=_pad_cols(rwkv_w1[j], lw).astype(BF16), w2=_pad_rows(rwkv_w2[j], lw).astype(BF16),
            a1=_pad_cols(rwkv_a1[j], la).astype(BF16), a2=_pad_rows(rwkv_a2[j], la).astype(BF16),
            g1=_pad_cols(rwkv_g1[j], lg).astype(BF16), g2=_pad_rows(rwkv_g2[j], lg).astype(BF16),
            r_k=rwkv_r_k[j].reshape(1, d), ln_w=rwkv_ln_w[j][None], ln_b=rwkv_ln_b[j][None],
        ))
    return dict(norm_mix=norm_mix, norm_ffn=norm_ffn, norm_final=norm_final[None], gdn=gdn, rwkv=rwkv,
                ffn_gate=ffn_w_gate.astype(BF16), ffn_up=ffn_w_up.astype(BF16), ffn_down=ffn_w_down.astype(BF16))


def kernel(x_prompt, x_sample, cache_gdn_conv, state_gdn, cache_rwkv_shift, state_rwkv, meta_tokens, norm_mix, norm_ffn, norm_final, gdn_w_in, gdn_conv_w, gdn_a_log, gdn_dt_bias, gdn_o_norm, gdn_w_out, rwkv_mu, rwkv_w0, rwkv_w1, rwkv_w2, rwkv_a0, rwkv_a1, rwkv_a2, rwkv_g1, rwkv_g2, rwkv_k_k, rwkv_k_a, rwkv_r_k, rwkv_w_r, rwkv_w_k, rwkv_w_v, rwkv_w_o, rwkv_ln_w, rwkv_ln_b, ffn_w_gate, ffn_w_up, ffn_w_down):
    wts = _prepare_weights(norm_mix, norm_ffn, norm_final, gdn_w_in, gdn_conv_w, gdn_a_log, gdn_dt_bias,
                           gdn_o_norm, gdn_w_out, rwkv_mu, rwkv_w0, rwkv_w1, rwkv_w2, rwkv_a0, rwkv_a1, rwkv_a2,
                           rwkv_g1, rwkv_g2, rwkv_k_k, rwkv_k_a, rwkv_r_k, rwkv_w_r, rwkv_w_k, rwkv_w_v,
                           rwkv_w_o, rwkv_ln_w, rwkv_ln_b, ffn_w_gate, ffn_w_up, ffn_w_down)
    n_gdn = gdn_w_in.shape[0]
    n_rwkv = rwkv_mu.shape[0]
    b, seq, d = x_prompt.shape
    dt = x_prompt.dtype

    lead_pad = (-N_META) % CHUNK
    h_p = jnp.concatenate([jnp.zeros((b, lead_pad, d), dt),
                           jnp.broadcast_to(meta_tokens.astype(dt)[None], (b, N_META, d)), x_prompt], axis=1)
    zero_conv = jnp.zeros((n_gdn, b, CONV_W - 1, GDN_QKV), dt)
    zero_gdn = jnp.zeros((n_gdn, b, GDN_HEADS, GDN_DK, GDN_DV), dt)
    zero_shift = jnp.zeros((n_rwkv, b, 1, d), dt)
    zero_rwkv = jnp.zeros((n_rwkv, b, d // RWKV_N, RWKV_N, RWKV_N), dt)
    out_p, p_conv, p_gdn, p_shift, p_rwkv = _run_trunk(h_p, zero_conv, zero_gdn, zero_shift, zero_rwkv, wts)
    y_prompt = out_p[:, lead_pad + N_META:]

    y_sample, s_conv, s_gdn, s_shift, s_rwkv = _run_trunk(
        x_sample, cache_gdn_conv, state_gdn, cache_rwkv_shift, state_rwkv, wts)

    return (y_prompt, y_sample, p_conv, p_gdn, p_shift, p_rwkv, s_conv, s_gdn, s_shift, s_rwkv)
```

```python
import functools

import jax
import jax.numpy as jnp
from jax import lax
from jax.experimental import pallas as pl
from jax.experimental.pallas import tpu as pltpu

F32 = jnp.float32
BF16 = jnp.bfloat16

CHUNK = 64
N_META = 16
EPS = 1e-6
L2_EPS = 1e-6
GN_EPS = 64e-5

GDN_HEADS = 8
GDN_DK = 128
GDN_DV = 128
GDN_KDIM = GDN_HEADS * GDN_DK
GDN_QKV = 3 * GDN_KDIM
CONV_W = 4
RWKV_N = 64
DECAY_SCALE = 0.6065306597126334

LANES = 128
SUBLANES = 8
VMEM_LIMIT = 56 * 1024 * 1024


def _dot(a, b):
    return jnp.dot(a.astype(BF16), b.astype(BF16), preferred_element_type=F32)


def _bdot(a, b):
    return lax.dot_general(a.astype(BF16), b.astype(BF16), (((2,), (1,)), ((0,), (0,))),
                           preferred_element_type=F32)


def _bdot_nt(a, b):
    return lax.dot_general(a.astype(BF16), b.astype(BF16), (((2,), (2,)), ((0,), (0,))),
                           preferred_element_type=F32)


def _bdot_tn(a, b):
    return lax.dot_general(a.astype(BF16), b.astype(BF16), (((1,), (1,)), ((0,), (0,))),
                           preferred_element_type=F32)


def _split_bf16(x, parts):
    out = []
    for _ in range(parts - 1):
        hi = x.astype(BF16)
        out.append(hi)
        x = x - hi.astype(F32)
    out.append(x.astype(BF16))
    return out


def _dot_exact_lhs(m01, x, parts=3):
    return sum(_dot(m01, t) for t in _split_bf16(x, parts))


def _rms(x, g, eps=EPS):
    return x * lax.rsqrt(jnp.mean(x * x, axis=-1, keepdims=True) + eps) * g


def _neumann(x, eye, size):
    rows = x.shape[1]
    t = eye + x
    p = _bdot(x, x)
    n = 4
    while n < size:
        both = _bdot(jnp.concatenate([p, t], axis=1), p)
        t = t + both[:, rows:]
        p = both[:, :rows]
        n *= 2
    return t + _bdot(t, p)


def _pick_tile(n, candidates):
    for c in candidates:
        if n % c == 0:
            return c
    raise ValueError(f"no tile for {n}")


def _const_spec(shape):
    nd = len(shape)
    return pl.BlockSpec(shape, lambda *_: (0,) * nd)


def _gdn_proj_kernel(x_ref, g_ref, w_ref, cw_ref, cbuf_ref, q_o, k_o, v_o, z_o, ab_o, cst_o, xs_ref):
    t = pl.program_id(1)
    tt = x_ref.shape[0]
    hist = SUBLANES

    @pl.when(t == 0)
    def _():
        xs_ref[0:hist, :] = cbuf_ref[...]

    xn = _rms(x_ref[...], g_ref[...]).astype(BF16)
    proj = _dot(xn, w_ref[...])
    xs_ref[hist:hist + tt, :] = proj[:, :GDN_QKV]
    z_o[...] = proj[:, GDN_QKV:GDN_QKV + GDN_KDIM].astype(z_o.dtype)
    ab_o[...] = proj[:, GDN_QKV + GDN_KDIM:]
    cst_o[...] = proj[tt - hist:tt, :GDN_QKV]

    win = xs_ref[...]
    win1 = pltpu.roll(win, 1, 0)
    near = cw_ref[3:4, :] * win + cw_ref[2:3, :] * win1
    far = cw_ref[1:2, :] * win + cw_ref[0:1, :] * win1
    acc = (near + pltpu.roll(far, 2, 0))[hist:]
    xs_ref[0:hist, :] = win[tt:tt + hist]
    act = jax.nn.silu(acc)
    q_o[...] = act[:, :GDN_KDIM].astype(q_o.dtype)
    k_o[...] = act[:, GDN_KDIM:2 * GDN_KDIM].astype(k_o.dtype)
    v_o[...] = act[:, 2 * GDN_KDIM:].astype(v_o.dtype)


def _gdn_proj(h3, g_row, w_bf16, conv_w, cbuf8):
    b, tp, d = h3.shape
    n = w_bf16.shape[1]
    tt = _pick_tile(tp, (256, 192, 128, 64))
    tok = lambda width: pl.BlockSpec((None, tt, width), lambda i, t: (i, t, 0))
    hist_spec = pl.BlockSpec((None, SUBLANES, GDN_QKV), lambda i, t: (i, 0, 0))
    return pl.pallas_call(
        _gdn_proj_kernel,
        grid=(b, tp // tt),
        in_specs=[tok(d), _const_spec((1, d)), _const_spec((d, n)), _const_spec((CONV_W, GDN_QKV)), hist_spec],
        out_specs=[tok(GDN_KDIM), tok(GDN_KDIM), tok(GDN_KDIM), tok(GDN_KDIM), tok(2 * LANES), hist_spec],
        out_shape=[jax.ShapeDtypeStruct((b, tp, GDN_KDIM), BF16)] * 4
        + [jax.ShapeDtypeStruct((b, tp, 2 * LANES), F32), jax.ShapeDtypeStruct((b, SUBLANES, GDN_QKV), F32)],
        scratch_shapes=[pltpu.VMEM((tt + SUBLANES, GDN_QKV), F32)],
        compiler_params=pltpu.CompilerParams(dimension_semantics=("arbitrary", "arbitrary"),
                                             vmem_limit_bytes=VMEM_LIMIT),
        name="gdn_in_proj",
    )(h3, g_row, w_bf16, conv_w, cbuf8)


def _gdn_scan_kernel(q_ref, k_ref, v_ref, ab_ref, s0_ref, alog_ref, dtb_ref, onorm_ref, o_ref, s_ref):
    c = pl.program_id(1)
    L = CHUNK

    @pl.when(c == 0)
    def _():
        s_ref[...] = s0_ref[...]

    row = lax.broadcasted_iota(jnp.int32, (L, L), 0)
    col = lax.broadcasted_iota(jnp.int32, (L, L), 1)
    causal = row >= col
    strict = row > col
    tril = causal.astype(BF16)
    eye = (row == col).astype(F32)

    nb = q_ref.shape[0]
    a_in = jnp.concatenate([ab_ref[i, :, :LANES] + dtb_ref[...] for i in range(nb)], axis=1)
    b_in = jnp.concatenate([ab_ref[i, :, LANES:] for i in range(nb)], axis=1)
    neg_a = jnp.concatenate([-jnp.exp(alog_ref[...])] * nb, axis=1)
    g = neg_a * jax.nn.softplus(a_in)
    beta = jax.nn.sigmoid(b_in)
    G = _dot_exact_lhs(tril, g)
    Gt = G.T
    GL = G[L - 1:L, :]
    exp_g = jnp.exp(G)
    exp_gr = jnp.exp(GL - G)
    exp_gl = jnp.exp(GL)

    chains = [(i, h) for i in range(nb) for h in range(GDN_HEADS)]

    def per_head(ref):
        return jnp.stack([ref[i, :, h * GDN_DK:(h + 1) * GDN_DK] for i, h in chains])

    def head_cols(x2d):
        return jnp.stack([x2d[:, i * LANES + h:i * LANES + h + 1] for i, h in chains])

    q = per_head(q_ref).astype(F32)
    kf = per_head(k_ref).astype(F32)
    v = per_head(v_ref).astype(F32)
    q = q * (lax.rsqrt(jnp.sum(q * q, axis=-1, keepdims=True) + L2_EPS) * (GDN_DK ** -0.5))
    kf = kf * lax.rsqrt(jnp.sum(kf * kf, axis=-1, keepdims=True) + L2_EPS)
    k = kf.astype(BF16)
    bcol = head_cols(beta)
    g_col = head_cols(G)
    g_row = jnp.stack([Gt[i * LANES + h:i * LANES + h + 1, :] for i, h in chains])
    eg_col = head_cols(exp_g)
    egr_col = head_cols(exp_gr)
    egl = jnp.stack([exp_gl[:, i * LANES + h:i * LANES + h + 1] for i, h in chains])

    dlt = g_col - g_row
    dec = jnp.where(causal, jnp.exp(jnp.where(causal, dlt, 0.0)), 0.0)
    kq_k = _bdot_nt(jnp.concatenate([k, q.astype(BF16)], axis=1), k)
    a_mat = jnp.where(strict, bcol * kq_k[:, :L] * dec, 0.0)
    qk = kq_k[:, L:] * dec
    t_inv = _neumann(-a_mat, eye, L)
    rhs = jnp.concatenate([v * bcol, kf * (bcol * eg_col)], axis=-1)
    sol = _bdot(t_inv, rhs)
    u = sol[..., :GDN_DV]
    w = sol[..., GDN_DV:]
    s = s_ref[...].reshape(nb * GDN_HEADS, GDN_DK, GDN_DV)
    ws_qs = _bdot(jnp.concatenate([w, q * eg_col], axis=1), s)
    v_new = u - ws_qs[:, :L]
    o = ws_qs[:, L:] + _bdot(qk, v_new)
    s_ref[...] = (s * egl + _bdot_tn(kf * egr_col, v_new)).reshape(s_ref.shape)
    o = o * lax.rsqrt(jnp.mean(o * o, axis=-1, keepdims=True) + EPS) * onorm_ref[...]
    for n, (i, h) in enumerate(chains):
        o_ref[i, :, h * GDN_DV:(h + 1) * GDN_DV] = o[n].astype(o_ref.dtype)


def _scan_batch(b):
    return 2 if b % 2 == 0 else 1


def _gdn_scan(q, k, v, ab, s0, alog_row, dtb_row, onorm_row):
    b, tp, _ = q.shape
    nb = _scan_batch(b)
    tok = pl.BlockSpec((nb, CHUNK, GDN_KDIM), lambda i, c: (i, c, 0))
    st = pl.BlockSpec((nb, GDN_HEADS, GDN_DK, GDN_DV), lambda i, c: (i, 0, 0, 0))
    return pl.pallas_call(
        _gdn_scan_kernel,
        grid=(b // nb, tp // CHUNK),
        in_specs=[tok, tok, tok, pl.BlockSpec((nb, CHUNK, 2 * LANES), lambda i, c: (i, c, 0)), st,
                  _const_spec((1, LANES)), _const_spec((1, LANES)), _const_spec((1, GDN_DV))],
        out_specs=[tok, st],
        out_shape=[jax.ShapeDtypeStruct((b, tp, GDN_KDIM), BF16),
                   jax.ShapeDtypeStruct((b, GDN_HEADS, GDN_DK, GDN_DV), F32)],
        compiler_params=pltpu.CompilerParams(dimension_semantics=("arbitrary", "arbitrary"),
                                             vmem_limit_bytes=VMEM_LIMIT),
        name="gdn_scan",
    )(q, k, v, ab, s0, alog_row, dtb_row, onorm_row)


def _mix_ffn_kernel(*refs, silu_gate, final_norm):
    if final_norm:
        y_ref, gm_ref, h_ref, wo_ref, nf_ref, wg_ref, wu_ref, wd_ref, fn_ref, o_ref = refs
    else:
        y_ref, gm_ref, h_ref, wo_ref, nf_ref, wg_ref, wu_ref, wd_ref, o_ref = refs
    gm = gm_ref[...].astype(F32)
    if silu_gate:
        gm = jax.nn.silu(gm)
    h1 = h_ref[...] + _dot(y_ref[...].astype(F32) * gm, wo_ref[...])
    n = _rms(h1, nf_ref[...])
    act = jax.nn.silu(_dot(n, wg_ref[...])) * _dot(n, wu_ref[...])
    h2 = h1 + _dot(act, wd_ref[...])
    if final_norm:
        h2 = _rms(h2, fn_ref[...])
    o_ref[...] = h2


def _mix_ffn(y2d, gm2d, h2d, wo, nf_row, wg, wu, wd, fn_row, *, silu_gate, name):
    rows, d = h2d.shape
    dff = wg.shape[1]
    tm = _pick_tile(rows, (256, 128, 64))
    final_norm = fn_row is not None
    tok = pl.BlockSpec((tm, d), lambda i: (i, 0))
    in_specs = [tok, tok, tok, _const_spec((d, d)), _const_spec((1, d)),
                _const_spec((d, dff)), _const_spec((d, dff)), _const_spec((dff, d))]
    args = [y2d, gm2d, h2d, wo, nf_row, wg, wu, wd]
    if final_norm:
        in_specs.append(_const_spec((1, d)))
        args.append(fn_row)
    return pl.pallas_call(
        functools.partial(_mix_ffn_kernel, silu_gate=silu_gate, final_norm=final_norm),
        grid=(rows // tm,),
        in_specs=in_specs,
        out_specs=tok,
        out_shape=jax.ShapeDtypeStruct((rows, d), F32),
        compiler_params=pltpu.CompilerParams(dimension_semantics=("arbitrary",), vmem_limit_bytes=VMEM_LIMIT),
        name=name,
    )(*args)


def _rwkv_proj_kernel(h_ref, sb_ref, vec_ref, mu_ref, wr_ref, wk_ref, wv_ref,
                      w1_ref, w2_ref, a1_ref, a2_ref, g1_ref, g2_ref,
                      r_o, k_o, v_o, kk_o, a_o, w_o, g_o, last_o, hs_ref):
    t = pl.program_id(1)
    tt = h_ref.shape[0]
    hist = SUBLANES
    hn = _rms(h_ref[...], vec_ref[0:1, :])

    @pl.when(t == 0)
    def _():
        hs_ref[hist - 1:hist, :] = sb_ref[...]

    hs_ref[hist:hist + tt, :] = hn
    prev = pltpu.roll(hs_ref[...], 1, 0)[hist:]
    hs_ref[hist - 1:hist, :] = hn[tt - 1:tt, :]
    last_o[...] = hn[tt - 1:tt, :]
    xx = prev - hn

    def mix(i):
        return (hn + xx * mu_ref[i:i + 1, :]).astype(BF16)

    r = _dot(mix(0), wr_ref[...])
    k = _dot(mix(2), wk_ref[...])
    v = _dot(mix(3), wv_ref[...])
    wl = vec_ref[1:2, :] + _dot(jnp.tanh(_dot(mix(1), w1_ref[...])), w2_ref[...])
    a = jax.nn.sigmoid(vec_ref[2:3, :] + _dot(_dot(mix(4), a1_ref[...]), a2_ref[...]))
    gate = _dot(jax.nn.sigmoid(_dot(mix(5), g1_ref[...])), g2_ref[...])
    r_o[...] = r.astype(r_o.dtype)
    k_o[...] = (k * (1.0 + (a - 1.0) * vec_ref[4:5, :])).astype(k_o.dtype)
    v_o[...] = v.astype(v_o.dtype)
    kk_o[...] = (k * vec_ref[3:4, :]).astype(kk_o.dtype)
    a_o[...] = a.astype(a_o.dtype)
    w_o[...] = -DECAY_SCALE * jax.nn.sigmoid(wl)
    g_o[...] = gate.astype(g_o.dtype)


def _rwkv_proj(h3, sbuf, vec, mu8, wr, wk, wv, w1, w2, a1, a2, g1, g2):
    b, tp, d = h3.shape
    tt = _pick_tile(tp, (256, 192, 128, 64))
    tok = pl.BlockSpec((None, tt, d), lambda i, t: (i, t, 0))
    row = pl.BlockSpec((None, 1, d), lambda i, t: (i, 0, 0))
    weights = [wr, wk, wv, w1, w2, a1, a2, g1, g2]
    tok_shape = lambda dt: jax.ShapeDtypeStruct((b, tp, d), dt)
    return pl.pallas_call(
        _rwkv_proj_kernel,
        grid=(b, tp // tt),
        in_specs=[tok, row, _const_spec(vec.shape), _const_spec(mu8.shape)] + [_const_spec(w.shape) for w in weights],
        out_specs=[tok] * 7 + [row],
        out_shape=[tok_shape(BF16)] * 5 + [tok_shape(F32), tok_shape(BF16), jax.ShapeDtypeStruct((b, 1, d), F32)],
        scratch_shapes=[pltpu.VMEM((tt + SUBLANES, d), F32)],
        compiler_params=pltpu.CompilerParams(dimension_semantics=("arbitrary", "arbitrary"),
                                             vmem_limit_bytes=VMEM_LIMIT),
        name="rwkv_proj",
    )(h3, sbuf, vec, mu8, *weights)


def _rwkv_scan_kernel(r_ref, k_ref, v_ref, kk_ref, a_ref, w_ref, s0_ref, rk_ref, lnw_ref, lnb_ref,
                      yo_ref, s_ref):
    c = pl.program_id(1)
    L = CHUNK
    N = RWKV_N
    P2 = 2 * N
    d = r_ref.shape[1]

    @pl.when(c == 0)
    def _():
        s_ref[...] = s0_ref[...]

    row = lax.broadcasted_iota(jnp.int32, (L, L), 0)
    col = lax.broadcasted_iota(jnp.int32, (L, L), 1)
    tril = (row >= col).astype(BF16)
    r2 = lax.broadcasted_iota(jnp.int32, (P2, P2), 0)
    c2 = lax.broadcasted_iota(jnp.int32, (P2, P2), 1)
    same = (r2 < N) == (c2 < N)
    strict2 = same & ((r2 & (N - 1)) > (c2 & (N - 1)))
    incl2 = same & ((r2 & (N - 1)) >= (c2 & (N - 1)))
    eye2 = (r2 == c2).astype(F32)
    first = lax.broadcasted_iota(jnp.int32, (L, P2), 1) < N

    npairs = d // P2
    pairs = range(npairs)

    def per_pair(x2d):
        return jnp.stack([x2d[:, p * P2:(p + 1) * P2] for p in pairs])

    def lane_sums(x3):
        s0 = jnp.sum(jnp.where(first, x3, 0.0), axis=-1, keepdims=True)
        s1 = jnp.sum(jnp.where(first, 0.0, x3), axis=-1, keepdims=True)
        return jnp.where(first, s0, s1)

    w_all = w_ref[...]
    cw_all = _dot_exact_lhs(tril, w_all)
    cwl = cw_all[L - 1:L, :]
    dec_in = jnp.exp(cw_all)
    dec_ex = jnp.exp(cw_all - w_all)
    dec_inv = jnp.exp(-cw_all)
    dec_rem = jnp.exp(cwl - cw_all)
    dec_all = jnp.exp(cwl)

    kkr = per_pair(kk_ref[...].astype(F32))
    kkn = kkr * lax.rsqrt(lane_sums(kkr * kkr) + L2_EPS)
    rp = per_pair(r_ref[...].astype(F32))
    kp = per_pair(k_ref[...].astype(F32))
    vp = per_pair(v_ref[...].astype(F32))
    bb = kkn * per_pair(a_ref[...].astype(F32))
    p_in = per_pair(dec_in)
    p_ex = per_pair(dec_ex)
    p_inv = per_pair(dec_inv)
    p_rem = per_pair(dec_rem)
    p_all = jnp.stack([dec_all[:, p * P2:(p + 1) * P2] for p in pairs])

    def stack(x):
        return jnp.concatenate([jnp.where(first, x, 0.0), jnp.where(first, 0.0, x)], axis=1).astype(BF16)

    lhs = jnp.concatenate([stack(-kkn * p_ex), stack(rp * p_in)], axis=1)
    rhs = jnp.concatenate([stack(bb * p_inv), stack(kp * p_inv)], axis=1)
    m = _bdot_nt(lhs, rhs)
    l_ab = jnp.where(strict2, m[:, :P2, :P2], 0.0)
    l_ak = jnp.where(strict2, m[:, :P2, P2:], 0.0)
    m_rb = jnp.where(incl2, m[:, P2:, :P2], 0.0)
    m_rk = jnp.where(incl2, m[:, P2:, P2:], 0.0)

    s = s_ref[...]
    from_s = _bdot_nt(lhs, s)
    vs = stack(vp)
    t_inv = _neumann(l_ab, eye2, L)
    u = _bdot(t_inv, from_s[:, :P2] + _bdot(l_ak, vs))
    uv = jnp.concatenate([u.astype(BF16), vs], axis=1)
    ys = from_s[:, P2:] + _bdot(jnp.concatenate([m_rb, m_rk], axis=2), uv)
    y = ys[:, :L] + ys[:, L:]
    upd = jnp.concatenate([stack(bb * p_rem), stack(kp * p_rem)], axis=1)
    s_ref[...] = s * p_all + _bdot_tn(uv, upd)

    mean = lane_sums(y) * (1.0 / N)
    cy = y - mean
    var = lane_sums(cy * cy) * (1.0 / N)
    rk3 = jnp.stack([rk_ref[:, p * P2:(p + 1) * P2] for p in pairs])
    bonus = lane_sums(rp * kp * rk3) * vp
    for p in pairs:
        sl = slice(p * P2, (p + 1) * P2)
        yn = cy[p] * lax.rsqrt(var[p] + GN_EPS) * lnw_ref[:, sl] + lnb_ref[:, sl]
        yo_ref[:, sl] = (yn + bonus[p]).astype(yo_ref.dtype)


def _rwkv_scan(r, k, v, kk, a, w, s0_pairs, rk_row, lnw_row, lnb_row):
    b, tp, d = r.shape
    npairs = d // (2 * RWKV_N)
    tok = pl.BlockSpec((None, CHUNK, d), lambda i, c: (i, c, 0))
    st = pl.BlockSpec((None, npairs, 2 * RWKV_N, 2 * RWKV_N), lambda i, c: (i, 0, 0, 0))
    return pl.pallas_call(
        _rwkv_scan_kernel,
        grid=(b, tp // CHUNK),
        in_specs=[tok] * 6 + [st, _const_spec((1, d)), _const_spec((1, d)), _const_spec((1, d))],
        out_specs=[tok, st],
        out_shape=[jax.ShapeDtypeStruct((b, tp, d), BF16),
                   jax.ShapeDtypeStruct((b, npairs, 2 * RWKV_N, 2 * RWKV_N), F32)],
        compiler_params=pltpu.CompilerParams(dimension_semantics=("arbitrary", "arbitrary"),
                                             vmem_limit_bytes=VMEM_LIMIT),
        name="rwkv_scan",
    )(r, k, v, kk, a, w, s0_pairs, rk_row, lnw_row, lnb_row)


def _pad_cols(w, n):
    return jnp.pad(w, ((0, 0), (0, n - w.shape[1])))


def _pad_rows(w, n):
    return jnp.pad(w, ((0, n - w.shape[0]), (0, 0)))


def _pairs_from_state(s):
    b, h, n, _ = s.shape
    s = s.reshape(b, h // 2, 2, n, n)
    z = jnp.zeros_like(s[:, :, 0])
    top = jnp.concatenate([s[:, :, 0], z], axis=-1)
    bot = jnp.concatenate([z, s[:, :, 1]], axis=-1)
    return jnp.concatenate([top, bot], axis=-2)


def _state_from_pairs(sp):
    b, hp, n2, _ = sp.shape
    n = n2 // 2
    return jnp.stack([sp[:, :, :n, :n], sp[:, :, n:, n:]], axis=2).reshape(b, 2 * hp, n, n)


def _run_trunk(h, conv_bufs, gdn_states, shift_bufs, rwkv_states, wts):
    b, tp, d = h.shape
    rows = b * tp
    depth = wts["norm_mix"].shape[0]
    new_conv, new_gdn, new_shift, new_rwkv = [], [], [], []
    for i in range(depth):
        j = i // 2
        fn_row = wts["norm_final"] if i == depth - 1 else None
        if i % 2 == 0:
            g = wts["gdn"][j]
            cbuf8 = jnp.pad(conv_bufs[j], ((0, 0), (SUBLANES - (CONV_W - 1), 0), (0, 0)))
            q, k, v, z, ab, cst = _gdn_proj(h, wts["norm_mix"][i:i + 1], g["w_in"], g["conv_w"], cbuf8)
            o, s_new = _gdn_scan(q, k, v, ab, gdn_states[j], g["a_log"], g["dt_bias"], g["o_norm"])
            new_conv.append(cst[:, SUBLANES - (CONV_W - 1):])
            new_gdn.append(s_new)
            h2 = _mix_ffn(o.reshape(rows, d), z.reshape(rows, d), h.reshape(rows, d), g["w_out"],
                          wts["norm_ffn"][i:i + 1], wts["ffn_gate"][i], wts["ffn_up"][i], wts["ffn_down"][i],
                          fn_row, silu_gate=True, name="gdn_out_ffn")
        else:
            rw = wts["rwkv"][j]
            vec = jnp.concatenate([wts["norm_mix"][i:i + 1], rw["vec"]], axis=0)
            r, k, v, kk, a, w, gate, last_hn = _rwkv_proj(
                h, shift_bufs[j], vec, rw["mu"], rw["w_r"], rw["w_k"], rw["w_v"],
                rw["w1"], rw["w2"], rw["a1"], rw["a2"], rw["g1"], rw["g2"])
            yo, sp_new = _rwkv_scan(r, k, v, kk, a, w, _pairs_from_state(rwkv_states[j]),
                                    rw["r_k"], rw["ln_w"], rw["ln_b"])
            new_shift.append(last_hn)
            new_rwkv.append(_state_from_pairs(sp_new))
            h2 = _mix_ffn(yo.reshape(rows, d), gate.reshape(rows, d), h.reshape(rows, d), rw["w_o"],
                          wts["norm_ffn"][i:i + 1], wts["ffn_gate"][i], wts["ffn_up"][i], wts["ffn_down"][i],
                          fn_row, silu_gate=False, name="rwkv_out_ffn")
        h = h2.reshape(b, tp, d)
    return h, jnp.stack(new_conv), jnp.stack(new_gdn), jnp.stack(new_shift), jnp.stack(new_rwkv)


def _prepare_weights(norm_mix, norm_ffn, norm_final, gdn_w_in, gdn_conv_w, gdn_a_log, gdn_dt_bias, gdn_o_norm,
                     gdn_w_out, rwkv_mu, rwkv_w0, rwkv_w1, rwkv_w2, rwkv_a0, rwkv_a1, rwkv_a2, rwkv_g1, rwkv_g2,
                     rwkv_k_k, rwkv_k_a, rwkv_r_k, rwkv_w_r, rwkv_w_k, rwkv_w_v, rwkv_w_o, rwkv_ln_w,
                     rwkv_ln_b, ffn_w_gate, ffn_w_up, ffn_w_down):
    d = norm_mix.shape[1]
    gdn = []
    for j in range(gdn_w_in.shape[0]):
        w = gdn_w_in[j]
        qkvz = w[:, :GDN_QKV + GDN_KDIM]
        a_cols = _pad_cols(w[:, GDN_QKV + GDN_KDIM:GDN_QKV + GDN_KDIM + GDN_HEADS], LANES)
        b_cols = _pad_cols(w[:, GDN_QKV + GDN_KDIM + GDN_HEADS:], LANES)
        gdn.append(dict(
            w_in=jnp.concatenate([qkvz, a_cols, b_cols], axis=1).astype(BF16),
            conv_w=gdn_conv_w[j],
            a_log=_pad_cols(gdn_a_log[j][None], LANES),
            dt_bias=_pad_cols(gdn_dt_bias[j][None], LANES),
            o_norm=gdn_o_norm[j][None],
            w_out=gdn_w_out[j].astype(BF16),
        ))
    rwkv = []
    for j in range(rwkv_mu.shape[0]):
        lw = -(-rwkv_w1.shape[2] // LANES) * LANES
        la = -(-rwkv_a1.shape[2] // LANES) * LANES
        lg = -(-rwkv_g1.shape[2] // LANES) * LANES
        rwkv.append(dict(
            vec=jnp.stack([rwkv_w0[j], rwkv_a0[j], rwkv_k_k[j], rwkv_k_a[j],
                           jnp.zeros_like(rwkv_w0[j]), jnp.zeros_like(rwkv_w0[j]), jnp.zeros_like(rwkv_w0[j])]),
            mu=_pad_rows(rwkv_mu[j], SUBLANES),
            w_r=rwkv_w_r[j].astype(BF16), w_k=rwkv_w_k[j].astype(BF16), w_v=rwkv_w_v[j].astype(BF16),
            w_o=rwkv_w_o[j].astype(BF16),
            w1=_pad_cols(rwkv_w1[j], lw).astype(BF16), w2=_pad_rows(rwkv_w2[j], lw).astype(BF16),
            a1=_pad_cols(rwkv_a1[j], la).astype(BF16), a2=_pad_rows(rwkv_a2[j], la).astype(BF16),
            g1=_pad_cols(rwkv_g1[j], lg).astype(BF16), g2=_pad_rows(rwkv_g2[j], lg).astype(BF16),
            r_k=rwkv_r_k[j].reshape(1, d), ln_w=rwkv_ln_w[j][None], ln_b=rwkv_ln_b[j][None],
        ))
    return dict(norm_mix=norm_mix, norm_ffn=norm_ffn, norm_final=norm_final[None], gdn=gdn, rwkv=rwkv,
                ffn_gate=ffn_w_gate.astype(BF16), ffn_up=ffn_w_up.astype(BF16), ffn_down=ffn_w_down.astype(BF16))


def kernel(x_prompt, x_sample, cache_gdn_conv, state_gdn, cache_rwkv_shift, state_rwkv, meta_tokens, norm_mix, norm_ffn, norm_final, gdn_w_in, gdn_conv_w, gdn_a_log, gdn_dt_bias, gdn_o_norm, gdn_w_out, rwkv_mu, rwkv_w0, rwkv_w1, rwkv_w2, rwkv_a0, rwkv_a1, rwkv_a2, rwkv_g1, rwkv_g2, rwkv_k_k, rwkv_k_a, rwkv_r_k, rwkv_w_r, rwkv_w_k, rwkv_w_v, rwkv_w_o, rwkv_ln_w, rwkv_ln_b, ffn_w_gate, ffn_w_up, ffn_w_down):
    wts = _prepare_weights(norm_mix, norm_ffn, norm_final, gdn_w_in, gdn_conv_w, gdn_a_log, gdn_dt_bias,
                           gdn_o_norm, gdn_w_out, rwkv_mu, rwkv_w0, rwkv_w1, rwkv_w2, rwkv_a0, rwkv_a1, rwkv_a2,
                           rwkv_g1, rwkv_g2, rwkv_k_k, rwkv_k_a, rwkv_r_k, rwkv_w_r, rwkv_w_k, rwkv_w_v,
                           rwkv_w_o, rwkv_ln_w, rwkv_ln_b, ffn_w_gate, ffn_w_up, ffn_w_down)
    n_gdn = gdn_w_in.shape[0]
    n_rwkv = rwkv_mu.shape[0]
    b, seq, d = x_prompt.shape
    dt = x_prompt.dtype

    lead_pad = (-N_META) % CHUNK
    h_p = jnp.concatenate([jnp.zeros((b, lead_pad, d), dt),
                           jnp.broadcast_to(meta_tokens.astype(dt)[None], (b, N_META, d)), x_prompt], axis=1)
    zero_conv = jnp.zeros((n_gdn, b, CONV_W - 1, GDN_QKV), dt)
    zero_gdn = jnp.zeros((n_gdn, b, GDN_HEADS, GDN_DK, GDN_DV), dt)
    zero_shift = jnp.zeros((n_rwkv, b, 1, d), dt)
    zero_rwkv = jnp.zeros((n_rwkv, b, d // RWKV_N, RWKV_N, RWKV_N), dt)
    out_p, p_conv, p_gdn, p_shift, p_rwkv = _run_trunk(h_p, zero_conv, zero_gdn, zero_shift, zero_rwkv, wts)
    y_prompt = out_p[:, lead_pad + N_META:]

    y_sample, s_conv, s_gdn, s_shift, s_rwkv = _run_trunk(
        x_sample, cache_gdn_conv, state_gdn, cache_rwkv_shift, state_rwkv, wts)

    return (y_prompt, y_sample, p_conv, p_gdn, p_shift, p_rwkv, s_conv, s_gdn, s_shift, s_rwkv)
```

```python
import functools

import jax
import jax.numpy as jnp
from jax import lax
from jax.experimental import pallas as pl
from jax.experimental.pallas import tpu as pltpu

F32 = jnp.float32
BF16 = jnp.bfloat16

CHUNK = 64
N_META = 16
EPS = 1e-6
L2_EPS = 1e-6
GN_EPS = 64e-5

GDN_HEADS = 8
GDN_DK = 128
GDN_DV = 128
GDN_KDIM = GDN_HEADS * GDN_DK
GDN_QKV = 3 * GDN_KDIM
CONV_W = 4
RWKV_N = 64
DECAY_SCALE = 0.6065306597126334

LANES = 128
SUBLANES = 8
VMEM_LIMIT = 56 * 1024 * 1024


def _dot(a, b):
    return jnp.dot(a.astype(BF16), b.astype(BF16), preferred_element_type=F32)


def _bdot(a, b):
    return lax.dot_general(a.astype(BF16), b.astype(BF16), (((2,), (1,)), ((0,), (0,))),
                           preferred_element_type=F32)


def _bdot_nt(a, b):
    return lax.dot_general(a.astype(BF16), b.astype(BF16), (((2,), (2,)), ((0,), (0,))),
                           preferred_element_type=F32)


def _bdot_tn(a, b):
    return lax.dot_general(a.astype(BF16), b.astype(BF16), (((1,), (1,)), ((0,), (0,))),
                           preferred_element_type=F32)


def _split_bf16(x, parts):
    out = []
    for _ in range(parts - 1):
        hi = x.astype(BF16)
        out.append(hi)
        x = x - hi.astype(F32)
    out.append(x.astype(BF16))
    return out


def _dot_exact_lhs(m01, x, parts=3):
    return sum(_dot(m01, t) for t in _split_bf16(x, parts))


def _rms(x, g, eps=EPS):
    return x * lax.rsqrt(jnp.mean(x * x, axis=-1, keepdims=True) + eps) * g


def _neumann(x, eye, size):
    rows = x.shape[1]
    t = eye + x
    p = _bdot(x, x)
    n = 4
    while n < size:
        both = _bdot(jnp.concatenate([p, t], axis=1), p)
        t = t + both[:, rows:]
        p = both[:, :rows]
        n *= 2
    return t + _bdot(t, p)


def _pick_tile(n, candidates):
    for c in candidates:
        if n % c == 0:
            return c
    raise ValueError(f"no tile for {n}")


def _const_spec(shape):
    nd = len(shape)
    return pl.BlockSpec(shape, lambda *_: (0,) * nd, pipeline_mode=pl.Buffered(1))


def _gdn_proj_kernel(x_ref, g_ref, w_ref, cw_ref, cbuf_ref, q_o, k_o, v_o, z_o, ab_o, cst_o, xs_ref):
    t = pl.program_id(1)
    tt = x_ref.shape[0]
    hist = SUBLANES

    @pl.when(t == 0)
    def _():
        xs_ref[...] = cbuf_ref[...]

    xn = _rms(x_ref[...], g_ref[...]).astype(BF16)

    for out, c0 in ((q_o, 0), (k_o, GDN_KDIM), (v_o, 2 * GDN_KDIM)):
        cols = slice(c0, c0 + GDN_KDIM)
        proj = _dot(xn, w_ref[:, cols])
        win = jnp.concatenate([xs_ref[:, cols], proj], axis=0)
        xs_ref[:, cols] = proj[tt - hist:tt]
        cst_o[:, cols] = proj[tt - hist:tt]
        win1 = pltpu.roll(win, 1, 0)
        near = cw_ref[3:4, cols] * win + cw_ref[2:3, cols] * win1
        far = cw_ref[1:2, cols] * win + cw_ref[0:1, cols] * win1
        acc = (near + pltpu.roll(far, 2, 0))[hist:]
        out[...] = jax.nn.silu(acc).astype(out.dtype)
    z_o[...] = _dot(xn, w_ref[:, GDN_QKV:GDN_QKV + GDN_KDIM]).astype(z_o.dtype)
    ab_o[...] = _dot(xn, w_ref[:, GDN_QKV + GDN_KDIM:])


def _gdn_proj(h3, g_row, w_bf16, conv_w, cbuf8):
    b, tp, d = h3.shape
    n = w_bf16.shape[1]
    tt = _pick_tile(tp, (256, 192, 128, 64))
    tok = lambda width: pl.BlockSpec((None, tt, width), lambda i, t: (i, t, 0))
    hist_spec = pl.BlockSpec((None, SUBLANES, GDN_QKV), lambda i, t: (i, 0, 0))
    return pl.pallas_call(
        _gdn_proj_kernel,
        grid=(b, tp // tt),
        in_specs=[tok(d), _const_spec((1, d)), _const_spec((d, n)), _const_spec((CONV_W, GDN_QKV)), hist_spec],
        out_specs=[tok(GDN_KDIM), tok(GDN_KDIM), tok(GDN_KDIM), tok(GDN_KDIM), tok(2 * LANES), hist_spec],
        out_shape=[jax.ShapeDtypeStruct((b, tp, GDN_KDIM), BF16)] * 4
        + [jax.ShapeDtypeStruct((b, tp, 2 * LANES), F32), jax.ShapeDtypeStruct((b, SUBLANES, GDN_QKV), F32)],
        scratch_shapes=[pltpu.VMEM((SUBLANES, GDN_QKV), F32)],
        compiler_params=pltpu.CompilerParams(dimension_semantics=("arbitrary", "arbitrary"),
                                             vmem_limit_bytes=VMEM_LIMIT),
        name="gdn_in_proj",
    )(h3, g_row, w_bf16, conv_w, cbuf8)


def _gdn_scan_kernel(q_ref, k_ref, v_ref, ab_ref, s0_ref, alog_ref, dtb_ref, onorm_ref, o_ref, s_ref):
    c = pl.program_id(1)
    L = CHUNK

    @pl.when(c == 0)
    def _():
        s_ref[...] = s0_ref[...]

    row = lax.broadcasted_iota(jnp.int32, (L, L), 0)
    col = lax.broadcasted_iota(jnp.int32, (L, L), 1)
    causal = row >= col
    strict = row > col
    tril = causal.astype(BF16)
    eye = (row == col).astype(F32)

    nb = q_ref.shape[0]
    a_in = jnp.concatenate([ab_ref[i, :, :LANES] + dtb_ref[...] for i in range(nb)], axis=1)
    b_in = jnp.concatenate([ab_ref[i, :, LANES:] for i in range(nb)], axis=1)
    neg_a = jnp.concatenate([-jnp.exp(alog_ref[...])] * nb, axis=1)
    g = neg_a * jax.nn.softplus(a_in)
    beta = jax.nn.sigmoid(b_in)
    G = _dot_exact_lhs(tril, g)
    Gt = G.T
    GL = G[L - 1:L, :]
    exp_g = jnp.exp(G)
    exp_gr = jnp.exp(GL - G)
    exp_gl = jnp.exp(GL)

    chains = [(i, h) for i in range(nb) for h in range(GDN_HEADS)]

    def per_head(ref):
        return jnp.stack([ref[i, :, h * GDN_DK:(h + 1) * GDN_DK] for i, h in chains])

    def head_cols(x2d):
        return jnp.stack([x2d[:, i * LANES + h:i * LANES + h + 1] for i, h in chains])

    q = per_head(q_ref).astype(F32)
    kf = per_head(k_ref).astype(F32)
    v = per_head(v_ref).astype(F32)
    q = q * (lax.rsqrt(jnp.sum(q * q, axis=-1, keepdims=True) + L2_EPS) * (GDN_DK ** -0.5))
    kf = kf * lax.rsqrt(jnp.sum(kf * kf, axis=-1, keepdims=True) + L2_EPS)
    k = kf.astype(BF16)
    bcol = head_cols(beta)
    g_col = head_cols(G)
    g_row = jnp.stack([Gt[i * LANES + h:i * LANES + h + 1, :] for i, h in chains])
    eg_col = head_cols(exp_g)
    egr_col = head_cols(exp_gr)
    egl = jnp.stack([exp_gl[:, i * LANES + h:i * LANES + h + 1] for i, h in chains])

    dlt = g_col - g_row
    dec = jnp.where(causal, jnp.exp(jnp.where(causal, dlt, 0.0)), 0.0)
    kq_k = _bdot_nt(jnp.concatenate([k, q.astype(BF16)], axis=1), k)
    a_mat = jnp.where(strict, bcol * kq_k[:, :L] * dec, 0.0)
    qk = kq_k[:, L:] * dec
    t_inv = _neumann(-a_mat, eye, L)
    rhs = jnp.concatenate([v * bcol, kf * (bcol * eg_col)], axis=-1)
    sol = _bdot(t_inv, rhs)
    u = sol[..., :GDN_DV]
    w = sol[..., GDN_DV:]
    s = s_ref[...].reshape(nb * GDN_HEADS, GDN_DK, GDN_DV)
    ws_qs = _bdot(jnp.concatenate([w, q * eg_col], axis=1), s)
    v_new = u - ws_qs[:, :L]
    o = ws_qs[:, L:] + _bdot(qk, v_new)
    s_ref[...] = (s * egl + _bdot_tn(kf * egr_col, v_new)).reshape(s_ref.shape)
    o = o * lax.rsqrt(jnp.mean(o * o, axis=-1, keepdims=True) + EPS) * onorm_ref[...]
    for n, (i, h) in enumerate(chains):
        o_ref[i, :, h * GDN_DV:(h + 1) * GDN_DV] = o[n].astype(o_ref.dtype)


def _scan_batch(b):
    return 2 if b % 2 == 0 else 1


def _gdn_scan(q, k, v, ab, s0, alog_row, dtb_row, onorm_row):
    b, tp, _ = q.shape
    nb = _scan_batch(b)
    tok = pl.BlockSpec((nb, CHUNK, GDN_KDIM), lambda i, c: (i, c, 0))
    st = pl.BlockSpec((nb, GDN_HEADS, GDN_DK, GDN_DV), lambda i, c: (i, 0, 0, 0))
    return pl.pallas_call(
        _gdn_scan_kernel,
        grid=(b // nb, tp // CHUNK),
        in_specs=[tok, tok, tok, pl.BlockSpec((nb, CHUNK, 2 * LANES), lambda i, c: (i, c, 0)), st,
                  _const_spec((1, LANES)), _const_spec((1, LANES)), _const_spec((1, GDN_DV))],
        out_specs=[tok, st],
        out_shape=[jax.ShapeDtypeStruct((b, tp, GDN_KDIM), BF16),
                   jax.ShapeDtypeStruct((b, GDN_HEADS, GDN_DK, GDN_DV), F32)],
        compiler_params=pltpu.CompilerParams(dimension_semantics=("arbitrary", "arbitrary"),
                                             vmem_limit_bytes=VMEM_LIMIT),
        name="gdn_scan",
    )(q, k, v, ab, s0, alog_row, dtb_row, onorm_row)


def _mix_ffn_kernel(*refs, silu_gate, final_norm):
    if final_norm:
        y_ref, gm_ref, h_ref, wo_ref, nf_ref, wg_ref, wu_ref, wd_ref, fn_ref, o_ref = refs
    else:
        y_ref, gm_ref, h_ref, wo_ref, nf_ref, wg_ref, wu_ref, wd_ref, o_ref = refs
    gm = gm_ref[...].astype(F32)
    if silu_gate:
        gm = jax.nn.silu(gm)
    h1 = h_ref[...] + _dot(y_ref[...].astype(F32) * gm, wo_ref[...])
    n = _rms(h1, nf_ref[...])
    act = jax.nn.silu(_dot(n, wg_ref[...])) * _dot(n, wu_ref[...])
    h2 = h1 + _dot(act, wd_ref[...])
    if final_norm:
        h2 = _rms(h2, fn_ref[...])
    o_ref[...] = h2


def _mix_ffn(y2d, gm2d, h2d, wo, nf_row, wg, wu, wd, fn_row, *, silu_gate, name):
    rows, d = h2d.shape
    dff = wg.shape[1]
    tm = _pick_tile(rows, (512, 256, 128, 64))
    final_norm = fn_row is not None
    tok = pl.BlockSpec((tm, d), lambda i: (i, 0))
    in_specs = [tok, tok, tok, _const_spec((d, d)), _const_spec((1, d)),
                _const_spec((d, dff)), _const_spec((d, dff)), _const_spec((dff, d))]
    args = [y2d, gm2d, h2d, wo, nf_row, wg, wu, wd]
    if final_norm:
        in_specs.append(_const_spec((1, d)))
        args.append(fn_row)
    return pl.pallas_call(
        functools.partial(_mix_ffn_kernel, silu_gate=silu_gate, final_norm=final_norm),
        grid=(rows // tm,),
        in_specs=in_specs,
        out_specs=tok,
        out_shape=jax.ShapeDtypeStruct((rows, d), F32),
        compiler_params=pltpu.CompilerParams(dimension_semantics=("arbitrary",), vmem_limit_bytes=VMEM_LIMIT),
        name=name,
    )(*args)


def _rwkv_proj_kernel(h_ref, sb_ref, vec_ref, mu_ref, wr_ref, wk_ref, wv_ref,
                      w1_ref, w2_ref, a1_ref, a2_ref, g1_ref, g2_ref,
                      r_o, k_o, v_o, kk_o, a_o, w_o, g_o, last_o, hs_ref):
    t = pl.program_id(1)
    tt = h_ref.shape[0]
    hist = SUBLANES
    hn = _rms(h_ref[...], vec_ref[0:1, :])

    @pl.when(t == 0)
    def _():
        hs_ref[...] = jnp.broadcast_to(sb_ref[...], hs_ref.shape)

    prev = pltpu.roll(jnp.concatenate([hs_ref[...], hn], axis=0), 1, 0)[hist:]
    hs_ref[hist - 1:hist, :] = hn[tt - 1:tt, :]
    last_o[...] = hn[tt - 1:tt, :]
    xx = prev - hn

    def mix(i):
        return (hn + xx * mu_ref[i:i + 1, :]).astype(BF16)

    r = _dot(mix(0), wr_ref[...])
    k = _dot(mix(2), wk_ref[...])
    v = _dot(mix(3), wv_ref[...])
    wl = vec_ref[1:2, :] + _dot(jnp.tanh(_dot(mix(1), w1_ref[...])), w2_ref[...])
    a = jax.nn.sigmoid(vec_ref[2:3, :] + _dot(_dot(mix(4), a1_ref[...]), a2_ref[...]))
    gate = _dot(jax.nn.sigmoid(_dot(mix(5), g1_ref[...])), g2_ref[...])
    r_o[...] = r.astype(r_o.dtype)
    k_o[...] = (k * (1.0 + (a - 1.0) * vec_ref[4:5, :])).astype(k_o.dtype)
    v_o[...] = v.astype(v_o.dtype)
    kk_o[...] = (k * vec_ref[3:4, :]).astype(kk_o.dtype)
    a_o[...] = a.astype(a_o.dtype)
    w_o[...] = -DECAY_SCALE * jax.nn.sigmoid(wl)
    g_o[...] = gate.astype(g_o.dtype)


def _rwkv_proj(h3, sbuf, vec, mu8, wr, wk, wv, w1, w2, a1, a2, g1, g2):
    b, tp, d = h3.shape
    tt = _pick_tile(tp, (512, 256, 192, 128, 64))
    tok = pl.BlockSpec((None, tt, d), lambda i, t: (i, t, 0))
    row = pl.BlockSpec((None, 1, d), lambda i, t: (i, 0, 0))
    weights = [wr, wk, wv, w1, w2, a1, a2, g1, g2]
    tok_shape = lambda dt: jax.ShapeDtypeStruct((b, tp, d), dt)
    return pl.pallas_call(
        _rwkv_proj_kernel,
        grid=(b, tp // tt),
        in_specs=[tok, row, _const_spec(vec.shape), _const_spec(mu8.shape)] + [_const_spec(w.shape) for w in weights],
        out_specs=[tok] * 7 + [row],
        out_shape=[tok_shape(BF16)] * 5 + [tok_shape(F32), tok_shape(BF16), jax.ShapeDtypeStruct((b, 1, d), F32)],
        scratch_shapes=[pltpu.VMEM((SUBLANES, d), F32)],
        compiler_params=pltpu.CompilerParams(dimension_semantics=("arbitrary", "arbitrary"),
                                             vmem_limit_bytes=VMEM_LIMIT),
        name="rwkv_proj",
    )(h3, sbuf, vec, mu8, *weights)


def _rwkv_scan_kernel(r_ref, k_ref, v_ref, kk_ref, a_ref, w_ref, s0_ref, rk_ref, lnw_ref, lnb_ref,
                      yo_ref, s_ref):
    c = pl.program_id(1)
    L = CHUNK
    N = RWKV_N
    P2 = 2 * N
    d = r_ref.shape[1]

    @pl.when(c == 0)
    def _():
        s_ref[...] = s0_ref[...]

    row = lax.broadcasted_iota(jnp.int32, (L, L), 0)
    col = lax.broadcasted_iota(jnp.int32, (L, L), 1)
    tril = (row >= col).astype(BF16)
    r2 = lax.broadcasted_iota(jnp.int32, (P2, P2), 0)
    c2 = lax.broadcasted_iota(jnp.int32, (P2, P2), 1)
    same = (r2 < N) == (c2 < N)
    strict2 = same & ((r2 & (N - 1)) > (c2 & (N - 1)))
    incl2 = same & ((r2 & (N - 1)) >= (c2 & (N - 1)))
    eye2 = (r2 == c2).astype(F32)
    first = lax.broadcasted_iota(jnp.int32, (L, P2), 1) < N

    npairs = d // P2
    pairs = range(npairs)

    def per_pair(x2d):
        return jnp.stack([x2d[:, p * P2:(p + 1) * P2] for p in pairs])

    def lane_sums(x3):
        s0 = jnp.sum(jnp.where(first, x3, 0.0), axis=-1, keepdims=True)
        s1 = jnp.sum(jnp.where(first, 0.0, x3), axis=-1, keepdims=True)
        return jnp.where(first, s0, s1)

    w_all = w_ref[...]
    cw_all = _dot_exact_lhs(tril, w_all)
    cwl = cw_all[L - 1:L, :]
    dec_in = jnp.exp(cw_all)
    dec_ex = jnp.exp(cw_all - w_all)
    dec_inv = jnp.exp(-cw_all)
    dec_rem = jnp.exp(cwl - cw_all)
    dec_all = jnp.exp(cwl)

    kkr = per_pair(kk_ref[...].astype(F32))
    kkn = kkr * lax.rsqrt(lane_sums(kkr * kkr) + L2_EPS)
    rp = per_pair(r_ref[...].astype(F32))
    kp = per_pair(k_ref[...].astype(F32))
    vp = per_pair(v_ref[...].astype(F32))
    bb = kkn * per_pair(a_ref[...].astype(F32))
    p_in = per_pair(dec_in)
    p_ex = per_pair(dec_ex)
    p_inv = per_pair(dec_inv)
    p_rem = per_pair(dec_rem)
    p_all = jnp.stack([dec_all[:, p * P2:(p + 1) * P2] for p in pairs])

    def stack(x):
        return jnp.concatenate([jnp.where(first, x, 0.0), jnp.where(first, 0.0, x)], axis=1).astype(BF16)

    lhs = jnp.concatenate([stack(-kkn * p_ex), stack(rp * p_in)], axis=1)
    rhs = jnp.concatenate([stack(bb * p_inv), stack(kp * p_inv)], axis=1)
    m = _bdot_nt(lhs, rhs)
    l_ab = jnp.where(strict2, m[:, :P2, :P2], 0.0)
    l_ak = jnp.where(strict2, m[:, :P2, P2:], 0.0)
    m_rb = jnp.where(incl2, m[:, P2:, :P2], 0.0)
    m_rk = jnp.where(incl2, m[:, P2:, P2:], 0.0)

    s = s_ref[...]
    from_s = _bdot_nt(lhs, s)
    vs = stack(vp)
    t_inv = _neumann(l_ab, eye2, L)
    u = _bdot(t_inv, from_s[:, :P2] + _bdot(l_ak, vs))
    uv = jnp.concatenate([u.astype(BF16), vs], axis=1)
    ys = from_s[:, P2:] + _bdot(jnp.concatenate([m_rb, m_rk], axis=2), uv)
    y = ys[:, :L] + ys[:, L:]
    upd = jnp.concatenate([stack(bb * p_rem), stack(kp * p_rem)], axis=1)
    s_ref[...] = s * p_all + _bdot_tn(uv, upd)

    mean = lane_sums(y) * (1.0 / N)
    cy = y - mean
    var = lane_sums(cy * cy) * (1.0 / N)
    rk3 = jnp.stack([rk_ref[:, p * P2:(p + 1) * P2] for p in pairs])
    bonus = lane_sums(rp * kp * rk3) * vp
    for p in pairs:
        sl = slice(p * P2, (p + 1) * P2)
        yn = cy[p] * lax.rsqrt(var[p] + GN_EPS) * lnw_ref[:, sl] + lnb_ref[:, sl]
        yo_ref[:, sl] = (yn + bonus[p]).astype(yo_ref.dtype)


def _rwkv_scan(r, k, v, kk, a, w, s0_pairs, rk_row, lnw_row, lnb_row):
    b, tp, d = r.shape
    npairs = d // (2 * RWKV_N)
    tok = pl.BlockSpec((None, CHUNK, d), lambda i, c: (i, c, 0))
    st = pl.BlockSpec((None, npairs, 2 * RWKV_N, 2 * RWKV_N), lambda i, c: (i, 0, 0, 0))
    return pl.pallas_call(
        _rwkv_scan_kernel,
        grid=(b, tp // CHUNK),
        in_specs=[tok] * 6 + [st, _const_spec((1, d)), _const_spec((1, d)), _const_spec((1, d))],
        out_specs=[tok, st],
        out_shape=[jax.ShapeDtypeStruct((b, tp, d), BF16),
                   jax.ShapeDtypeStruct((b, npairs, 2 * RWKV_N, 2 * RWKV_N), F32)],
        compiler_params=pltpu.CompilerParams(dimension_semantics=("arbitrary", "arbitrary"),
                                             vmem_limit_bytes=VMEM_LIMIT),
        name="rwkv_scan",
    )(r, k, v, kk, a, w, s0_pairs, rk_row, lnw_row, lnb_row)


def _pad_cols(w, n):
    return jnp.pad(w, ((0, 0), (0, n - w.shape[1])))


def _pad_rows(w, n):
    return jnp.pad(w, ((0, n - w.shape[0]), (0, 0)))


def _pairs_from_state(s):
    b, h, n, _ = s.shape
    s = s.reshape(b, h // 2, 2, n, n)
    z = jnp.zeros_like(s[:, :, 0])
    top = jnp.concatenate([s[:, :, 0], z], axis=-1)
    bot = jnp.concatenate([z, s[:, :, 1]], axis=-1)
    return jnp.concatenate([top, bot], axis=-2)


def _state_from_pairs(sp):
    b, hp, n2, _ = sp.shape
    n = n2 // 2
    return jnp.stack([sp[:, :, :n, :n], sp[:, :, n:, n:]], axis=2).reshape(b, 2 * hp, n, n)


def _run_trunk(h, conv_bufs, gdn_states, shift_bufs, rwkv_states, wts):
    b, tp, d = h.shape
    rows = b * tp
    depth = wts["norm_mix"].shape[0]
    new_conv, new_gdn, new_shift, new_rwkv = [], [], [], []
    for i in range(depth):
        j = i // 2
        fn_row = wts["norm_final"] if i == depth - 1 else None
        if i % 2 == 0:
            g = wts["gdn"][j]
            cbuf8 = jnp.pad(conv_bufs[j], ((0, 0), (SUBLANES - (CONV_W - 1), 0), (0, 0)))
            q, k, v, z, ab, cst = _gdn_proj(h, wts["norm_mix"][i:i + 1], g["w_in"], g["conv_w"], cbuf8)
            o, s_new = _gdn_scan(q, k, v, ab, gdn_states[j], g["a_log"], g["dt_bias"], g["o_norm"])
            new_conv.append(cst[:, SUBLANES - (CONV_W - 1):])
            new_gdn.append(s_new)
            h2 = _mix_ffn(o.reshape(rows, d), z.reshape(rows, d), h.reshape(rows, d), g["w_out"],
                          wts["norm_ffn"][i:i + 1], wts["ffn_gate"][i], wts["ffn_up"][i], wts["ffn_down"][i],
                          fn_row, silu_gate=True, name="gdn_out_ffn")
        else:
            rw = wts["rwkv"][j]
            vec = jnp.concatenate([wts["norm_mix"][i:i + 1], rw["vec"]], axis=0)
            r, k, v, kk, a, w, gate, last_hn = _rwkv_proj(
                h, shift_bufs[j], vec, rw["mu"], rw["w_r"], rw["w_k"], rw["w_v"],
                rw["w1"], rw["w2"], rw["a1"], rw["a2"], rw["g1"], rw["g2"])
            yo, sp_new = _rwkv_scan(r, k, v, kk, a, w, _pairs_from_state(rwkv_states[j]),
                                    rw["r_k"], rw["ln_w"], rw["ln_b"])
            new_shift.append(last_hn)
            new_rwkv.append(_state_from_pairs(sp_new))
            h2 = _mix_ffn(yo.reshape(rows, d), gate.reshape(rows, d), h.reshape(rows, d), rw["w_o"],
                          wts["norm_ffn"][i:i + 1], wts["ffn_gate"][i], wts["ffn_up"][i], wts["ffn_down"][i],
                          fn_row, silu_gate=False, name="rwkv_out_ffn")
        h = h2.reshape(b, tp, d)
    return h, jnp.stack(new_conv), jnp.stack(new_gdn), jnp.stack(new_shift), jnp.stack(new_rwkv)


def _prepare_weights(norm_mix, norm_ffn, norm_final, gdn_w_in, gdn_conv_w, gdn_a_log, gdn_dt_bias, gdn_o_norm,
                     gdn_w_out, rwkv_mu, rwkv_w0, rwkv_w1, rwkv_w2, rwkv_a0, rwkv_a1, rwkv_a2, rwkv_g1, rwkv_g2,
                     rwkv_k_k, rwkv_k_a, rwkv_r_k, rwkv_w_r, rwkv_w_k, rwkv_w_v, rwkv_w_o, rwkv_ln_w,
                     rwkv_ln_b, ffn_w_gate, ffn_w_up, ffn_w_down):
    d = norm_mix.shape[1]
    gdn = []
    for j in range(gdn_w_in.shape[0]):
        w = gdn_w_in[j]
        qkvz = w[:, :GDN_QKV + GDN_KDIM]
        a_cols = _pad_cols(w[:, GDN_QKV + GDN_KDIM:GDN_QKV + GDN_KDIM + GDN_HEADS], LANES)
        b_cols = _pad_cols(w[:, GDN_QKV + GDN_KDIM + GDN_HEADS:], LANES)
        gdn.append(dict(
            w_in=jnp.concatenate([qkvz, a_cols, b_cols], axis=1).astype(BF16),
            conv_w=gdn_conv_w[j],
            a_log=_pad_cols(gdn_a_log[j][None], LANES),
            dt_bias=_pad_cols(gdn_dt_bias[j][None], LANES),
            o_norm=gdn_o_norm[j][None],
            w_out=gdn_w_out[j].astype(BF16),
        ))
    rwkv = []
    for j in range(rwkv_mu.shape[0]):
        lw = -(-rwkv_w1.shape[2] // LANES) * LANES
        la = -(-rwkv_a1.shape[2] // LANES) * LANES
        lg = -(-rwkv_g1.shape[2] // LANES) * LANES
        rwkv.append(dict(
            vec=jnp.stack([rwkv_w0[j], rwkv_a0[j], rwkv_k_k[j], rwkv_k_a[j],
                           jnp.zeros_like(rwkv_w0[j]), jnp.zeros_like(rwkv_w0[j]), jnp.zeros_like(rwkv_w0[j])]),
            mu=_pad_rows(rwkv_mu[j], SUBLANES),
            w_r=rwkv_w_r[j].astype(BF16), w_k=rwkv_w_k[j].astype(BF16), w_v=rwkv_w_v[j].astype(BF16),
            w_o=rwkv_w_o[j].astype(BF16),
            w1=_pad_cols(rwkv_w1[j], lw).astype(BF16), w2=_pad_rows(rwkv_w2[j], lw).astype(BF16),
            a1=_pad_cols(rwkv_a1[j], la).astype(BF16), a2=_pad_rows(rwkv_a2[j], la).astype(BF16),
            g1=_pad_cols(rwkv_g1[j], lg).astype(BF16), g2=_pad_rows(rwkv_g2[j], lg).astype(BF16),
            r_k=rwkv_r_k[j].reshape(1, d), ln_w=rwkv_ln_w[j][None], ln_b=rwkv_ln_b[j][None],
        ))
    return dict(norm_mix=norm_mix, norm_ffn=norm_ffn, norm_final=norm_final[None], gdn=gdn, rwkv=rwkv,
                ffn_gate=ffn_w_gate.astype(BF16), ffn_up=ffn_w_up.astype(BF16), ffn_down=ffn_w_down.astype(BF16))


def kernel(x_prompt, x_sample, cache_gdn_conv, state_gdn, cache_rwkv_shift, state_rwkv, meta_tokens, norm_mix, norm_ffn, norm_final, gdn_w_in, gdn_conv_w, gdn_a_log, gdn_dt_bias, gdn_o_norm, gdn_w_out, rwkv_mu, rwkv_w0, rwkv_w1, rwkv_w2, rwkv_a0, rwkv_a1, rwkv_a2, rwkv_g1, rwkv_g2, rwkv_k_k, rwkv_k_a, rwkv_r_k, rwkv_w_r, rwkv_w_k, rwkv_w_v, rwkv_w_o, rwkv_ln_w, rwkv_ln_b, ffn_w_gate, ffn_w_up, ffn_w_down):
    wts = _prepare_weights(norm_mix, norm_ffn, norm_final, gdn_w_in, gdn_conv_w, gdn_a_log, gdn_dt_bias,
                           gdn_o_norm, gdn_w_out, rwkv_mu, rwkv_w0, rwkv_w1, rwkv_w2, rwkv_a0, rwkv_a1, rwkv_a2,
                           rwkv_g1, rwkv_g2, rwkv_k_k, rwkv_k_a, rwkv_r_k, rwkv_w_r, rwkv_w_k, rwkv_w_v,
                           rwkv_w_o, rwkv_ln_w, rwkv_ln_b, ffn_w_gate, ffn_w_up, ffn_w_down)
    n_gdn = gdn_w_in.shape[0]
    n_rwkv = rwkv_mu.shape[0]
    b, seq, d = x_prompt.shape
    dt = x_prompt.dtype

    lead_pad = (-N_META) % CHUNK
    lead = jnp.concatenate([jnp.zeros((b, lead_pad, d), dt),
                            jnp.broadcast_to(meta_tokens.astype(dt)[None], (b, N_META, d))], axis=1)
    zeros_like_states = (jnp.zeros((n_gdn, b, CONV_W - 1, GDN_QKV), dt),
                         jnp.zeros((n_gdn, b, GDN_HEADS, GDN_DK, GDN_DV), dt),
                         jnp.zeros((n_rwkv, b, 1, d), dt),
                         jnp.zeros((n_rwkv, b, d // RWKV_N, RWKV_N, RWKV_N), dt))
    sample_states = (cache_gdn_conv, state_gdn, cache_rwkv_shift, state_rwkv)

    bs = x_sample.shape[0]
    if x_sample.shape[1:] == lead.shape[1:]:
        both = _run_trunk(jnp.concatenate([x_sample, lead], axis=0),
                          *[jnp.concatenate([s, z], axis=1) for s, z in zip(sample_states, zeros_like_states)],
                          wts)
        y_sample, s_conv, s_gdn, s_shift, s_rwkv = [both[0][:bs]] + [t[:, :bs] for t in both[1:]]
        lead_states = [t[:, bs:] for t in both[1:]]
    else:
        y_sample, s_conv, s_gdn, s_shift, s_rwkv = _run_trunk(x_sample, *sample_states, wts)
        lead_states = _run_trunk(lead, *zeros_like_states, wts)[1:]

    y_prompt, p_conv, p_gdn, p_shift, p_rwkv = _run_trunk(x_prompt, *lead_states, wts)

    return (y_prompt, y_sample, p_conv, p_gdn, p_shift, p_rwkv, s_conv, s_gdn, s_shift, s_rwkv)
```

```python
import functools

import jax
import jax.numpy as jnp
from jax import lax
from jax.experimental import pallas as pl
from jax.experimental.pallas import tpu as pltpu

F32 = jnp.float32
BF16 = jnp.bfloat16

CHUNK = 64
N_META = 16
EPS = 1e-6
L2_EPS = 1e-6
GN_EPS = 64e-5

GDN_HEADS = 8
GDN_DK = 128
GDN_DV = 128
GDN_KDIM = GDN_HEADS * GDN_DK
GDN_QKV = 3 * GDN_KDIM
CONV_W = 4
RWKV_N = 64
RWKV_GROUP = 4
DECAY_SCALE = 0.6065306597126334

LANES = 128
SUBLANES = 8
VMEM_LIMIT = 56 * 1024 * 1024


def _dot(a, b):
    return jnp.dot(a.astype(BF16), b.astype(BF16), preferred_element_type=F32)


def _bdot(a, b):
    return lax.dot_general(a.astype(BF16), b.astype(BF16), (((2,), (1,)), ((0,), (0,))),
                           preferred_element_type=F32)


def _bdot_nt(a, b):
    return lax.dot_general(a.astype(BF16), b.astype(BF16), (((2,), (2,)), ((0,), (0,))),
                           preferred_element_type=F32)


def _bdot_tn(a, b):
    return lax.dot_general(a.astype(BF16), b.astype(BF16), (((1,), (1,)), ((0,), (0,))),
                           preferred_element_type=F32)


def _split_bf16(x, parts):
    out = []
    for _ in range(parts - 1):
        hi = x.astype(BF16)
        out.append(hi)
        x = x - hi.astype(F32)
    out.append(x.astype(BF16))
    return out


def _dot_exact_lhs(m01, x, parts=3):
    return sum(_dot(m01, t) for t in _split_bf16(x, parts))


def _rms(x, g, eps=EPS):
    return x * lax.rsqrt(jnp.mean(x * x, axis=-1, keepdims=True) + eps) * g


def _neumann(x, eye, size):
    rows = x.shape[1]
    t = eye + x
    p = _bdot(x, x)
    n = 4
    while n < size:
        both = _bdot(jnp.concatenate([p, t], axis=1), p)
        t = t + both[:, rows:]
        p = both[:, :rows]
        n *= 2
    return t + _bdot(t, p)


def _pick_tile(n, candidates):
    for c in candidates:
        if n % c == 0:
            return c
    raise ValueError(f"no tile for {n}")


def _const_spec(shape):
    nd = len(shape)
    return pl.BlockSpec(shape, lambda *_: (0,) * nd, pipeline_mode=pl.Buffered(1))


def _gdn_proj_kernel(x_ref, g_ref, w_ref, cw_ref, cbuf_ref, q_o, k_o, v_o, z_o, ab_o, cst_o, xs_ref):
    t = pl.program_id(1)
    tt = x_ref.shape[0]
    hist = SUBLANES

    @pl.when(t == 0)
    def _():
        xs_ref[...] = cbuf_ref[...]

    xn = _rms(x_ref[...], g_ref[...]).astype(BF16)

    for out, c0 in ((q_o, 0), (k_o, GDN_KDIM), (v_o, 2 * GDN_KDIM)):
        cols = slice(c0, c0 + GDN_KDIM)
        proj = _dot(xn, w_ref[:, cols])
        win = jnp.concatenate([xs_ref[:, cols], proj], axis=0)
        xs_ref[:, cols] = proj[tt - hist:tt]
        cst_o[:, cols] = proj[tt - hist:tt]
        win1 = pltpu.roll(win, 1, 0)
        near = cw_ref[3:4, cols] * win + cw_ref[2:3, cols] * win1
        far = cw_ref[1:2, cols] * win + cw_ref[0:1, cols] * win1
        acc = (near + pltpu.roll(far, 2, 0))[hist:]
        out[...] = jax.nn.silu(acc).astype(out.dtype)
    z_o[...] = _dot(xn, w_ref[:, GDN_QKV:GDN_QKV + GDN_KDIM]).astype(z_o.dtype)
    ab_o[...] = _dot(xn, w_ref[:, GDN_QKV + GDN_KDIM:])


def _gdn_proj(h3, g_row, w_bf16, conv_w, cbuf8):
    b, tp, d = h3.shape
    n = w_bf16.shape[1]
    tt = _pick_tile(tp, (256, 192, 128, 64))
    tok = lambda width: pl.BlockSpec((None, tt, width), lambda i, t: (i, t, 0))
    hist_spec = pl.BlockSpec((None, SUBLANES, GDN_QKV), lambda i, t: (i, 0, 0))
    return pl.pallas_call(
        _gdn_proj_kernel,
        grid=(b, tp // tt),
        in_specs=[tok(d), _const_spec((1, d)), _const_spec((d, n)), _const_spec((CONV_W, GDN_QKV)), hist_spec],
        out_specs=[tok(GDN_KDIM), tok(GDN_KDIM), tok(GDN_KDIM), tok(GDN_KDIM), tok(2 * LANES), hist_spec],
        out_shape=[jax.ShapeDtypeStruct((b, tp, GDN_KDIM), BF16)] * 4
        + [jax.ShapeDtypeStruct((b, tp, 2 * LANES), F32), jax.ShapeDtypeStruct((b, SUBLANES, GDN_QKV), F32)],
        scratch_shapes=[pltpu.VMEM((SUBLANES, GDN_QKV), F32)],
        compiler_params=pltpu.CompilerParams(dimension_semantics=("arbitrary", "arbitrary"),
                                             vmem_limit_bytes=VMEM_LIMIT),
        name="gdn_in_proj",
    )(h3, g_row, w_bf16, conv_w, cbuf8)


def _gdn_scan_kernel(q_ref, k_ref, v_ref, ab_ref, s0_ref, alog_ref, dtb_ref, onorm_ref, o_ref, s_ref):
    c = pl.program_id(1)
    L = CHUNK

    @pl.when(c == 0)
    def _():
        s_ref[...] = s0_ref[...]

    row = lax.broadcasted_iota(jnp.int32, (L, L), 0)
    col = lax.broadcasted_iota(jnp.int32, (L, L), 1)
    causal = row >= col
    strict = row > col
    tril = causal.astype(BF16)
    eye = (row == col).astype(F32)

    nb = q_ref.shape[0]
    a_in = jnp.concatenate([ab_ref[i, :, :LANES] + dtb_ref[...] for i in range(nb)], axis=1)
    b_in = jnp.concatenate([ab_ref[i, :, LANES:] for i in range(nb)], axis=1)
    neg_a = jnp.concatenate([-jnp.exp(alog_ref[...])] * nb, axis=1)
    g = neg_a * jax.nn.softplus(a_in)
    beta = jax.nn.sigmoid(b_in)
    G = _dot_exact_lhs(tril, g)
    Gt = G.T
    GL = G[L - 1:L, :]
    exp_g = jnp.exp(G)
    exp_gr = jnp.exp(GL - G)
    exp_gl = jnp.exp(GL)

    chains = [(i, h) for i in range(nb) for h in range(GDN_HEADS)]

    def per_head(ref):
        return jnp.stack([ref[i, :, h * GDN_DK:(h + 1) * GDN_DK] for i, h in chains])

    def head_cols(x2d):
        return jnp.stack([x2d[:, i * LANES + h:i * LANES + h + 1] for i, h in chains])

    q = per_head(q_ref).astype(F32)
    kf = per_head(k_ref).astype(F32)
    v = per_head(v_ref).astype(F32)
    q = q * (lax.rsqrt(jnp.sum(q * q, axis=-1, keepdims=True) + L2_EPS) * (GDN_DK ** -0.5))
    kf = kf * lax.rsqrt(jnp.sum(kf * kf, axis=-1, keepdims=True) + L2_EPS)
    k = kf.astype(BF16)
    bcol = head_cols(beta)
    g_col = head_cols(G)
    g_row = jnp.stack([Gt[i * LANES + h:i * LANES + h + 1, :] for i, h in chains])
    eg_col = head_cols(exp_g)
    egr_col = head_cols(exp_gr)
    egl = jnp.stack([exp_gl[:, i * LANES + h:i * LANES + h + 1] for i, h in chains])

    dlt = g_col - g_row
    dec = jnp.where(causal, jnp.exp(jnp.where(causal, dlt, 0.0)), 0.0)
    kq_k = _bdot_nt(jnp.concatenate([k, q.astype(BF16)], axis=1), k)
    a_mat = jnp.where(strict, bcol * kq_k[:, :L] * dec, 0.0)
    qk = kq_k[:, L:] * dec
    t_inv = _neumann(-a_mat, eye, L)
    rhs = jnp.concatenate([v * bcol, kf * (bcol * eg_col)], axis=-1)
    sol = _bdot(t_inv, rhs)
    u = sol[..., :GDN_DV]
    w = sol[..., GDN_DV:]
    s = s_ref[...].reshape(nb * GDN_HEADS, GDN_DK, GDN_DV)
    ws_qs = _bdot(jnp.concatenate([w, q * eg_col], axis=1), s)
    v_new = u - ws_qs[:, :L]
    o = ws_qs[:, L:] + _bdot(qk, v_new)
    s_ref[...] = (s * egl + _bdot_tn(kf * egr_col, v_new)).reshape(s_ref.shape)
    o = o * lax.rsqrt(jnp.mean(o * o, axis=-1, keepdims=True) + EPS) * onorm_ref[...]
    for n, (i, h) in enumerate(chains):
        o_ref[i, :, h * GDN_DV:(h + 1) * GDN_DV] = o[n].astype(o_ref.dtype)


def _scan_batch(b):
    return next(n for n in (4, 2, 1) if b % n == 0)


def _gdn_scan(q, k, v, ab, s0, alog_row, dtb_row, onorm_row):
    b, tp, _ = q.shape
    nb = _scan_batch(b)
    tok = pl.BlockSpec((nb, CHUNK, GDN_KDIM), lambda i, c: (i, c, 0))
    st = pl.BlockSpec((nb, GDN_HEADS, GDN_DK, GDN_DV), lambda i, c: (i, 0, 0, 0))
    return pl.pallas_call(
        _gdn_scan_kernel,
        grid=(b // nb, tp // CHUNK),
        in_specs=[tok, tok, tok, pl.BlockSpec((nb, CHUNK, 2 * LANES), lambda i, c: (i, c, 0)), st,
                  _const_spec((1, LANES)), _const_spec((1, LANES)), _const_spec((1, GDN_DV))],
        out_specs=[tok, st],
        out_shape=[jax.ShapeDtypeStruct((b, tp, GDN_KDIM), BF16),
                   jax.ShapeDtypeStruct((b, GDN_HEADS, GDN_DK, GDN_DV), F32)],
        compiler_params=pltpu.CompilerParams(dimension_semantics=("arbitrary", "arbitrary"),
                                             vmem_limit_bytes=VMEM_LIMIT),
        name="gdn_scan",
    )(q, k, v, ab, s0, alog_row, dtb_row, onorm_row)


def _mix_ffn_kernel(*refs, silu_gate, final_norm):
    if final_norm:
        y_ref, gm_ref, h_ref, wo_ref, nf_ref, wg_ref, wu_ref, wd_ref, fn_ref, o_ref = refs
    else:
        y_ref, gm_ref, h_ref, wo_ref, nf_ref, wg_ref, wu_ref, wd_ref, o_ref = refs
    gm = gm_ref[...].astype(F32)
    if silu_gate:
        gm = jax.nn.silu(gm)
    h1 = h_ref[...] + _dot(y_ref[...].astype(F32) * gm, wo_ref[...])
    n = _rms(h1, nf_ref[...])
    act = jax.nn.silu(_dot(n, wg_ref[...])) * _dot(n, wu_ref[...])
    h2 = h1 + _dot(act, wd_ref[...])
    if final_norm:
        h2 = _rms(h2, fn_ref[...])
    o_ref[...] = h2


def _mix_ffn(y2d, gm2d, h2d, wo, nf_row, wg, wu, wd, fn_row, *, silu_gate, name):
    rows, d = h2d.shape
    dff = wg.shape[1]
    tm = _pick_tile(rows, (512, 256, 128, 64))
    final_norm = fn_row is not None
    tok = pl.BlockSpec((tm, d), lambda i: (i, 0))
    in_specs = [tok, tok, tok, _const_spec((d, d)), _const_spec((1, d)),
                _const_spec((d, dff)), _const_spec((d, dff)), _const_spec((dff, d))]
    args = [y2d, gm2d, h2d, wo, nf_row, wg, wu, wd]
    if final_norm:
        in_specs.append(_const_spec((1, d)))
        args.append(fn_row)
    return pl.pallas_call(
        functools.partial(_mix_ffn_kernel, silu_gate=silu_gate, final_norm=final_norm),
        grid=(rows // tm,),
        in_specs=in_specs,
        out_specs=tok,
        out_shape=jax.ShapeDtypeStruct((rows, d), F32),
        compiler_params=pltpu.CompilerParams(dimension_semantics=("arbitrary",), vmem_limit_bytes=VMEM_LIMIT),
        name=name,
    )(*args)


def _rwkv_proj_kernel(h_ref, sb_ref, vec_ref, mu_ref, wr_ref, wk_ref, wv_ref,
                      w1_ref, w2_ref, a1_ref, a2_ref, g1_ref, g2_ref,
                      r_o, k_o, v_o, kk_o, a_o, w_o, g_o, last_o, hs_ref):
    t = pl.program_id(1)
    tt = h_ref.shape[0]
    hist = SUBLANES
    hn = _rms(h_ref[...], vec_ref[0:1, :])

    @pl.when(t == 0)
    def _():
        hs_ref[...] = jnp.broadcast_to(sb_ref[...], hs_ref.shape)

    prev = pltpu.roll(jnp.concatenate([hs_ref[...], hn], axis=0), 1, 0)[hist:]
    hs_ref[hist - 1:hist, :] = hn[tt - 1:tt, :]
    last_o[...] = hn[tt - 1:tt, :]
    xx = prev - hn

    def mix(i):
        return (hn + xx * mu_ref[i:i + 1, :]).astype(BF16)

    r = _dot(mix(0), wr_ref[...])
    k = _dot(mix(2), wk_ref[...])
    v = _dot(mix(3), wv_ref[...])
    wl = vec_ref[1:2, :] + _dot(jnp.tanh(_dot(mix(1), w1_ref[...])), w2_ref[...])
    a = jax.nn.sigmoid(vec_ref[2:3, :] + _dot(_dot(mix(4), a1_ref[...]), a2_ref[...]))
    gate = _dot(jax.nn.sigmoid(_dot(mix(5), g1_ref[...])), g2_ref[...])
    r_o[...] = r.astype(r_o.dtype)
    k_o[...] = (k * (1.0 + (a - 1.0) * vec_ref[4:5, :])).astype(k_o.dtype)
    v_o[...] = v.astype(v_o.dtype)
    kk_o[...] = (k * vec_ref[3:4, :]).astype(kk_o.dtype)
    a_o[...] = a.astype(a_o.dtype)
    w_o[...] = -DECAY_SCALE * jax.nn.sigmoid(wl)
    g_o[...] = gate.astype(g_o.dtype)


def _rwkv_proj(h3, sbuf, vec, mu8, wr, wk, wv, w1, w2, a1, a2, g1, g2):
    b, tp, d = h3.shape
    tt = _pick_tile(tp, (512, 256, 192, 128, 64))
    tok = pl.BlockSpec((None, tt, d), lambda i, t: (i, t, 0))
    row = pl.BlockSpec((None, 1, d), lambda i, t: (i, 0, 0))
    weights = [wr, wk, wv, w1, w2, a1, a2, g1, g2]
    tok_shape = lambda dt: jax.ShapeDtypeStruct((b, tp, d), dt)
    return pl.pallas_call(
        _rwkv_proj_kernel,
        grid=(b, tp // tt),
        in_specs=[tok, row, _const_spec(vec.shape), _const_spec(mu8.shape)] + [_const_spec(w.shape) for w in weights],
        out_specs=[tok] * 7 + [row],
        out_shape=[tok_shape(BF16)] * 5 + [tok_shape(F32), tok_shape(BF16), jax.ShapeDtypeStruct((b, 1, d), F32)],
        scratch_shapes=[pltpu.VMEM((SUBLANES, d), F32)],
        compiler_params=pltpu.CompilerParams(dimension_semantics=("arbitrary", "arbitrary"),
                                             vmem_limit_bytes=VMEM_LIMIT),
        name="rwkv_proj",
    )(h3, sbuf, vec, mu8, *weights)


def _rwkv_scan_kernel(r_ref, k_ref, v_ref, kk_ref, a_ref, w_ref, s0_ref, rk_ref, lnw_ref, lnb_ref,
                      yo_ref, s_ref):
    c = pl.program_id(1)
    L = CHUNK
    N = RWKV_N
    G = RWKV_GROUP
    Q = G * N
    nb, _, d = r_ref.shape
    groups = range(nb * d // Q)

    def cat_batch(ref):
        return jnp.concatenate([ref[i] for i in range(nb)], axis=1)

    @pl.when(c == 0)
    def _():
        s_ref[...] = s0_ref[...]

    row = lax.broadcasted_iota(jnp.int32, (L, L), 0)
    col = lax.broadcasted_iota(jnp.int32, (L, L), 1)
    tril = (row >= col).astype(BF16)
    t_idx = lax.broadcasted_iota(jnp.int32, (L, Q), 0)
    lane = lax.broadcasted_iota(jnp.int32, (L, Q), 1)
    s_idx = lane % N
    strict_c = t_idx > s_idx
    incl_c = t_idx >= s_idx
    eye_c = (t_idx == s_idx).astype(F32)
    same_head_sq = (lax.broadcasted_iota(jnp.int32, (Q, Q), 0) // N
                    == lax.broadcasted_iota(jnp.int32, (Q, Q), 1) // N)
    ones_blk = same_head_sq.astype(BF16)
    lane_blk = lax.broadcasted_iota(jnp.int32, (L, LANES), 1) // N

    def per_group(x2d):
        return jnp.stack([x2d[:, q * Q:(q + 1) * Q] for q in groups])

    def expand(x3):
        xb = x3.astype(BF16)
        zero = jnp.zeros(xb.shape[:2] + (LANES,), BF16)
        blocks = []
        for g in range(G):
            col, slot = divmod(g * N, LANES)
            piece = jnp.where(lane_blk == slot // N, xb[:, :, col * LANES:(col + 1) * LANES], zero)
            blocks.append(jnp.concatenate([piece if c == col else zero for c in range(Q // LANES)], axis=2))
        return jnp.concatenate(blocks, axis=1)

    def lane_sums(x3):
        n, rws, _ = x3.shape
        x2 = x3.reshape(n * rws, Q)
        return sum(_dot(t, ones_blk) for t in _split_bf16(x2, 2)).reshape(n, rws, Q)

    w_all = cat_batch(w_ref)
    cw_all = _dot_exact_lhs(tril, w_all)
    cwl = cw_all[L - 1:L, :]
    p_in = per_group(jnp.exp(cw_all))
    p_ex = per_group(jnp.exp(cw_all - w_all))
    p_inv = per_group(jnp.exp(-cw_all))
    p_rem = per_group(jnp.exp(cwl - cw_all))
    dec_all = jnp.exp(cwl)
    p_all = jnp.stack([dec_all[:, q * Q:(q + 1) * Q] for q in groups])

    kkr = per_group(cat_batch(kk_ref).astype(F32))
    kkn = kkr * lax.rsqrt(lane_sums(kkr * kkr) + L2_EPS)
    rp = per_group(cat_batch(r_ref).astype(F32))
    kp = per_group(cat_batch(k_ref).astype(F32))
    vp = per_group(cat_batch(v_ref).astype(F32))
    bb = kkn * per_group(cat_batch(a_ref).astype(F32))

    lhs = jnp.concatenate([-kkn * p_ex, rp * p_in], axis=1).astype(BF16)
    m = _bdot_nt(lhs, jnp.concatenate([expand(bb * p_inv), expand(kp * p_inv)], axis=1))
    l_ab = jnp.where(strict_c, m[:, :L, :Q], 0.0)
    l_ak = jnp.where(strict_c, m[:, :L, Q:], 0.0)
    m_rb = jnp.where(incl_c, m[:, L:, :Q], 0.0)
    m_rk = jnp.where(incl_c, m[:, L:, Q:], 0.0)

    s = s_ref[...].reshape(len(groups), Q, Q)
    from_s = _bdot_nt(lhs, s)

    t_inv = eye_c + l_ab
    pw = _bdot(l_ab, expand(l_ab))
    n = 4
    while n < L:
        both = _bdot(jnp.concatenate([pw, t_inv], axis=1), expand(pw))
        t_inv = t_inv + both[:, L:]
        pw = both[:, :L]
        n *= 2
    t_inv = t_inv + _bdot(t_inv, expand(pw))

    ve = expand(vp)
    u = _bdot(t_inv, expand(from_s[:, :L] + _bdot(l_ak, ve)))
    y = from_s[:, L:] + _bdot(jnp.concatenate([m_rb, m_rk], axis=2),
                              jnp.concatenate([expand(u), ve], axis=1))
    cross = _bdot_tn(jnp.concatenate([u, vp], axis=1), jnp.concatenate([bb * p_rem, kp * p_rem], axis=1))
    s_ref[...] = (s * p_all + jnp.where(same_head_sq, cross, 0.0)).reshape(s_ref.shape)

    mean = lane_sums(y) * (1.0 / N)
    cy = y - mean
    var = lane_sums(cy * cy) * (1.0 / N)
    rk3 = per_group(jnp.concatenate([rk_ref[...]] * nb, axis=1))
    bonus = lane_sums(rp * kp * rk3) * vp
    for g in groups:
        i, q = divmod(g, d // Q)
        sl = slice(q * Q, (q + 1) * Q)
        yn = cy[g] * lax.rsqrt(var[g] + GN_EPS) * lnw_ref[:, sl] + lnb_ref[:, sl]
        yo_ref[i, :, sl] = (yn + bonus[g]).astype(yo_ref.dtype)


def _rwkv_scan(r, k, v, kk, a, w, s0_groups, rk_row, lnw_row, lnb_row):
    b, tp, d = r.shape
    q = RWKV_GROUP * RWKV_N
    nb = _scan_batch(b)
    tok = pl.BlockSpec((nb, CHUNK, d), lambda i, c: (i, c, 0))
    st = pl.BlockSpec((nb, d // q, q, q), lambda i, c: (i, 0, 0, 0))
    return pl.pallas_call(
        _rwkv_scan_kernel,
        grid=(b // nb, tp // CHUNK),
        in_specs=[tok] * 6 + [st, _const_spec((1, d)), _const_spec((1, d)), _const_spec((1, d))],
        out_specs=[tok, st],
        out_shape=[jax.ShapeDtypeStruct((b, tp, d), BF16),
                   jax.ShapeDtypeStruct((b, d // q, q, q), F32)],
        compiler_params=pltpu.CompilerParams(dimension_semantics=("arbitrary", "arbitrary"),
                                             vmem_limit_bytes=VMEM_LIMIT),
        name="rwkv_scan",
    )(r, k, v, kk, a, w, s0_groups, rk_row, lnw_row, lnb_row)


def _pad_cols(w, n):
    return jnp.pad(w, ((0, 0), (0, n - w.shape[1])))


def _pad_rows(w, n):
    return jnp.pad(w, ((0, n - w.shape[0]), (0, 0)))


def _groups_from_state(s):
    b, h, n, _ = s.shape
    g = RWKV_GROUP
    eye = jnp.eye(g, dtype=s.dtype)
    return jnp.einsum("bqgij,gh->bqgihj", s.reshape(b, h // g, g, n, n), eye).reshape(b, h // g, g * n, g * n)


def _state_from_groups(sg):
    b, hq, gn, _ = sg.shape
    g = RWKV_GROUP
    n = gn // g
    return jnp.einsum("bqgigj->bqgij", sg.reshape(b, hq, g, n, g, n)).reshape(b, hq * g, n, n)


def _run_trunk(h, conv_bufs, gdn_states, shift_bufs, rwkv_states, wts):
    b, tp, d = h.shape
    rows = b * tp
    depth = wts["norm_mix"].shape[0]
    new_conv, new_gdn, new_shift, new_rwkv = [], [], [], []
    for i in range(depth):
        j = i // 2
        fn_row = wts["norm_final"] if i == depth - 1 else None
        if i % 2 == 0:
            g = wts["gdn"][j]
            cbuf8 = jnp.pad(conv_bufs[j], ((0, 0), (SUBLANES - (CONV_W - 1), 0), (0, 0)))
            q, k, v, z, ab, cst = _gdn_proj(h, wts["norm_mix"][i:i + 1], g["w_in"], g["conv_w"], cbuf8)
            o, s_new = _gdn_scan(q, k, v, ab, gdn_states[j], g["a_log"], g["dt_bias"], g["o_norm"])
            new_conv.append(cst[:, SUBLANES - (CONV_W - 1):])
            new_gdn.append(s_new)
            h2 = _mix_ffn(o.reshape(rows, d), z.reshape(rows, d), h.reshape(rows, d), g["w_out"],
                          wts["norm_ffn"][i:i + 1], wts["ffn_gate"][i], wts["ffn_up"][i], wts["ffn_down"][i],
                          fn_row, silu_gate=True, name="gdn_out_ffn")
        else:
            rw = wts["rwkv"][j]
            vec = jnp.concatenate([wts["norm_mix"][i:i + 1], rw["vec"]], axis=0)
            r, k, v, kk, a, w, gate, last_hn = _rwkv_proj(
                h, shift_bufs[j], vec, rw["mu"], rw["w_r"], rw["w_k"], rw["w_v"],
                rw["w1"], rw["w2"], rw["a1"], rw["a2"], rw["g1"], rw["g2"])
            yo, sp_new = _rwkv_scan(r, k, v, kk, a, w, _groups_from_state(rwkv_states[j]),
                                    rw["r_k"], rw["ln_w"], rw["ln_b"])
            new_shift.append(last_hn)
            new_rwkv.append(_state_from_groups(sp_new))
            h2 = _mix_ffn(yo.reshape(rows, d), gate.reshape(rows, d), h.reshape(rows, d), rw["w_o"],
                          wts["norm_ffn"][i:i + 1], wts["ffn_gate"][i], wts["ffn_up"][i], wts["ffn_down"][i],
                          fn_row, silu_gate=False, name="rwkv_out_ffn")
        h = h2.reshape(b, tp, d)
    return h, jnp.stack(new_conv), jnp.stack(new_gdn), jnp.stack(new_shift), jnp.stack(new_rwkv)


def _prepare_weights(norm_mix, norm_ffn, norm_final, gdn_w_in, gdn_conv_w, gdn_a_log, gdn_dt_bias, gdn_o_norm,
                     gdn_w_out, rwkv_mu, rwkv_w0, rwkv_w1, rwkv_w2, rwkv_a0, rwkv_a1, rwkv_a2, rwkv_g1, rwkv_g2,
                     rwkv_k_k, rwkv_k_a, rwkv_r_k, rwkv_w_r, rwkv_w_k, rwkv_w_v, rwkv_w_o, rwkv_ln_w,
                     rwkv_ln_b, ffn_w_gate, ffn_w_up, ffn_w_down):
    d = norm_mix.shape[1]
    gdn = []
    for j in range(gdn_w_in.shape[0]):
        w = gdn_w_in[j]
        qkvz = w[:, :GDN_QKV + GDN_KDIM]
        a_cols = _pad_cols(w[:, GDN_QKV + GDN_KDIM:GDN_QKV + GDN_KDIM + GDN_HEADS], LANES)
        b_cols = _pad_cols(w[:, GDN_QKV + GDN_KDIM + GDN_HEADS:], LANES)
        gdn.append(dict(
            w_in=jnp.concatenate([qkvz, a_cols, b_cols], axis=1).astype(BF16),
            conv_w=gdn_conv_w[j],
            a_log=_pad_cols(gdn_a_log[j][None], LANES),
            dt_bias=_pad_cols(gdn_dt_bias[j][None], LANES),
            o_norm=gdn_o_norm[j][None],
            w_out=gdn_w_out[j].astype(BF16),
        ))
    rwkv = []
    for j in range(rwkv_mu.shape[0]):
        lw = -(-rwkv_w1.shape[2] // LANES) * LANES
        la = -(-rwkv_a1.shape[2] // LANES) * LANES
        lg = -(-rwkv_g1.shape[2] // LANES) * LANES
        rwkv.append(dict(
            vec=jnp.stack([rwkv_w0[j], rwkv_a0[j], rwkv_k_k[j], rwkv_k_a[j],
                           jnp.zeros_like(rwkv_w0[j]), jnp.zeros_like(rwkv_w0[j]), jnp.zeros_like(rwkv_w0[j])]),
            mu=_pad_rows(rwkv_mu[j], SUBLANES),
            w_r=rwkv_w_r[j].astype(BF16), w_k=rwkv_w_k[j].astype(BF16), w_v=rwkv_w_v[j].astype(BF16),
            w_o=rwkv_w_o[j].astype(BF16),
            w1=_pad_cols(rwkv_w1[j], lw).astype(BF16), w2=_pad_rows(rwkv_w2[j], lw).astype(BF16),
            a1=_pad_cols(rwkv_a1[j], la).astype(BF16), a2=_pad_rows(rwkv_a2[j], la).astype(BF16),
            g1=_pad_cols(rwkv_g1[j], lg).astype(BF16), g2=_pad_rows(rwkv_g2[j], lg).astype(BF16),
            r_k=rwkv_r_k[j].reshape(1, d), ln_w=rwkv_ln_w[j][None], ln_b=rwkv_ln_b[j][None],
        ))
    return dict(norm_mix=norm_mix, norm_ffn=norm_ffn, norm_final=norm_final[None], gdn=gdn, rwkv=rwkv,
                ffn_gate=ffn_w_gate.astype(BF16), ffn_up=ffn_w_up.astype(BF16), ffn_down=ffn_w_down.astype(BF16))


def kernel(x_prompt, x_sample, cache_gdn_conv, state_gdn, cache_rwkv_shift, state_rwkv, meta_tokens, norm_mix, norm_ffn, norm_final, gdn_w_in, gdn_conv_w, gdn_a_log, gdn_dt_bias, gdn_o_norm, gdn_w_out, rwkv_mu, rwkv_w0, rwkv_w1, rwkv_w2, rwkv_a0, rwkv_a1, rwkv_a2, rwkv_g1, rwkv_g2, rwkv_k_k, rwkv_k_a, rwkv_r_k, rwkv_w_r, rwkv_w_k, rwkv_w_v, rwkv_w_o, rwkv_ln_w, rwkv_ln_b, ffn_w_gate, ffn_w_up, ffn_w_down):
    wts = _prepare_weights(norm_mix, norm_ffn, norm_final, gdn_w_in, gdn_conv_w, gdn_a_log, gdn_dt_bias,
                           gdn_o_norm, gdn_w_out, rwkv_mu, rwkv_w0, rwkv_w1, rwkv_w2, rwkv_a0, rwkv_a1, rwkv_a2,
                           rwkv_g1, rwkv_g2, rwkv_k_k, rwkv_k_a, rwkv_r_k, rwkv_w_r, rwkv_w_k, rwkv_w_v,
                           rwkv_w_o, rwkv_ln_w, rwkv_ln_b, ffn_w_gate, ffn_w_up, ffn_w_down)
    n_gdn = gdn_w_in.shape[0]
    n_rwkv = rwkv_mu.shape[0]
    b, seq, d = x_prompt.shape
    dt = x_prompt.dtype

    lead_pad = (-N_META) % CHUNK
    lead = jnp.concatenate([jnp.zeros((b, lead_pad, d), dt),
                            jnp.broadcast_to(meta_tokens.astype(dt)[None], (b, N_META, d))], axis=1)
    zeros_like_states = (jnp.zeros((n_gdn, b, CONV_W - 1, GDN_QKV), dt),
                         jnp.zeros((n_gdn, b, GDN_HEADS, GDN_DK, GDN_DV), dt),
                         jnp.zeros((n_rwkv, b, 1, d), dt),
                         jnp.zeros((n_rwkv, b, d // RWKV_N, RWKV_N, RWKV_N), dt))
    sample_states = (cache_gdn_conv, state_gdn, cache_rwkv_shift, state_rwkv)

    bs = x_sample.shape[0]
    if x_sample.shape[1:] == lead.shape[1:]:
        both = _run_trunk(jnp.concatenate([x_sample, lead], axis=0),
                          *[jnp.concatenate([s, z], axis=1) for s, z in zip(sample_states, zeros_like_states)],
                          wts)
        y_sample, s_conv, s_gdn, s_shift, s_rwkv = [both[0][:bs]] + [t[:, :bs] for t in both[1:]]
        lead_states = [t[:, bs:] for t in both[1:]]
    else:
        y_sample, s_conv, s_gdn, s_shift, s_rwkv = _run_trunk(x_sample, *sample_states, wts)
        lead_states = _run_trunk(lead, *zeros_like_states, wts)[1:]

    y_prompt, p_conv, p_gdn, p_shift, p_rwkv = _run_trunk(x_prompt, *lead_states, wts)

    return (y_prompt, y_sample, p_conv, p_gdn, p_shift, p_rwkv, s_conv, s_gdn, s_shift, s_rwkv)
```

```python
import functools

import jax
import jax.numpy as jnp
from jax import lax
from jax.experimental import pallas as pl
from jax.experimental.pallas import tpu as pltpu

F32 = jnp.float32
BF16 = jnp.bfloat16

CHUNK = 64
N_META = 16
EPS = 1e-6
L2_EPS = 1e-6
GN_EPS = 64e-5

GDN_HEADS = 8
GDN_DK = 128
GDN_DV = 128
GDN_KDIM = GDN_HEADS * GDN_DK
GDN_QKV = 3 * GDN_KDIM
CONV_W = 4
RWKV_N = 64
RWKV_GROUP = 4
DECAY_SCALE = 0.6065306597126334

LANES = 128
SUBLANES = 8
VMEM_LIMIT = 56 * 1024 * 1024


def _dot(a, b):
    return jnp.dot(a.astype(BF16), b.astype(BF16), preferred_element_type=F32)


def _bdot(a, b):
    return lax.dot_general(a.astype(BF16), b.astype(BF16), (((2,), (1,)), ((0,), (0,))),
                           preferred_element_type=F32)


def _bdot_nt(a, b):
    return lax.dot_general(a.astype(BF16), b.astype(BF16), (((2,), (2,)), ((0,), (0,))),
                           preferred_element_type=F32)


def _bdot_tn(a, b):
    return lax.dot_general(a.astype(BF16), b.astype(BF16), (((1,), (1,)), ((0,), (0,))),
                           preferred_element_type=F32)


def _split_bf16(x, parts):
    out = []
    for _ in range(parts - 1):
        hi = x.astype(BF16)
        out.append(hi)
        x = x - hi.astype(F32)
    out.append(x.astype(BF16))
    return out


def _dot_exact_lhs(m01, x, parts=3):
    return sum(_dot(m01, t) for t in _split_bf16(x, parts))


def _rms(x, g, eps=EPS):
    return x * lax.rsqrt(jnp.mean(x * x, axis=-1, keepdims=True) + eps) * g


def _neumann(x, eye, size):
    rows = x.shape[1]
    t = eye + x
    p = _bdot(x, x)
    n = 4
    while n < size:
        both = _bdot(jnp.concatenate([p, t], axis=1), p)
        t = t + both[:, rows:]
        p = both[:, :rows]
        n *= 2
    return t + _bdot(t, p)


def _pick_tile(n, candidates):
    for c in candidates:
        if n % c == 0:
            return c
    raise ValueError(f"no tile for {n}")


def _const_spec(shape):
    nd = len(shape)
    return pl.BlockSpec(shape, lambda *_: (0,) * nd, pipeline_mode=pl.Buffered(1))


def _gdn_proj_kernel(x_ref, g_ref, w_ref, cw_ref, cbuf_ref, q_o, k_o, v_o, z_o, ab_o, cst_o, xs_ref):
    t = pl.program_id(1)
    tt = x_ref.shape[0]
    hist = SUBLANES

    @pl.when(t == 0)
    def _():
        xs_ref[...] = cbuf_ref[...]

    xn = _rms(x_ref[...], g_ref[...]).astype(BF16)

    for out, c0 in ((q_o, 0), (k_o, GDN_KDIM), (v_o, 2 * GDN_KDIM)):
        cols = slice(c0, c0 + GDN_KDIM)
        proj = _dot(xn, w_ref[:, cols])
        win = jnp.concatenate([xs_ref[:, cols], proj], axis=0)
        xs_ref[:, cols] = proj[tt - hist:tt]
        cst_o[:, cols] = proj[tt - hist:tt]
        win1 = pltpu.roll(win, 1, 0)
        near = cw_ref[3:4, cols] * win + cw_ref[2:3, cols] * win1
        far = cw_ref[1:2, cols] * win + cw_ref[0:1, cols] * win1
        out[...] = (near + pltpu.roll(far, 2, 0))[hist:].astype(out.dtype)
    z_o[...] = _dot(xn, w_ref[:, GDN_QKV:GDN_QKV + GDN_KDIM]).astype(z_o.dtype)
    ab_o[...] = _dot(xn, w_ref[:, GDN_QKV + GDN_KDIM:])


def _gdn_proj(h3, g_row, w_bf16, conv_w, cbuf8):
    b, tp, d = h3.shape
    n = w_bf16.shape[1]
    tt = _pick_tile(tp, (256, 192, 128, 64))
    tok = lambda width: pl.BlockSpec((None, tt, width), lambda i, t: (i, t, 0))
    hist_spec = pl.BlockSpec((None, SUBLANES, GDN_QKV), lambda i, t: (i, 0, 0))
    return pl.pallas_call(
        _gdn_proj_kernel,
        grid=(b, tp // tt),
        in_specs=[tok(d), _const_spec((1, d)), _const_spec((d, n)), _const_spec((CONV_W, GDN_QKV)), hist_spec],
        out_specs=[tok(GDN_KDIM), tok(GDN_KDIM), tok(GDN_KDIM), tok(GDN_KDIM), tok(2 * LANES), hist_spec],
        out_shape=[jax.ShapeDtypeStruct((b, tp, GDN_KDIM), BF16)] * 4
        + [jax.ShapeDtypeStruct((b, tp, 2 * LANES), F32), jax.ShapeDtypeStruct((b, SUBLANES, GDN_QKV), F32)],
        scratch_shapes=[pltpu.VMEM((SUBLANES, GDN_QKV), F32)],
        compiler_params=pltpu.CompilerParams(dimension_semantics=("arbitrary", "arbitrary"),
                                             vmem_limit_bytes=VMEM_LIMIT),
        name="gdn_in_proj",
    )(h3, g_row, w_bf16, conv_w, cbuf8)


def _gdn_scan_kernel(q_ref, k_ref, v_ref, ab_ref, s0_ref, alog_ref, dtb_ref, onorm_ref, o_ref, s_ref):
    c = pl.program_id(1)
    L = CHUNK

    @pl.when(c == 0)
    def _():
        s_ref[...] = s0_ref[...]

    row = lax.broadcasted_iota(jnp.int32, (L, L), 0)
    col = lax.broadcasted_iota(jnp.int32, (L, L), 1)
    causal = row >= col
    strict = row > col
    tril = causal.astype(BF16)
    eye = (row == col).astype(F32)

    nb = q_ref.shape[0]
    a_in = jnp.concatenate([ab_ref[i, :, :LANES] + dtb_ref[...] for i in range(nb)], axis=1)
    b_in = jnp.concatenate([ab_ref[i, :, LANES:] for i in range(nb)], axis=1)
    neg_a = jnp.concatenate([-jnp.exp(alog_ref[...])] * nb, axis=1)
    g = neg_a * jax.nn.softplus(a_in)
    beta = jax.nn.sigmoid(b_in)
    G = _dot_exact_lhs(tril, g)
    Gt = G.T

    chains = [(i, h) for i in range(nb) for h in range(GDN_HEADS)]

    def per_head(ref):
        return jnp.stack([ref[i, :, h * GDN_DK:(h + 1) * GDN_DK] for i, h in chains])

    def head_cols(x2d):
        return jnp.stack([x2d[:, i * LANES + h:i * LANES + h + 1] for i, h in chains])

    q = jax.nn.silu(per_head(q_ref).astype(F32))
    kf = jax.nn.silu(per_head(k_ref).astype(F32))
    v = jax.nn.silu(per_head(v_ref).astype(F32))
    q = q * (lax.rsqrt(jnp.sum(q * q, axis=-1, keepdims=True) + L2_EPS) * (GDN_DK ** -0.5))
    kf = kf * lax.rsqrt(jnp.sum(kf * kf, axis=-1, keepdims=True) + L2_EPS)
    k = kf.astype(BF16)
    bcol = head_cols(beta)
    g_col = head_cols(G)
    g_row = jnp.stack([Gt[i * LANES + h:i * LANES + h + 1, :] for i, h in chains])
    g_last = g_col[:, L - 1:L, :]
    eg_col = jnp.exp(g_col)
    egr_col = jnp.exp(g_last - g_col)
    egl = jnp.exp(g_last)

    dlt = g_col - g_row
    dec = jnp.where(causal, jnp.exp(jnp.where(causal, dlt, 0.0)), 0.0)
    kq_k = _bdot_nt(jnp.concatenate([k, q.astype(BF16)], axis=1), k)
    a_mat = jnp.where(strict, bcol * kq_k[:, :L] * dec, 0.0)
    qk = kq_k[:, L:] * dec
    t_inv = _neumann(-a_mat, eye, L)
    rhs = jnp.concatenate([v * bcol, kf * (bcol * eg_col)], axis=-1)
    sol = _bdot(t_inv, rhs)
    u = sol[..., :GDN_DV]
    w = sol[..., GDN_DV:]
    s = s_ref[...].reshape(nb * GDN_HEADS, GDN_DK, GDN_DV)
    ws_qs = _bdot(jnp.concatenate([w, q * eg_col], axis=1), s)
    v_new = u - ws_qs[:, :L]
    o = ws_qs[:, L:] + _bdot(qk, v_new)
    s_ref[...] = (s * egl + _bdot_tn(kf * egr_col, v_new)).reshape(s_ref.shape)
    o = o * lax.rsqrt(jnp.mean(o * o, axis=-1, keepdims=True) + EPS) * onorm_ref[...]
    for n, (i, h) in enumerate(chains):
        o_ref[i, :, h * GDN_DV:(h + 1) * GDN_DV] = o[n].astype(o_ref.dtype)


def _scan_batch(b):
    return next(n for n in (4, 2, 1) if b % n == 0)


def _gdn_scan(q, k, v, ab, s0, alog_row, dtb_row, onorm_row):
    b, tp, _ = q.shape
    nb = _scan_batch(b)
    tok = pl.BlockSpec((nb, CHUNK, GDN_KDIM), lambda i, c: (i, c, 0))
    st = pl.BlockSpec((nb, GDN_HEADS, GDN_DK, GDN_DV), lambda i, c: (i, 0, 0, 0))
    return pl.pallas_call(
        _gdn_scan_kernel,
        grid=(b // nb, tp // CHUNK),
        in_specs=[tok, tok, tok, pl.BlockSpec((nb, CHUNK, 2 * LANES), lambda i, c: (i, c, 0)), st,
                  _const_spec((1, LANES)), _const_spec((1, LANES)), _const_spec((1, GDN_DV))],
        out_specs=[tok, st],
        out_shape=[jax.ShapeDtypeStruct((b, tp, GDN_KDIM), BF16),
                   jax.ShapeDtypeStruct((b, GDN_HEADS, GDN_DK, GDN_DV), F32)],
        compiler_params=pltpu.CompilerParams(dimension_semantics=("arbitrary", "arbitrary"),
                                             vmem_limit_bytes=VMEM_LIMIT),
        name="gdn_scan",
    )(q, k, v, ab, s0, alog_row, dtb_row, onorm_row)


def _mix_ffn_kernel(*refs, silu_gate, final_norm):
    if final_norm:
        y_ref, gm_ref, h_ref, wo_ref, nf_ref, wg_ref, wu_ref, wd_ref, fn_ref, o_ref = refs
    else:
        y_ref, gm_ref, h_ref, wo_ref, nf_ref, wg_ref, wu_ref, wd_ref, o_ref = refs
    gm = gm_ref[...].astype(F32)
    if silu_gate:
        gm = jax.nn.silu(gm)
    h1 = h_ref[...] + _dot(y_ref[...].astype(F32) * gm, wo_ref[...])
    n = _rms(h1, nf_ref[...])
    act = jax.nn.silu(_dot(n, wg_ref[...])) * _dot(n, wu_ref[...])
    h2 = h1 + _dot(act, wd_ref[...])
    if final_norm:
        h2 = _rms(h2, fn_ref[...])
    o_ref[...] = h2


def _mix_ffn(y2d, gm2d, h2d, wo, nf_row, wg, wu, wd, fn_row, *, silu_gate, name):
    rows, d = h2d.shape
    dff = wg.shape[1]
    tm = _pick_tile(rows, (512, 256, 128, 64))
    final_norm = fn_row is not None
    tok = pl.BlockSpec((tm, d), lambda i: (i, 0))
    in_specs = [tok, tok, tok, _const_spec((d, d)), _const_spec((1, d)),
                _const_spec((d, dff)), _const_spec((d, dff)), _const_spec((dff, d))]
    args = [y2d, gm2d, h2d, wo, nf_row, wg, wu, wd]
    if final_norm:
        in_specs.append(_const_spec((1, d)))
        args.append(fn_row)
    return pl.pallas_call(
        functools.partial(_mix_ffn_kernel, silu_gate=silu_gate, final_norm=final_norm),
        grid=(rows // tm,),
        in_specs=in_specs,
        out_specs=tok,
        out_shape=jax.ShapeDtypeStruct((rows, d), F32),
        compiler_params=pltpu.CompilerParams(dimension_semantics=("arbitrary",), vmem_limit_bytes=VMEM_LIMIT),
        name=name,
    )(*args)


def _rwkv_proj_kernel(h_ref, sb_ref, vec_ref, mu_ref, wr_ref, wk_ref, wv_ref,
                      w1_ref, w2_ref, a1_ref, a2_ref, g1_ref, g2_ref,
                      r_o, k_o, v_o, kk_o, a_o, w_o, g_o, last_o, hs_ref):
    t = pl.program_id(1)
    tt = h_ref.shape[0]
    hist = SUBLANES
    hn = _rms(h_ref[...], vec_ref[0:1, :])

    @pl.when(t == 0)
    def _():
        hs_ref[...] = jnp.broadcast_to(sb_ref[...], hs_ref.shape)

    prev = pltpu.roll(jnp.concatenate([hs_ref[...], hn], axis=0), 1, 0)[hist:]
    hs_ref[hist - 1:hist, :] = hn[tt - 1:tt, :]
    last_o[...] = hn[tt - 1:tt, :]
    xx = prev - hn

    def mix(i):
        return (hn + xx * mu_ref[i:i + 1, :]).astype(BF16)

    r = _dot(mix(0), wr_ref[...])
    k = _dot(mix(2), wk_ref[...])
    v = _dot(mix(3), wv_ref[...])
    wl = vec_ref[1:2, :] + _dot(jnp.tanh(_dot(mix(1), w1_ref[...])), w2_ref[...])
    a = jax.nn.sigmoid(vec_ref[2:3, :] + _dot(_dot(mix(4), a1_ref[...]), a2_ref[...]))
    gate = _dot(jax.nn.sigmoid(_dot(mix(5), g1_ref[...])), g2_ref[...])
    r_o[...] = r.astype(r_o.dtype)
    k_o[...] = (k * (1.0 + (a - 1.0) * vec_ref[4:5, :])).astype(k_o.dtype)
    v_o[...] = v.astype(v_o.dtype)
    kk_o[...] = (k * vec_ref[3:4, :]).astype(kk_o.dtype)
    a_o[...] = a.astype(a_o.dtype)
    w_o[...] = -DECAY_SCALE * jax.nn.sigmoid(wl)
    g_o[...] = gate.astype(g_o.dtype)


def _rwkv_proj(h3, sbuf, vec, mu8, wr, wk, wv, w1, w2, a1, a2, g1, g2):
    b, tp, d = h3.shape
    tt = _pick_tile(tp, (512, 256, 192, 128, 64))
    tok = pl.BlockSpec((None, tt, d), lambda i, t: (i, t, 0))
    row = pl.BlockSpec((None, 1, d), lambda i, t: (i, 0, 0))
    weights = [wr, wk, wv, w1, w2, a1, a2, g1, g2]
    tok_shape = lambda dt: jax.ShapeDtypeStruct((b, tp, d), dt)
    return pl.pallas_call(
        _rwkv_proj_kernel,
        grid=(b, tp // tt),
        in_specs=[tok, row, _const_spec(vec.shape), _const_spec(mu8.shape)] + [_const_spec(w.shape) for w in weights],
        out_specs=[tok] * 7 + [row],
        out_shape=[tok_shape(BF16)] * 5 + [tok_shape(F32), tok_shape(BF16), jax.ShapeDtypeStruct((b, 1, d), F32)],
        scratch_shapes=[pltpu.VMEM((SUBLANES, d), F32)],
        compiler_params=pltpu.CompilerParams(dimension_semantics=("arbitrary", "arbitrary"),
                                             vmem_limit_bytes=VMEM_LIMIT),
        name="rwkv_proj",
    )(h3, sbuf, vec, mu8, *weights)


def _rwkv_scan_kernel(r_ref, k_ref, v_ref, kk_ref, a_ref, w_ref, s0_ref, rk_ref, lnw_ref, lnb_ref,
                      yo_ref, s_ref):
    c = pl.program_id(1)
    L = CHUNK
    N = RWKV_N
    G = RWKV_GROUP
    Q = G * N
    nb, _, d = r_ref.shape
    groups = range(nb * d // Q)

    def cat_batch(ref):
        return jnp.concatenate([ref[i] for i in range(nb)], axis=1)

    @pl.when(c == 0)
    def _():
        s_ref[...] = s0_ref[...]

    row = lax.broadcasted_iota(jnp.int32, (L, L), 0)
    col = lax.broadcasted_iota(jnp.int32, (L, L), 1)
    tril = (row >= col).astype(BF16)
    t_idx = lax.broadcasted_iota(jnp.int32, (L, Q), 0)
    lane = lax.broadcasted_iota(jnp.int32, (L, Q), 1)
    s_idx = lane % N
    strict_c = t_idx > s_idx
    incl_c = t_idx >= s_idx
    eye_c = (t_idx == s_idx).astype(F32)
    same_head_sq = (lax.broadcasted_iota(jnp.int32, (Q, Q), 0) // N
                    == lax.broadcasted_iota(jnp.int32, (Q, Q), 1) // N)
    ones_blk = same_head_sq.astype(BF16)
    lane_blk = lax.broadcasted_iota(jnp.int32, (L, LANES), 1) // N

    def per_group(x2d):
        return jnp.stack([x2d[:, q * Q:(q + 1) * Q] for q in groups])

    def expand(x3):
        xb = x3.astype(BF16)
        zero = jnp.zeros(xb.shape[:2] + (LANES,), BF16)
        blocks = []
        for g in range(G):
            col, slot = divmod(g * N, LANES)
            piece = jnp.where(lane_blk == slot // N, xb[:, :, col * LANES:(col + 1) * LANES], zero)
            blocks.append(jnp.concatenate([piece if c == col else zero for c in range(Q // LANES)], axis=2))
        return jnp.concatenate(blocks, axis=1)

    def lane_sums(x3):
        n, rws, _ = x3.shape
        x2 = x3.reshape(n * rws, Q)
        return sum(_dot(t, ones_blk) for t in _split_bf16(x2, 2)).reshape(n, rws, Q)

    w_all = cat_batch(w_ref)
    cw_all = _dot_exact_lhs(tril, w_all)
    cwl = cw_all[L - 1:L, :]
    p_in = per_group(jnp.exp(cw_all))
    p_ex = per_group(jnp.exp(cw_all - w_all))
    p_inv = per_group(jnp.exp(-cw_all))
    p_rem = per_group(jnp.exp(cwl - cw_all))
    dec_all = jnp.exp(cwl)
    p_all = jnp.stack([dec_all[:, q * Q:(q + 1) * Q] for q in groups])

    kkr = per_group(cat_batch(kk_ref).astype(F32))
    kkn = kkr * lax.rsqrt(lane_sums(kkr * kkr) + L2_EPS)
    rp = per_group(cat_batch(r_ref).astype(F32))
    kp = per_group(cat_batch(k_ref).astype(F32))
    vp = per_group(cat_batch(v_ref).astype(F32))
    bb = kkn * per_group(cat_batch(a_ref).astype(F32))

    lhs = jnp.concatenate([-kkn * p_ex, rp * p_in], axis=1).astype(BF16)
    m = _bdot_nt(lhs, jnp.concatenate([expand(bb * p_inv), expand(kp * p_inv)], axis=1))
    l_ab = jnp.where(strict_c, m[:, :L, :Q], 0.0)
    l_ak = jnp.where(strict_c, m[:, :L, Q:], 0.0)
    m_rb = jnp.where(incl_c, m[:, L:, :Q], 0.0)
    m_rk = jnp.where(incl_c, m[:, L:, Q:], 0.0)

    s = s_ref[...].reshape(len(groups), Q, Q)
    from_s = _bdot_nt(lhs, s)

    t_inv = eye_c + l_ab
    pw = _bdot(l_ab, expand(l_ab))
    n = 4
    while n < L:
        both = _bdot(jnp.concatenate([pw, t_inv], axis=1), expand(pw))
        t_inv = t_inv + both[:, L:]
        pw = both[:, :L]
        n *= 2
    t_inv = t_inv + _bdot(t_inv, expand(pw))

    ve = expand(vp)
    u = _bdot(t_inv, expand(from_s[:, :L] + _bdot(l_ak, ve)))
    y = from_s[:, L:] + _bdot(jnp.concatenate([m_rb, m_rk], axis=2),
                              jnp.concatenate([expand(u), ve], axis=1))
    cross = _bdot_tn(jnp.concatenate([u, vp], axis=1), jnp.concatenate([bb * p_rem, kp * p_rem], axis=1))
    s_ref[...] = (s * p_all + jnp.where(same_head_sq, cross, 0.0)).reshape(s_ref.shape)

    mean = lane_sums(y) * (1.0 / N)
    cy = y - mean
    var = lane_sums(cy * cy) * (1.0 / N)
    rk3 = per_group(jnp.concatenate([rk_ref[...]] * nb, axis=1))
    bonus = lane_sums(rp * kp * rk3) * vp
    for g in groups:
        i, q = divmod(g, d // Q)
        sl = slice(q * Q, (q + 1) * Q)
        yn = cy[g] * lax.rsqrt(var[g] + GN_EPS) * lnw_ref[:, sl] + lnb_ref[:, sl]
        yo_ref[i, :, sl] = (yn + bonus[g]).astype(yo_ref.dtype)


def _rwkv_scan(r, k, v, kk, a, w, s0_groups, rk_row, lnw_row, lnb_row):
    b, tp, d = r.shape
    q = RWKV_GROUP * RWKV_N
    nb = _scan_batch(b)
    tok = pl.BlockSpec((nb, CHUNK, d), lambda i, c: (i, c, 0))
    st = pl.BlockSpec((nb, d // q, q, q), lambda i, c: (i, 0, 0, 0))
    return pl.pallas_call(
        _rwkv_scan_kernel,
        grid=(b // nb, tp // CHUNK),
        in_specs=[tok] * 6 + [st, _const_spec((1, d)), _const_spec((1, d)), _const_spec((1, d))],
        out_specs=[tok, st],
        out_shape=[jax.ShapeDtypeStruct((b, tp, d), BF16),
                   jax.ShapeDtypeStruct((b, d // q, q, q), F32)],
        compiler_params=pltpu.CompilerParams(dimension_semantics=("arbitrary", "arbitrary"),
                                             vmem_limit_bytes=VMEM_LIMIT),
        name="rwkv_scan",
    )(r, k, v, kk, a, w, s0_groups, rk_row, lnw_row, lnb_row)


def _pad_cols(w, n):
    return jnp.pad(w, ((0, 0), (0, n - w.shape[1])))


def _pad_rows(w, n):
    return jnp.pad(w, ((0, n - w.shape[0]), (0, 0)))


def _groups_from_state(s):
    b, h, n, _ = s.shape
    g = RWKV_GROUP
    eye = jnp.eye(g, dtype=s.dtype)
    return jnp.einsum("bqgij,gh->bqgihj", s.reshape(b, h // g, g, n, n), eye).reshape(b, h // g, g * n, g * n)


def _state_from_groups(sg):
    b, hq, gn, _ = sg.shape
    g = RWKV_GROUP
    n = gn // g
    return jnp.einsum("bqgigj->bqgij", sg.reshape(b, hq, g, n, g, n)).reshape(b, hq * g, n, n)


def _run_trunk(h, conv_bufs, gdn_states, shift_bufs, rwkv_states, wts):
    b, tp, d = h.shape
    rows = b * tp
    depth = wts["norm_mix"].shape[0]
    new_conv, new_gdn, new_shift, new_rwkv = [], [], [], []
    for i in range(depth):
        j = i // 2
        fn_row = wts["norm_final"] if i == depth - 1 else None
        if i % 2 == 0:
            g = wts["gdn"][j]
            cbuf8 = jnp.pad(conv_bufs[j], ((0, 0), (SUBLANES - (CONV_W - 1), 0), (0, 0)))
            q, k, v, z, ab, cst = _gdn_proj(h, wts["norm_mix"][i:i + 1], g["w_in"], g["conv_w"], cbuf8)
            o, s_new = _gdn_scan(q, k, v, ab, gdn_states[j], g["a_log"], g["dt_bias"], g["o_norm"])
            new_conv.append(cst[:, SUBLANES - (CONV_W - 1):])
            new_gdn.append(s_new)
            h2 = _mix_ffn(o.reshape(rows, d), z.reshape(rows, d), h.reshape(rows, d), g["w_out"],
                          wts["norm_ffn"][i:i + 1], wts["ffn_gate"][i], wts["ffn_up"][i], wts["ffn_down"][i],
                          fn_row, silu_gate=True, name="gdn_out_ffn")
        else:
            rw = wts["rwkv"][j]
            vec = jnp.concatenate([wts["norm_mix"][i:i + 1], rw["vec"]], axis=0)
            r, k, v, kk, a, w, gate, last_hn = _rwkv_proj(
                h, shift_bufs[j], vec, rw["mu"], rw["w_r"], rw["w_k"], rw["w_v"],
                rw["w1"], rw["w2"], rw["a1"], rw["a2"], rw["g1"], rw["g2"])
            yo, sp_new = _rwkv_scan(r, k, v, kk, a, w, _groups_from_state(rwkv_states[j]),
                                    rw["r_k"], rw["ln_w"], rw["ln_b"])
            new_shift.append(last_hn)
            new_rwkv.append(_state_from_groups(sp_new))
            h2 = _mix_ffn(yo.reshape(rows, d), gate.reshape(rows, d), h.reshape(rows, d), rw["w_o"],
                          wts["norm_ffn"][i:i + 1], wts["ffn_gate"][i], wts["ffn_up"][i], wts["ffn_down"][i],
                          fn_row, silu_gate=False, name="rwkv_out_ffn")
        h = h2.reshape(b, tp, d)
    return h, jnp.stack(new_conv), jnp.stack(new_gdn), jnp.stack(new_shift), jnp.stack(new_rwkv)


def _prepare_weights(norm_mix, norm_ffn, norm_final, gdn_w_in, gdn_conv_w, gdn_a_log, gdn_dt_bias, gdn_o_norm,
                     gdn_w_out, rwkv_mu, rwkv_w0, rwkv_w1, rwkv_w2, rwkv_a0, rwkv_a1, rwkv_a2, rwkv_g1, rwkv_g2,
                     rwkv_k_k, rwkv_k_a, rwkv_r_k, rwkv_w_r, rwkv_w_k, rwkv_w_v, rwkv_w_o, rwkv_ln_w,
                     rwkv_ln_b, ffn_w_gate, ffn_w_up, ffn_w_down):
    d = norm_mix.shape[1]
    gdn = []
    for j in range(gdn_w_in.shape[0]):
        w = gdn_w_in[j]
        qkvz = w[:, :GDN_QKV + GDN_KDIM]
        a_cols = _pad_cols(w[:, GDN_QKV + GDN_KDIM:GDN_QKV + GDN_KDIM + GDN_HEADS], LANES)
        b_cols = _pad_cols(w[:, GDN_QKV + GDN_KDIM + GDN_HEADS:], LANES)
        gdn.append(dict(
            w_in=jnp.concatenate([qkvz, a_cols, b_cols], axis=1).astype(BF16),
            conv_w=gdn_conv_w[j],
            a_log=_pad_cols(gdn_a_log[j][None], LANES),
            dt_bias=_pad_cols(gdn_dt_bias[j][None], LANES),
            o_norm=gdn_o_norm[j][None],
            w_out=gdn_w_out[j].astype(BF16),
        ))
    rwkv = []
    for j in range(rwkv_mu.shape[0]):
        lw = -(-rwkv_w1.shape[2] // LANES) * LANES
        la = -(-rwkv_a1.shape[2] // LANES) * LANES
        lg = -(-rwkv_g1.shape[2] // LANES) * LANES
        rwkv.append(dict(
            vec=jnp.stack([rwkv_w0[j], rwkv_a0[j], rwkv_k_k[j], rwkv_k_a[j],
                           jnp.zeros_like(rwkv_w0[j]), jnp.zeros_like(rwkv_w0[j]), jnp.zeros_like(rwkv_w0[j])]),
            mu=_pad_rows(rwkv_mu[j], SUBLANES),
            w_r=rwkv_w_r[j].astype(BF16), w_k=rwkv_w_k[j].astype(BF16), w_v=rwkv_w_v[j].astype(BF16),
            w_o=rwkv_w_o[j].astype(BF16),
            w1=_pad_cols(rwkv_w1[j], lw).astype(BF16), w2=_pad_rows(rwkv_w2[j], lw).astype(BF16),
            a1=_pad_cols(rwkv_a1[j], la).astype(BF16), a2=_pad_rows(rwkv_a2[j], la).astype(BF16),
            g1=_pad_cols(rwkv_g1[j], lg).astype(BF16), g2=_pad_rows(rwkv_g2[j], lg).astype(BF16),
            r_k=rwkv_r_k[j].reshape(1, d), ln_w=rwkv_ln_w[j][None], ln_b=rwkv_ln_b[j][None],
        ))
    return dict(norm_mix=norm_mix, norm_ffn=norm_ffn, norm_final=norm_final[None], gdn=gdn, rwkv=rwkv,
                ffn_gate=ffn_w_gate.astype(BF16), ffn_up=ffn_w_up.astype(BF16), ffn_down=ffn_w_down.astype(BF16))


def kernel(x_prompt, x_sample, cache_gdn_conv, state_gdn, cache_rwkv_shift, state_rwkv, meta_tokens, norm_mix, norm_ffn, norm_final, gdn_w_in, gdn_conv_w, gdn_a_log, gdn_dt_bias, gdn_o_norm, gdn_w_out, rwkv_mu, rwkv_w0, rwkv_w1, rwkv_w2, rwkv_a0, rwkv_a1, rwkv_a2, rwkv_g1, rwkv_g2, rwkv_k_k, rwkv_k_a, rwkv_r_k, rwkv_w_r, rwkv_w_k, rwkv_w_v, rwkv_w_o, rwkv_ln_w, rwkv_ln_b, ffn_w_gate, ffn_w_up, ffn_w_down):
    wts = _prepare_weights(norm_mix, norm_ffn, norm_final, gdn_w_in, gdn_conv_w, gdn_a_log, gdn_dt_bias,
                           gdn_o_norm, gdn_w_out, rwkv_mu, rwkv_w0, rwkv_w1, rwkv_w2, rwkv_a0, rwkv_a1, rwkv_a2,
                           rwkv_g1, rwkv_g2, rwkv_k_k, rwkv_k_a, rwkv_r_k, rwkv_w_r, rwkv_w_k, rwkv_w_v,
                           rwkv_w_o, rwkv_ln_w, rwkv_ln_b, ffn_w_gate, ffn_w_up, ffn_w_down)
    n_gdn = gdn_w_in.shape[0]
    n_rwkv = rwkv_mu.shape[0]
    b, seq, d = x_prompt.shape
    dt = x_prompt.dtype

    lead_pad = (-N_META) % CHUNK
    lead = jnp.concatenate([jnp.zeros((b, lead_pad, d), dt),
                            jnp.broadcast_to(meta_tokens.astype(dt)[None], (b, N_META, d))], axis=1)
    zeros_like_states = (jnp.zeros((n_gdn, b, CONV_W - 1, GDN_QKV), dt),
                         jnp.zeros((n_gdn, b, GDN_HEADS, GDN_DK, GDN_DV), dt),
                         jnp.zeros((n_rwkv, b, 1, d), dt),
                         jnp.zeros((n_rwkv, b, d // RWKV_N, RWKV_N, RWKV_N), dt))
    sample_states = (cache_gdn_conv, state_gdn, cache_rwkv_shift, state_rwkv)

    bs = x_sample.shape[0]
    if x_sample.shape[1:] == lead.shape[1:]:
        both = _run_trunk(jnp.concatenate([x_sample, lead], axis=0),
                          *[jnp.concatenate([s, z], axis=1) for s, z in zip(sample_states, zeros_like_states)],
                          wts)
        y_sample, s_conv, s_gdn, s_shift, s_rwkv = [both[0][:bs]] + [t[:, :bs] for t in both[1:]]
        lead_states = [t[:, bs:] for t in both[1:]]
    else:
        y_sample, s_conv, s_gdn, s_shift, s_rwkv = _run_trunk(x_sample, *sample_states, wts)
        lead_states = _run_trunk(lead, *zeros_like_states, wts)[1:]

    y_prompt, p_conv, p_gdn, p_shift, p_rwkv = _run_trunk(x_prompt, *lead_states, wts)

    return (y_prompt, y_sample, p_conv, p_gdn, p_shift, p_rwkv, s_conv, s_gdn, s_shift, s_rwkv)
```

```python
import functools

import jax
import jax.numpy as jnp
from jax import lax
from jax.experimental import pallas as pl
from jax.experimental.pallas import tpu as pltpu

F32 = jnp.float32
BF16 = jnp.bfloat16

CHUNK = 64
N_META = 16
EPS = 1e-6
L2_EPS = 1e-6
GN_EPS = 64e-5

GDN_HEADS = 8
GDN_DK = 128
GDN_DV = 128
GDN_KDIM = GDN_HEADS * GDN_DK
GDN_QKV = 3 * GDN_KDIM
CONV_W = 4
RWKV_N = 64
RWKV_GROUP = 4
DECAY_SCALE = 0.6065306597126334

LANES = 128
SUBLANES = 8
VMEM_LIMIT = 56 * 1024 * 1024


def _dot(a, b):
    return jnp.dot(a.astype(BF16), b.astype(BF16), preferred_element_type=F32)


def _bdot(a, b):
    return lax.dot_general(a.astype(BF16), b.astype(BF16), (((2,), (1,)), ((0,), (0,))),
                           preferred_element_type=F32)


def _bdot_nt(a, b):
    return lax.dot_general(a.astype(BF16), b.astype(BF16), (((2,), (2,)), ((0,), (0,))),
                           preferred_element_type=F32)


def _bdot_tn(a, b):
    return lax.dot_general(a.astype(BF16), b.astype(BF16), (((1,), (1,)), ((0,), (0,))),
                           preferred_element_type=F32)


def _split_bf16(x, parts):
    out = []
    for _ in range(parts - 1):
        hi = x.astype(BF16)
        out.append(hi)
        x = x - hi.astype(F32)
    out.append(x.astype(BF16))
    return out


def _dot_exact_lhs(m01, x, parts=3):
    return sum(_dot(m01, t) for t in _split_bf16(x, parts))


def _rms(x, g, eps=EPS):
    return x * lax.rsqrt(jnp.mean(x * x, axis=-1, keepdims=True) + eps) * g


def _neumann(x, eye, size):
    rows = x.shape[1]
    t = eye + x
    p = _bdot(x, x)
    n = 4
    while n < size:
        both = _bdot(jnp.concatenate([p, t], axis=1), p)
        t = t + both[:, rows:]
        p = both[:, :rows]
        n *= 2
    return t + _bdot(t, p)


def _pick_tile(n, candidates):
    for c in candidates:
        if n % c == 0:
            return c
    raise ValueError(f"no tile for {n}")


def _params(grid_rank):
    return pltpu.CompilerParams(dimension_semantics=("arbitrary",) * grid_rank, vmem_limit_bytes=VMEM_LIMIT)


def _const_spec(shape):
    nd = len(shape)
    return pl.BlockSpec(shape, lambda *_: (0,) * nd, pipeline_mode=pl.Buffered(1))


def _gdn_proj_kernel(x_ref, g_ref, w_ref, cw_ref, cbuf_ref, q_o, k_o, v_o, z_o, ab_o, cst_o, xs_ref):
    t = pl.program_id(1)
    tt = x_ref.shape[0]
    hist = SUBLANES

    @pl.when(t == 0)
    def _():
        xs_ref[...] = cbuf_ref[...]

    xn = _rms(x_ref[...], g_ref[...]).astype(BF16)

    for out, c0 in ((q_o, 0), (k_o, GDN_KDIM), (v_o, 2 * GDN_KDIM)):
        cols = slice(c0, c0 + GDN_KDIM)
        proj = _dot(xn, w_ref[:, cols])
        win = jnp.concatenate([xs_ref[:, cols], proj], axis=0)
        xs_ref[:, cols] = proj[tt - hist:tt]
        cst_o[:, cols] = proj[tt - hist:tt]
        win1 = pltpu.roll(win, 1, 0)
        near = cw_ref[3:4, cols] * win + cw_ref[2:3, cols] * win1
        far = cw_ref[1:2, cols] * win + cw_ref[0:1, cols] * win1
        out[...] = (near + pltpu.roll(far, 2, 0))[hist:].astype(out.dtype)
    z_o[...] = _dot(xn, w_ref[:, GDN_QKV:GDN_QKV + GDN_KDIM]).astype(z_o.dtype)
    ab_o[...] = _dot(xn, w_ref[:, GDN_QKV + GDN_KDIM:])


def _gdn_proj(h3, g_row, w_bf16, conv_w, cbuf8):
    b, tp, d = h3.shape
    n = w_bf16.shape[1]
    tt = _pick_tile(tp, (256, 192, 128, 64))
    tok = lambda width: pl.BlockSpec((None, tt, width), lambda i, t: (i, t, 0))
    hist_spec = pl.BlockSpec((None, SUBLANES, GDN_QKV), lambda i, t: (i, 0, 0))
    return pl.pallas_call(
        _gdn_proj_kernel,
        grid=(b, tp // tt),
        in_specs=[tok(d), _const_spec((1, d)), _const_spec((d, n)), _const_spec((CONV_W, GDN_QKV)), hist_spec],
        out_specs=[tok(GDN_KDIM), tok(GDN_KDIM), tok(GDN_KDIM), tok(GDN_KDIM), tok(2 * LANES), hist_spec],
        out_shape=[jax.ShapeDtypeStruct((b, tp, GDN_KDIM), BF16)] * 4
        + [jax.ShapeDtypeStruct((b, tp, 2 * LANES), F32), jax.ShapeDtypeStruct((b, SUBLANES, GDN_QKV), F32)],
        scratch_shapes=[pltpu.VMEM((SUBLANES, GDN_QKV), F32)],
        compiler_params=_params(2),
        name="gdn_in_proj",
    )(h3, g_row, w_bf16, conv_w, cbuf8)


def _gdn_scan_kernel(q_ref, k_ref, v_ref, ab_ref, s0_ref, alog_ref, dtb_ref, onorm_ref, o_ref, s_ref):
    c = pl.program_id(1)
    L = CHUNK

    @pl.when(c == 0)
    def _():
        s_ref[...] = s0_ref[...]

    row = lax.broadcasted_iota(jnp.int32, (L, L), 0)
    col = lax.broadcasted_iota(jnp.int32, (L, L), 1)
    causal = row >= col
    strict = row > col
    tril = causal.astype(BF16)
    eye = (row == col).astype(F32)

    nb = q_ref.shape[0]
    a_in = jnp.concatenate([ab_ref[i, :, :LANES] + dtb_ref[...] for i in range(nb)], axis=1)
    b_in = jnp.concatenate([ab_ref[i, :, LANES:] for i in range(nb)], axis=1)
    neg_a = jnp.concatenate([-jnp.exp(alog_ref[...])] * nb, axis=1)
    g = neg_a * jax.nn.softplus(a_in)
    beta = jax.nn.sigmoid(b_in)
    G = _dot_exact_lhs(tril, g)
    Gt = G.T

    chains = [(i, h) for i in range(nb) for h in range(GDN_HEADS)]

    def per_head(ref):
        return jnp.stack([ref[i, :, h * GDN_DK:(h + 1) * GDN_DK] for i, h in chains])

    def head_cols(x2d):
        return jnp.stack([x2d[:, i * LANES + h:i * LANES + h + 1] for i, h in chains])

    q = jax.nn.silu(per_head(q_ref).astype(F32))
    kf = jax.nn.silu(per_head(k_ref).astype(F32))
    v = jax.nn.silu(per_head(v_ref).astype(F32))
    q = q * (lax.rsqrt(jnp.sum(q * q, axis=-1, keepdims=True) + L2_EPS) * (GDN_DK ** -0.5))
    kf = kf * lax.rsqrt(jnp.sum(kf * kf, axis=-1, keepdims=True) + L2_EPS)
    k = kf.astype(BF16)
    bcol = head_cols(beta)
    g_col = head_cols(G)
    g_row = jnp.stack([Gt[i * LANES + h:i * LANES + h + 1, :] for i, h in chains])
    g_last = g_col[:, L - 1:L, :]
    eg_col = jnp.exp(g_col)
    egr_col = jnp.exp(g_last - g_col)
    egl = jnp.exp(g_last)

    dlt = g_col - g_row
    dec = jnp.where(causal, jnp.exp(jnp.where(causal, dlt, 0.0)), 0.0)
    kq_k = _bdot_nt(jnp.concatenate([k, q.astype(BF16)], axis=1), k)
    a_mat = jnp.where(strict, bcol * kq_k[:, :L] * dec, 0.0)
    qk = kq_k[:, L:] * dec
    t_inv = _neumann(-a_mat, eye, L)
    rhs = jnp.concatenate([v * bcol, kf * (bcol * eg_col)], axis=-1)
    sol = _bdot(t_inv, rhs)
    u = sol[..., :GDN_DV]
    w = sol[..., GDN_DV:]
    s = s_ref[...].reshape(nb * GDN_HEADS, GDN_DK, GDN_DV)
    ws_qs = _bdot(jnp.concatenate([w, q * eg_col], axis=1), s)
    v_new = u - ws_qs[:, :L]
    o = ws_qs[:, L:] + _bdot(qk, v_new)
    s_ref[...] = (s * egl + _bdot_tn(kf * egr_col, v_new)).reshape(s_ref.shape)
    o = o * lax.rsqrt(jnp.mean(o * o, axis=-1, keepdims=True) + EPS) * onorm_ref[...]
    for n, (i, h) in enumerate(chains):
        o_ref[i, :, h * GDN_DV:(h + 1) * GDN_DV] = o[n].astype(o_ref.dtype)


def _scan_batch(b):
    return next(n for n in (4, 2, 1) if b % n == 0)


def _gdn_scan(q, k, v, ab, s0, alog_row, dtb_row, onorm_row):
    b, tp, _ = q.shape
    nb = _scan_batch(b)
    tok = pl.BlockSpec((nb, CHUNK, GDN_KDIM), lambda i, c: (i, c, 0))
    st = pl.BlockSpec((nb, GDN_HEADS, GDN_DK, GDN_DV), lambda i, c: (i, 0, 0, 0))
    return pl.pallas_call(
        _gdn_scan_kernel,
        grid=(b // nb, tp // CHUNK),
        in_specs=[tok, tok, tok, pl.BlockSpec((nb, CHUNK, 2 * LANES), lambda i, c: (i, c, 0)), st,
                  _const_spec((1, LANES)), _const_spec((1, LANES)), _const_spec((1, GDN_DV))],
        out_specs=[tok, st],
        out_shape=[jax.ShapeDtypeStruct((b, tp, GDN_KDIM), BF16),
                   jax.ShapeDtypeStruct((b, GDN_HEADS, GDN_DK, GDN_DV), F32)],
        compiler_params=_params(2),
        name="gdn_scan",
    )(q, k, v, ab, s0, alog_row, dtb_row, onorm_row)


def _mix_ffn_kernel(*refs, silu_gate, final_norm):
    if final_norm:
        y_ref, gm_ref, h_ref, wo_ref, nf_ref, wg_ref, wu_ref, wd_ref, fn_ref, o_ref = refs
    else:
        y_ref, gm_ref, h_ref, wo_ref, nf_ref, wg_ref, wu_ref, wd_ref, o_ref = refs
    gm = gm_ref[...].astype(F32)
    if silu_gate:
        gm = jax.nn.silu(gm)
    h1 = h_ref[...] + _dot(y_ref[...].astype(F32) * gm, wo_ref[...])
    n = _rms(h1, nf_ref[...])
    act = jax.nn.silu(_dot(n, wg_ref[...])) * _dot(n, wu_ref[...])
    h2 = h1 + _dot(act, wd_ref[...])
    if final_norm:
        h2 = _rms(h2, fn_ref[...])
    o_ref[...] = h2


def _mix_ffn(y2d, gm2d, h2d, wo, nf_row, wg, wu, wd, fn_row, *, silu_gate, name):
    rows, d = h2d.shape
    dff = wg.shape[1]
    tm = _pick_tile(rows, (512, 256, 128, 64))
    final_norm = fn_row is not None
    tok = pl.BlockSpec((tm, d), lambda i: (i, 0))
    in_specs = [tok, tok, tok, _const_spec((d, d)), _const_spec((1, d)),
                _const_spec((d, dff)), _const_spec((d, dff)), _const_spec((dff, d))]
    args = [y2d, gm2d, h2d, wo, nf_row, wg, wu, wd]
    if final_norm:
        in_specs.append(_const_spec((1, d)))
        args.append(fn_row)
    return pl.pallas_call(
        functools.partial(_mix_ffn_kernel, silu_gate=silu_gate, final_norm=final_norm),
        grid=(rows // tm,),
        in_specs=in_specs,
        out_specs=tok,
        out_shape=jax.ShapeDtypeStruct((rows, d), F32),
        compiler_params=_params(1),
        name=name,
    )(*args)


def _rwkv_proj_kernel(h_ref, sb_ref, vec_ref, mu_ref, wr_ref, wk_ref, wv_ref,
                      w1_ref, w2_ref, a1_ref, a2_ref, g1_ref, g2_ref,
                      r_o, k_o, v_o, kk_o, a_o, w_o, g_o, last_o, hs_ref):
    t = pl.program_id(1)
    tt = h_ref.shape[0]
    hist = SUBLANES
    hn = _rms(h_ref[...], vec_ref[0:1, :])

    @pl.when(t == 0)
    def _():
        hs_ref[...] = jnp.broadcast_to(sb_ref[...], hs_ref.shape)

    prev = pltpu.roll(jnp.concatenate([hs_ref[...], hn], axis=0), 1, 0)[hist:]
    hs_ref[hist - 1:hist, :] = hn[tt - 1:tt, :]
    last_o[...] = hn[tt - 1:tt, :]
    xx = prev - hn

    def mix(i):
        return (hn + xx * mu_ref[i:i + 1, :]).astype(BF16)

    r = _dot(mix(0), wr_ref[...])
    k = _dot(mix(2), wk_ref[...])
    v = _dot(mix(3), wv_ref[...])
    wl = vec_ref[1:2, :] + _dot(jnp.tanh(_dot(mix(1), w1_ref[...])), w2_ref[...])
    a = jax.nn.sigmoid(vec_ref[2:3, :] + _dot(_dot(mix(4), a1_ref[...]), a2_ref[...]))
    gate = _dot(jax.nn.sigmoid(_dot(mix(5), g1_ref[...])), g2_ref[...])
    r_o[...] = r.astype(r_o.dtype)
    k_o[...] = (k * (1.0 + (a - 1.0) * vec_ref[4:5, :])).astype(k_o.dtype)
    v_o[...] = v.astype(v_o.dtype)
    kk_o[...] = (k * vec_ref[3:4, :]).astype(kk_o.dtype)
    a_o[...] = a.astype(a_o.dtype)
    w_o[...] = -DECAY_SCALE * jax.nn.sigmoid(wl)
    g_o[...] = gate.astype(g_o.dtype)


def _rwkv_proj(h3, sbuf, vec, mu8, wr, wk, wv, w1, w2, a1, a2, g1, g2):
    b, tp, d = h3.shape
    tt = _pick_tile(tp, (512, 256, 192, 128, 64))
    tok = pl.BlockSpec((None, tt, d), lambda i, t: (i, t, 0))
    row = pl.BlockSpec((None, 1, d), lambda i, t: (i, 0, 0))
    weights = [wr, wk, wv, w1, w2, a1, a2, g1, g2]
    tok_shape = lambda dt: jax.ShapeDtypeStruct((b, tp, d), dt)
    return pl.pallas_call(
        _rwkv_proj_kernel,
        grid=(b, tp // tt),
        in_specs=[tok, row, _const_spec(vec.shape), _const_spec(mu8.shape)] + [_const_spec(w.shape) for w in weights],
        out_specs=[tok] * 7 + [row],
        out_shape=[tok_shape(BF16)] * 5 + [tok_shape(F32), tok_shape(BF16), jax.ShapeDtypeStruct((b, 1, d), F32)],
        scratch_shapes=[pltpu.VMEM((SUBLANES, d), F32)],
        compiler_params=_params(2),
        name="rwkv_proj",
    )(h3, sbuf, vec, mu8, *weights)


def _rwkv_scan_kernel(r_ref, k_ref, v_ref, kk_ref, a_ref, w_ref, s0_ref, rk_ref, lnw_ref, lnb_ref,
                      yo_ref, s_ref):
    c = pl.program_id(1)
    L = CHUNK
    N = RWKV_N
    G = RWKV_GROUP
    Q = G * N
    nb, _, d = r_ref.shape
    groups = range(nb * d // Q)

    def cat_batch(ref):
        return jnp.concatenate([ref[i] for i in range(nb)], axis=1)

    @pl.when(c == 0)
    def _():
        s_ref[...] = s0_ref[...]

    row = lax.broadcasted_iota(jnp.int32, (L, L), 0)
    col = lax.broadcasted_iota(jnp.int32, (L, L), 1)
    tril = (row >= col).astype(BF16)
    t_idx = lax.broadcasted_iota(jnp.int32, (L, Q), 0)
    lane = lax.broadcasted_iota(jnp.int32, (L, Q), 1)
    s_idx = lane % N
    strict_c = t_idx > s_idx
    incl_c = t_idx >= s_idx
    eye_c = (t_idx == s_idx).astype(F32)
    ones_blk = (lax.broadcasted_iota(jnp.int32, (Q, Q), 0) // N
                == lax.broadcasted_iota(jnp.int32, (Q, Q), 1) // N).astype(BF16)
    head_lanes = [lax.broadcasted_iota(jnp.int32, (N, Q), 1) // N == g for g in range(G)]

    def per_group(x2d):
        return jnp.stack([x2d[:, q * Q:(q + 1) * Q] for q in groups])

    def expand(x3):
        xb = x3.astype(BF16)
        zero = jnp.zeros(xb.shape[:2] + (LANES,), BF16)
        slot_of_lane = lax.broadcasted_iota(jnp.int32, xb.shape[1:2] + (LANES,), 1) // N
        blocks = []
        for g in range(G):
            col, slot = divmod(g * N, LANES)
            piece = jnp.where(slot_of_lane == slot // N, xb[:, :, col * LANES:(col + 1) * LANES], zero)
            blocks.append(jnp.concatenate([piece if c == col else zero for c in range(Q // LANES)], axis=2))
        return jnp.concatenate(blocks, axis=1)

    def diag_blocks(x3):
        return sum(jnp.where(head_lanes[g], x3[:, g * N:(g + 1) * N, :], 0.0) for g in range(G))

    def lane_sums(x3):
        n, rws, _ = x3.shape
        return _dot(x3.reshape(n * rws, Q), ones_blk).reshape(n, rws, Q)

    w_all = cat_batch(w_ref)
    cw_all = _dot_exact_lhs(tril, w_all)
    cwl = cw_all[L - 1:L, :]
    p_in = per_group(jnp.exp(cw_all))
    p_ex = per_group(jnp.exp(cw_all - w_all))
    p_inv = per_group(jnp.exp(-cw_all))
    p_rem = per_group(jnp.exp(cwl - cw_all))
    dec_all = jnp.exp(cwl)
    p_all = jnp.stack([dec_all[:, q * Q:(q + 1) * Q] for q in groups])

    kkr = per_group(cat_batch(kk_ref).astype(F32))
    kkn = kkr * lax.rsqrt(lane_sums(kkr * kkr) + L2_EPS)
    rp = per_group(cat_batch(r_ref).astype(F32))
    kp = per_group(cat_batch(k_ref).astype(F32))
    vp = per_group(cat_batch(v_ref).astype(F32))
    bb = kkn * per_group(cat_batch(a_ref).astype(F32))

    lhs = jnp.concatenate([-kkn * p_ex, rp * p_in], axis=1).astype(BF16)
    m = _bdot_nt(lhs, jnp.concatenate([expand(bb * p_inv), expand(kp * p_inv)], axis=1))
    l_ab = jnp.where(strict_c, m[:, :L, :Q], 0.0)
    l_ak = jnp.where(strict_c, m[:, :L, Q:], 0.0)
    m_rb = jnp.where(incl_c, m[:, L:, :Q], 0.0)
    m_rk = jnp.where(incl_c, m[:, L:, Q:], 0.0)

    s = s_ref[...].reshape(len(groups), N, Q)
    from_s = _bdot_nt(lhs, expand(s))

    t_inv = eye_c + l_ab
    pw = _bdot(l_ab, expand(l_ab))
    n = 4
    while n < L:
        both = _bdot(jnp.concatenate([pw, t_inv], axis=1), expand(pw))
        t_inv = t_inv + both[:, L:]
        pw = both[:, :L]
        n *= 2
    t_inv = t_inv + _bdot(t_inv, expand(pw))

    ve = expand(vp)
    u = _bdot(t_inv, expand(from_s[:, :L] + _bdot(l_ak, ve)))
    y = from_s[:, L:] + _bdot(jnp.concatenate([m_rb, m_rk], axis=2),
                              jnp.concatenate([expand(u), ve], axis=1))
    cross = _bdot_tn(jnp.concatenate([u, vp], axis=1), jnp.concatenate([bb * p_rem, kp * p_rem], axis=1))
    s_ref[...] = (s * p_all + diag_blocks(cross)).reshape(s_ref.shape)

    mean = lane_sums(y) * (1.0 / N)
    cy = y - mean
    var = lane_sums(cy * cy) * (1.0 / N)
    rk3 = per_group(jnp.concatenate([rk_ref[...]] * nb, axis=1))
    bonus = lane_sums(rp * kp * rk3) * vp
    for g in groups:
        i, q = divmod(g, d // Q)
        sl = slice(q * Q, (q + 1) * Q)
        yn = cy[g] * lax.rsqrt(var[g] + GN_EPS) * lnw_ref[:, sl] + lnb_ref[:, sl]
        yo_ref[i, :, sl] = (yn + bonus[g]).astype(yo_ref.dtype)


def _rwkv_scan(r, k, v, kk, a, w, s0_groups, rk_row, lnw_row, lnb_row):
    b, tp, d = r.shape
    q = RWKV_GROUP * RWKV_N
    nb = _scan_batch(b)
    tok = pl.BlockSpec((nb, CHUNK, d), lambda i, c: (i, c, 0))
    st = pl.BlockSpec((nb, d // q, RWKV_N, q), lambda i, c: (i, 0, 0, 0))
    return pl.pallas_call(
        _rwkv_scan_kernel,
        grid=(b // nb, tp // CHUNK),
        in_specs=[tok] * 6 + [st, _const_spec((1, d)), _const_spec((1, d)), _const_spec((1, d))],
        out_specs=[tok, st],
        out_shape=[jax.ShapeDtypeStruct((b, tp, d), BF16),
                   jax.ShapeDtypeStruct((b, d // q, RWKV_N, q), F32)],
        compiler_params=_params(2),
        name="rwkv_scan",
    )(r, k, v, kk, a, w, s0_groups, rk_row, lnw_row, lnb_row)


def _pad_cols(w, n):
    return jnp.pad(w, ((0, 0), (0, n - w.shape[1])))


def _pad_rows(w, n):
    return jnp.pad(w, ((0, n - w.shape[0]), (0, 0)))


def _groups_from_state(s):
    b, h, n, _ = s.shape
    g = RWKV_GROUP
    return jnp.swapaxes(s.reshape(b, h // g, g, n, n), 2, 3).reshape(b, h // g, n, g * n)


def _state_from_groups(sg):
    b, hq, n, gn = sg.shape
    g = RWKV_GROUP
    return jnp.swapaxes(sg.reshape(b, hq, n, g, n), 2, 3).reshape(b, hq * g, n, n)


def _run_trunk(h, conv_bufs, gdn_states, shift_bufs, rwkv_states, wts):
    b, tp, d = h.shape
    rows = b * tp
    depth = wts["norm_mix"].shape[0]
    new_conv, new_gdn, new_shift, new_rwkv = [], [], [], []
    for i in range(depth):
        j = i // 2
        fn_row = wts["norm_final"] if i == depth - 1 else None
        if i % 2 == 0:
            g = wts["gdn"][j]
            cbuf8 = jnp.pad(conv_bufs[j], ((0, 0), (SUBLANES - (CONV_W - 1), 0), (0, 0)))
            q, k, v, z, ab, cst = _gdn_proj(h, wts["norm_mix"][i:i + 1], g["w_in"], g["conv_w"], cbuf8)
            o, s_new = _gdn_scan(q, k, v, ab, gdn_states[j], g["a_log"], g["dt_bias"], g["o_norm"])
            new_conv.append(cst[:, SUBLANES - (CONV_W - 1):])
            new_gdn.append(s_new)
            h2 = _mix_ffn(o.reshape(rows, d), z.reshape(rows, d), h.reshape(rows, d), g["w_out"],
                          wts["norm_ffn"][i:i + 1], wts["ffn_gate"][i], wts["ffn_up"][i], wts["ffn_down"][i],
                          fn_row, silu_gate=True, name="gdn_out_ffn")
        else:
            rw = wts["rwkv"][j]
            vec = jnp.concatenate([wts["norm_mix"][i:i + 1], rw["vec"]], axis=0)
            r, k, v, kk, a, w, gate, last_hn = _rwkv_proj(
                h, shift_bufs[j], vec, rw["mu"], rw["w_r"], rw["w_k"], rw["w_v"],
                rw["w1"], rw["w2"], rw["a1"], rw["a2"], rw["g1"], rw["g2"])
            yo, sp_new = _rwkv_scan(r, k, v, kk, a, w, _groups_from_state(rwkv_states[j]),
                                    rw["r_k"], rw["ln_w"], rw["ln_b"])
            new_shift.append(last_hn)
            new_rwkv.append(_state_from_groups(sp_new))
            h2 = _mix_ffn(yo.reshape(rows, d), gate.reshape(rows, d), h.reshape(rows, d), rw["w_o"],
                          wts["norm_ffn"][i:i + 1], wts["ffn_gate"][i], wts["ffn_up"][i], wts["ffn_down"][i],
                          fn_row, silu_gate=False, name="rwkv_out_ffn")
        h = h2.reshape(b, tp, d)
    return h, jnp.stack(new_conv), jnp.stack(new_gdn), jnp.stack(new_shift), jnp.stack(new_rwkv)


def _prepare_weights(norm_mix, norm_ffn, norm_final, gdn_w_in, gdn_conv_w, gdn_a_log, gdn_dt_bias, gdn_o_norm,
                     gdn_w_out, rwkv_mu, rwkv_w0, rwkv_w1, rwkv_w2, rwkv_a0, rwkv_a1, rwkv_a2, rwkv_g1, rwkv_g2,
                     rwkv_k_k, rwkv_k_a, rwkv_r_k, rwkv_w_r, rwkv_w_k, rwkv_w_v, rwkv_w_o, rwkv_ln_w,
                     rwkv_ln_b, ffn_w_gate, ffn_w_up, ffn_w_down):
    d = norm_mix.shape[1]
    gdn = []
    for j in range(gdn_w_in.shape[0]):
        w = gdn_w_in[j]
        qkvz = w[:, :GDN_QKV + GDN_KDIM]
        a_cols = _pad_cols(w[:, GDN_QKV + GDN_KDIM:GDN_QKV + GDN_KDIM + GDN_HEADS], LANES)
        b_cols = _pad_cols(w[:, GDN_QKV + GDN_KDIM + GDN_HEADS:], LANES)
        gdn.append(dict(
            w_in=jnp.concatenate([qkvz, a_cols, b_cols], axis=1).astype(BF16),
            conv_w=gdn_conv_w[j],
            a_log=_pad_cols(gdn_a_log[j][None], LANES),
            dt_bias=_pad_cols(gdn_dt_bias[j][None], LANES),
            o_norm=gdn_o_norm[j][None],
            w_out=gdn_w_out[j].astype(BF16),
        ))
    rwkv = []
    for j in range(rwkv_mu.shape[0]):
        lw = -(-rwkv_w1.shape[2] // LANES) * LANES
        la = -(-rwkv_a1.shape[2] // LANES) * LANES
        lg = -(-rwkv_g1.shape[2] // LANES) * LANES
        rwkv.append(dict(
            vec=jnp.stack([rwkv_w0[j], rwkv_a0[j], rwkv_k_k[j], rwkv_k_a[j],
                           jnp.zeros_like(rwkv_w0[j]), jnp.zeros_like(rwkv_w0[j]), jnp.zeros_like(rwkv_w0[j])]),
            mu=_pad_rows(rwkv_mu[j], SUBLANES),
            w_r=rwkv_w_r[j].astype(BF16), w_k=rwkv_w_k[j].astype(BF16), w_v=rwkv_w_v[j].astype(BF16),
            w_o=rwkv_w_o[j].astype(BF16),
            w1=_pad_cols(rwkv_w1[j], lw).astype(BF16), w2=_pad_rows(rwkv_w2[j], lw).astype(BF16),
            a1=_pad_cols(rwkv_a1[j], la).astype(BF16), a2=_pad_rows(rwkv_a2[j], la).astype(BF16),
            g1=_pad_cols(rwkv_g1[j], lg).astype(BF16), g2=_pad_rows(rwkv_g2[j], lg).astype(BF16),
            r_k=rwkv_r_k[j].reshape(1, d), ln_w=rwkv_ln_w[j][None], ln_b=rwkv_ln_b[j][None],
        ))
    return dict(norm_mix=norm_mix, norm_ffn=norm_ffn, norm_final=norm_final[None], gdn=gdn, rwkv=rwkv,
                ffn_gate=ffn_w_gate.astype(BF16), ffn_up=ffn_w_up.astype(BF16), ffn_down=ffn_w_down.astype(BF16))


def kernel(x_prompt, x_sample, cache_gdn_conv, state_gdn, cache_rwkv_shift, state_rwkv, meta_tokens, norm_mix, norm_ffn, norm_final, gdn_w_in, gdn_conv_w, gdn_a_log, gdn_dt_bias, gdn_o_norm, gdn_w_out, rwkv_mu, rwkv_w0, rwkv_w1, rwkv_w2, rwkv_a0, rwkv_a1, rwkv_a2, rwkv_g1, rwkv_g2, rwkv_k_k, rwkv_k_a, rwkv_r_k, rwkv_w_r, rwkv_w_k, rwkv_w_v, rwkv_w_o, rwkv_ln_w, rwkv_ln_b, ffn_w_gate, ffn_w_up, ffn_w_down):
    wts = _prepare_weights(norm_mix, norm_ffn, norm_final, gdn_w_in, gdn_conv_w, gdn_a_log, gdn_dt_bias,
                           gdn_o_norm, gdn_w_out, rwkv_mu, rwkv_w0, rwkv_w1, rwkv_w2, rwkv_a0, rwkv_a1, rwkv_a2,
                           rwkv_g1, rwkv_g2, rwkv_k_k, rwkv_k_a, rwkv_r_k, rwkv_w_r, rwkv_w_k, rwkv_w_v,
                           rwkv_w_o, rwkv_ln_w, rwkv_ln_b, ffn_w_gate, ffn_w_up, ffn_w_down)
    n_gdn = gdn_w_in.shape[0]
    n_rwkv = rwkv_mu.shape[0]
    b, seq, d = x_prompt.shape
    dt = x_prompt.dtype

    lead_pad = (-N_META) % CHUNK
    lead = jnp.concatenate([jnp.zeros((b, lead_pad, d), dt),
                            jnp.broadcast_to(meta_tokens.astype(dt)[None], (b, N_META, d))], axis=1)
    zeros_like_states = (jnp.zeros((n_gdn, b, CONV_W - 1, GDN_QKV), dt),
                         jnp.zeros((n_gdn, b, GDN_HEADS, GDN_DK, GDN_DV), dt),
                         jnp.zeros((n_rwkv, b, 1, d), dt),
                         jnp.zeros((n_rwkv, b, d // RWKV_N, RWKV_N, RWKV_N), dt))
    sample_states = (cache_gdn_conv, state_gdn, cache_rwkv_shift, state_rwkv)

    bs = x_sample.shape[0]
    if x_sample.shape[1:] == lead.shape[1:]:
        both = _run_trunk(jnp.concatenate([x_sample, lead], axis=0),
                          *[jnp.concatenate([s, z], axis=1) for s, z in zip(sample_states, zeros_like_states)],
                          wts)
        y_sample, s_conv, s_gdn, s_shift, s_rwkv = [both[0][:bs]] + [t[:, :bs] for t in both[1:]]
        lead_states = [t[:, bs:] for t in both[1:]]
    else:
        y_sample, s_conv, s_gdn, s_shift, s_rwkv = _run_trunk(x_sample, *sample_states, wts)
        lead_states = _run_trunk(lead, *zeros_like_states, wts)[1:]

    y_prompt, p_conv, p_gdn, p_shift, p_rwkv = _run_trunk(x_prompt, *lead_states, wts)

    return (y_prompt, y_sample, p_conv, p_gdn, p_shift, p_rwkv, s_conv, s_gdn, s_shift, s_rwkv)
```

```python
import functools

import jax
import jax.numpy as jnp
from jax import lax
from jax.experimental import pallas as pl
from jax.experimental.pallas import tpu as pltpu

F32 = jnp.float32
BF16 = jnp.bfloat16

CHUNK = 64
N_META = 16
EPS = 1e-6
L2_EPS = 1e-6
GN_EPS = 64e-5

GDN_HEADS = 8
GDN_DK = 128
GDN_DV = 128
GDN_KDIM = GDN_HEADS * GDN_DK
GDN_QKV = 3 * GDN_KDIM
CONV_W = 4
RWKV_N = 64
RWKV_GROUP = 4
DECAY_SCALE = 0.6065306597126334

LANES = 128
SUBLANES = 8
VMEM_LIMIT = 56 * 1024 * 1024


def _dot(a, b):
    return jnp.dot(a.astype(BF16), b.astype(BF16), preferred_element_type=F32)


def _bdot(a, b):
    return lax.dot_general(a.astype(BF16), b.astype(BF16), (((2,), (1,)), ((0,), (0,))),
                           preferred_element_type=F32)


def _bdot_nt(a, b):
    return lax.dot_general(a.astype(BF16), b.astype(BF16), (((2,), (2,)), ((0,), (0,))),
                           preferred_element_type=F32)


def _bdot_tn(a, b):
    return lax.dot_general(a.astype(BF16), b.astype(BF16), (((1,), (1,)), ((0,), (0,))),
                           preferred_element_type=F32)


def _split_bf16(x, parts):
    out = []
    for _ in range(parts - 1):
        hi = x.astype(BF16)
        out.append(hi)
        x = x - hi.astype(F32)
    out.append(x.astype(BF16))
    return out


def _dot_exact_lhs(m01, x, parts=3):
    return sum(_dot(m01, t) for t in _split_bf16(x, parts))


def _sigmoid(x):
    return 0.5 + 0.5 * jnp.tanh(0.5 * x)


def _silu(x):
    h = 0.5 * x
    return h + h * jnp.tanh(h)


def _rms(x, g, eps=EPS):
    return x * lax.rsqrt(jnp.mean(x * x, axis=-1, keepdims=True) + eps) * g


def _neumann(x, eye, size):
    rows = x.shape[1]
    t = eye + x
    p = _bdot(x, x)
    n = 4
    while n < size:
        both = _bdot(jnp.concatenate([p, t], axis=1), p)
        t = t + both[:, rows:]
        p = both[:, :rows]
        n *= 2
    return t + _bdot(t, p)


def _pick_tile(n, candidates):
    for c in candidates:
        if n % c == 0:
            return c
    raise ValueError(f"no tile for {n}")


def _params(grid_rank):
    return pltpu.CompilerParams(dimension_semantics=("arbitrary",) * grid_rank, vmem_limit_bytes=VMEM_LIMIT)


def _const_spec(shape):
    nd = len(shape)
    return pl.BlockSpec(shape, lambda *_: (0,) * nd, pipeline_mode=pl.Buffered(1))


def _gdn_proj_kernel(x_ref, g_ref, w_ref, cw_ref, cbuf_ref, q_o, k_o, v_o, z_o, ab_o, cst_o, xs_ref):
    t = pl.program_id(1)
    tt = x_ref.shape[0]
    hist = SUBLANES

    @pl.when(t == 0)
    def _():
        xs_ref[...] = cbuf_ref[...]

    xn = _rms(x_ref[...], g_ref[...]).astype(BF16)

    for out, c0 in ((q_o, 0), (k_o, GDN_KDIM), (v_o, 2 * GDN_KDIM)):
        cols = slice(c0, c0 + GDN_KDIM)
        proj = _dot(xn, w_ref[:, cols])
        win = jnp.concatenate([xs_ref[:, cols], proj], axis=0)
        xs_ref[:, cols] = proj[tt - hist:tt]
        cst_o[:, cols] = proj[tt - hist:tt]
        win1 = pltpu.roll(win, 1, 0)
        near = cw_ref[3:4, cols] * win + cw_ref[2:3, cols] * win1
        far = cw_ref[1:2, cols] * win + cw_ref[0:1, cols] * win1
        out[...] = (near + pltpu.roll(far, 2, 0))[hist:].astype(out.dtype)
    z_o[...] = _dot(xn, w_ref[:, GDN_QKV:GDN_QKV + GDN_KDIM]).astype(z_o.dtype)
    ab_o[...] = _dot(xn, w_ref[:, GDN_QKV + GDN_KDIM:])


def _gdn_proj(h3, g_row, w_bf16, conv_w, cbuf8):
    b, tp, d = h3.shape
    n = w_bf16.shape[1]
    tt = _pick_tile(tp, (256, 192, 128, 64))
    tok = lambda width: pl.BlockSpec((None, tt, width), lambda i, t: (i, t, 0))
    hist_spec = pl.BlockSpec((None, SUBLANES, GDN_QKV), lambda i, t: (i, 0, 0))
    return pl.pallas_call(
        _gdn_proj_kernel,
        grid=(b, tp // tt),
        in_specs=[tok(d), _const_spec((1, d)), _const_spec((d, n)), _const_spec((CONV_W, GDN_QKV)), hist_spec],
        out_specs=[tok(GDN_KDIM), tok(GDN_KDIM), tok(GDN_KDIM), tok(GDN_KDIM), tok(2 * LANES), hist_spec],
        out_shape=[jax.ShapeDtypeStruct((b, tp, GDN_KDIM), BF16)] * 4
        + [jax.ShapeDtypeStruct((b, tp, 2 * LANES), F32), jax.ShapeDtypeStruct((b, SUBLANES, GDN_QKV), F32)],
        scratch_shapes=[pltpu.VMEM((SUBLANES, GDN_QKV), F32)],
        compiler_params=_params(2),
        name="gdn_in_proj",
    )(h3, g_row, w_bf16, conv_w, cbuf8)


def _gdn_scan_kernel(q_ref, k_ref, v_ref, ab_ref, s0_ref, alog_ref, dtb_ref, onorm_ref, o_ref, s_ref):
    c = pl.program_id(1)
    L = CHUNK

    @pl.when(c == 0)
    def _():
        s_ref[...] = s0_ref[...]

    row = lax.broadcasted_iota(jnp.int32, (L, L), 0)
    col = lax.broadcasted_iota(jnp.int32, (L, L), 1)
    causal = row >= col
    strict = row > col
    tril = causal.astype(BF16)
    eye = (row == col).astype(F32)

    nb = q_ref.shape[0]
    a_in = jnp.concatenate([ab_ref[i, :, :LANES] + dtb_ref[...] for i in range(nb)], axis=1)
    b_in = jnp.concatenate([ab_ref[i, :, LANES:] for i in range(nb)], axis=1)
    neg_a = jnp.concatenate([-jnp.exp(alog_ref[...])] * nb, axis=1)
    g = neg_a * jax.nn.softplus(a_in)
    beta = _sigmoid(b_in)
    G = _dot_exact_lhs(tril, g)
    Gt = G.T

    chains = [(i, h) for i in range(nb) for h in range(GDN_HEADS)]

    def per_head(ref):
        return jnp.stack([ref[i, :, h * GDN_DK:(h + 1) * GDN_DK] for i, h in chains])

    def head_cols(x2d):
        return jnp.stack([x2d[:, i * LANES + h:i * LANES + h + 1] for i, h in chains])

    q = _silu(per_head(q_ref).astype(F32))
    kf = _silu(per_head(k_ref).astype(F32))
    v = _silu(per_head(v_ref).astype(F32))
    q = q * (lax.rsqrt(jnp.sum(q * q, axis=-1, keepdims=True) + L2_EPS) * (GDN_DK ** -0.5))
    kf = kf * lax.rsqrt(jnp.sum(kf * kf, axis=-1, keepdims=True) + L2_EPS)
    k = kf.astype(BF16)
    bcol = head_cols(beta)
    g_col = head_cols(G)
    g_row = jnp.stack([Gt[i * LANES + h:i * LANES + h + 1, :] for i, h in chains])
    g_last = g_col[:, L - 1:L, :]
    eg_col = jnp.exp(g_col)
    egr_col = jnp.exp(g_last - g_col)
    egl = jnp.exp(g_last)

    dlt = g_col - g_row
    dec = jnp.where(causal, jnp.exp(jnp.where(causal, dlt, 0.0)), 0.0)
    kq_k = _bdot_nt(jnp.concatenate([k, q.astype(BF16)], axis=1), k)
    a_mat = jnp.where(strict, bcol * kq_k[:, :L] * dec, 0.0)
    qk = kq_k[:, L:] * dec
    t_inv = _neumann(-a_mat, eye, L)
    rhs = jnp.concatenate([v * bcol, kf * (bcol * eg_col)], axis=-1)
    sol = _bdot(t_inv, rhs)
    u = sol[..., :GDN_DV]
    w = sol[..., GDN_DV:]
    s = s_ref[...].reshape(nb * GDN_HEADS, GDN_DK, GDN_DV)
    ws_qs = _bdot(jnp.concatenate([w, q * eg_col], axis=1), s)
    v_new = u - ws_qs[:, :L]
    o = ws_qs[:, L:] + _bdot(qk, v_new)
    s_ref[...] = (s * egl + _bdot_tn(kf * egr_col, v_new)).reshape(s_ref.shape)
    o = o * lax.rsqrt(jnp.mean(o * o, axis=-1, keepdims=True) + EPS) * onorm_ref[...]
    for n, (i, h) in enumerate(chains):
        o_ref[i, :, h * GDN_DV:(h + 1) * GDN_DV] = o[n].astype(o_ref.dtype)


def _scan_batch(b, most=4):
    return next(n for n in (8, 4, 2, 1) if n <= most and b % n == 0)


def _gdn_scan(q, k, v, ab, s0, alog_row, dtb_row, onorm_row):
    b, tp, _ = q.shape
    nb = _scan_batch(b)
    tok = pl.BlockSpec((nb, CHUNK, GDN_KDIM), lambda i, c: (i, c, 0))
    st = pl.BlockSpec((nb, GDN_HEADS, GDN_DK, GDN_DV), lambda i, c: (i, 0, 0, 0))
    return pl.pallas_call(
        _gdn_scan_kernel,
        grid=(b // nb, tp // CHUNK),
        in_specs=[tok, tok, tok, pl.BlockSpec((nb, CHUNK, 2 * LANES), lambda i, c: (i, c, 0)), st,
                  _const_spec((1, LANES)), _const_spec((1, LANES)), _const_spec((1, GDN_DV))],
        out_specs=[tok, st],
        out_shape=[jax.ShapeDtypeStruct((b, tp, GDN_KDIM), BF16),
                   jax.ShapeDtypeStruct((b, GDN_HEADS, GDN_DK, GDN_DV), F32)],
        compiler_params=_params(2),
        name="gdn_scan",
    )(q, k, v, ab, s0, alog_row, dtb_row, onorm_row)


def _mix_ffn_kernel(*refs, silu_gate, final_norm):
    if final_norm:
        y_ref, gm_ref, h_ref, wo_ref, nf_ref, wg_ref, wu_ref, wd_ref, fn_ref, o_ref = refs
    else:
        y_ref, gm_ref, h_ref, wo_ref, nf_ref, wg_ref, wu_ref, wd_ref, o_ref = refs
    gm = gm_ref[...].astype(F32)
    if silu_gate:
        gm = _silu(gm)
    h1 = h_ref[...] + _dot(y_ref[...].astype(F32) * gm, wo_ref[...])
    n = _rms(h1, nf_ref[...])
    act = _silu(_dot(n, wg_ref[...])) * _dot(n, wu_ref[...])
    h2 = h1 + _dot(act, wd_ref[...])
    if final_norm:
        h2 = _rms(h2, fn_ref[...])
    o_ref[...] = h2


def _mix_ffn(y2d, gm2d, h2d, wo, nf_row, wg, wu, wd, fn_row, *, silu_gate, name):
    rows, d = h2d.shape
    dff = wg.shape[1]
    tm = _pick_tile(rows, (512, 256, 128, 64))
    final_norm = fn_row is not None
    tok = pl.BlockSpec((tm, d), lambda i: (i, 0))
    in_specs = [tok, tok, tok, _const_spec((d, d)), _const_spec((1, d)),
                _const_spec((d, dff)), _const_spec((d, dff)), _const_spec((dff, d))]
    args = [y2d, gm2d, h2d, wo, nf_row, wg, wu, wd]
    if final_norm:
        in_specs.append(_const_spec((1, d)))
        args.append(fn_row)
    return pl.pallas_call(
        functools.partial(_mix_ffn_kernel, silu_gate=silu_gate, final_norm=final_norm),
        grid=(rows // tm,),
        in_specs=in_specs,
        out_specs=tok,
        out_shape=jax.ShapeDtypeStruct((rows, d), F32),
        compiler_params=_params(1),
        name=name,
    )(*args)


def _rwkv_proj_kernel(h_ref, sb_ref, vec_ref, mu_ref, wr_ref, wk_ref, wv_ref,
                      w1_ref, w2_ref, a1_ref, a2_ref, g1_ref, g2_ref,
                      r_o, k_o, v_o, kk_o, a_o, w_o, g_o, last_o, hs_ref):
    t = pl.program_id(1)
    tt = h_ref.shape[0]
    hist = SUBLANES
    hn = _rms(h_ref[...], vec_ref[0:1, :])

    @pl.when(t == 0)
    def _():
        hs_ref[...] = jnp.broadcast_to(sb_ref[...], hs_ref.shape)

    prev = pltpu.roll(jnp.concatenate([hs_ref[...], hn], axis=0), 1, 0)[hist:]
    hs_ref[hist - 1:hist, :] = hn[tt - 1:tt, :]
    last_o[...] = hn[tt - 1:tt, :]
    xx = prev - hn

    def mix(i):
        return (hn + xx * mu_ref[i:i + 1, :]).astype(BF16)

    r = _dot(mix(0), wr_ref[...])
    k = _dot(mix(2), wk_ref[...])
    v = _dot(mix(3), wv_ref[...])
    wl = vec_ref[1:2, :] + _dot(jnp.tanh(_dot(mix(1), w1_ref[...])), w2_ref[...])
    a = _sigmoid(vec_ref[2:3, :] + _dot(_dot(mix(4), a1_ref[...]), a2_ref[...]))
    gate = _dot(_sigmoid(_dot(mix(5), g1_ref[...])), g2_ref[...])
    r_o[...] = r.astype(r_o.dtype)
    k_o[...] = (k * (1.0 + (a - 1.0) * vec_ref[4:5, :])).astype(k_o.dtype)
    v_o[...] = v.astype(v_o.dtype)
    kk_o[...] = (k * vec_ref[3:4, :]).astype(kk_o.dtype)
    a_o[...] = a.astype(a_o.dtype)
    w_o[...] = -DECAY_SCALE * _sigmoid(wl)
    g_o[...] = gate.astype(g_o.dtype)


def _rwkv_proj(h3, sbuf, vec, mu8, wr, wk, wv, w1, w2, a1, a2, g1, g2):
    b, tp, d = h3.shape
    tt = _pick_tile(tp, (512, 256, 192, 128, 64))
    tok = pl.BlockSpec((None, tt, d), lambda i, t: (i, t, 0))
    row = pl.BlockSpec((None, 1, d), lambda i, t: (i, 0, 0))
    weights = [wr, wk, wv, w1, w2, a1, a2, g1, g2]
    tok_shape = lambda dt: jax.ShapeDtypeStruct((b, tp, d), dt)
    return pl.pallas_call(
        _rwkv_proj_kernel,
        grid=(b, tp // tt),
        in_specs=[tok, row, _const_spec(vec.shape), _const_spec(mu8.shape)] + [_const_spec(w.shape) for w in weights],
        out_specs=[tok] * 7 + [row],
        out_shape=[tok_shape(BF16)] * 5 + [tok_shape(F32), tok_shape(BF16), jax.ShapeDtypeStruct((b, 1, d), F32)],
        scratch_shapes=[pltpu.VMEM((SUBLANES, d), F32)],
        compiler_params=_params(2),
        name="rwkv_proj",
    )(h3, sbuf, vec, mu8, *weights)


def _rwkv_scan_kernel(r_ref, k_ref, v_ref, kk_ref, a_ref, w_ref, s0_ref, rk_ref, lnw_ref, lnb_ref,
                      yo_ref, s_ref):
    c = pl.program_id(1)
    L = CHUNK
    N = RWKV_N
    G = RWKV_GROUP
    Q = G * N
    nb, _, d = r_ref.shape
    groups = range(nb * d // Q)

    def cat_batch(ref):
        return jnp.concatenate([ref[i] for i in range(nb)], axis=1)

    @pl.when(c == 0)
    def _():
        s_ref[...] = s0_ref[...]

    row = lax.broadcasted_iota(jnp.int32, (L, L), 0)
    col = lax.broadcasted_iota(jnp.int32, (L, L), 1)
    tril = (row >= col).astype(BF16)
    t_idx = lax.broadcasted_iota(jnp.int32, (L, Q), 0)
    lane = lax.broadcasted_iota(jnp.int32, (L, Q), 1)
    s_idx = lane % N
    strict_c = t_idx > s_idx
    incl_c = t_idx >= s_idx
    eye_c = (t_idx == s_idx).astype(F32)
    ones_blk = (lax.broadcasted_iota(jnp.int32, (Q, Q), 0) // N
                == lax.broadcasted_iota(jnp.int32, (Q, Q), 1) // N).astype(BF16)
    head_lanes = [lax.broadcasted_iota(jnp.int32, (N, Q), 1) // N == g for g in range(G)]

    def per_group(x2d):
        return jnp.stack([x2d[:, q * Q:(q + 1) * Q] for q in groups])

    def expand(x3):
        xb = x3.astype(BF16)
        zero = jnp.zeros(xb.shape[:2] + (LANES,), BF16)
        slot_of_lane = lax.broadcasted_iota(jnp.int32, xb.shape[1:2] + (LANES,), 1) // N
        blocks = []
        for g in range(G):
            col, slot = divmod(g * N, LANES)
            piece = jnp.where(slot_of_lane == slot // N, xb[:, :, col * LANES:(col + 1) * LANES], zero)
            blocks.append(jnp.concatenate([piece if c == col else zero for c in range(Q // LANES)], axis=2))
        return jnp.concatenate(blocks, axis=1)

    def diag_blocks(x3):
        return sum(jnp.where(head_lanes[g], x3[:, g * N:(g + 1) * N, :], 0.0) for g in range(G))

    def lane_sums(x3):
        n, rws, _ = x3.shape
        return _dot(x3.reshape(n * rws, Q), ones_blk).reshape(n, rws, Q)

    w_all = cat_batch(w_ref)
    cw_all = _dot_exact_lhs(tril, w_all)
    cwl = cw_all[L - 1:L, :]
    p_in = per_group(jnp.exp(cw_all))
    p_ex = per_group(jnp.exp(cw_all - w_all))
    p_inv = per_group(jnp.exp(-cw_all))
    p_rem = per_group(jnp.exp(cwl - cw_all))
    dec_all = jnp.exp(cwl)
    p_all = jnp.stack([dec_all[:, q * Q:(q + 1) * Q] for q in groups])

    kkr = per_group(cat_batch(kk_ref).astype(F32))
    kkn = kkr * lax.rsqrt(lane_sums(kkr * kkr) + L2_EPS)
    rp = per_group(cat_batch(r_ref).astype(F32))
    kp = per_group(cat_batch(k_ref).astype(F32))
    vp = per_group(cat_batch(v_ref).astype(F32))
    bb = kkn * per_group(cat_batch(a_ref).astype(F32))

    lhs = jnp.concatenate([-kkn * p_ex, rp * p_in], axis=1).astype(BF16)
    m = _bdot_nt(lhs, jnp.concatenate([expand(bb * p_inv), expand(kp * p_inv)], axis=1))
    l_ab = jnp.where(strict_c, m[:, :L, :Q], 0.0)
    l_ak = jnp.where(strict_c, m[:, :L, Q:], 0.0)
    m_rb = jnp.where(incl_c, m[:, L:, :Q], 0.0)
    m_rk = jnp.where(incl_c, m[:, L:, Q:], 0.0)

    s = s_ref[...].reshape(len(groups), N, Q)
    from_s = _bdot_nt(lhs, expand(s))

    t_inv = eye_c + l_ab
    pw = _bdot(l_ab, expand(l_ab))
    n = 4
    while n < L:
        both = _bdot(jnp.concatenate([pw, t_inv], axis=1), expand(pw))
        t_inv = t_inv + both[:, L:]
        pw = both[:, :L]
        n *= 2
    t_inv = t_inv + _bdot(t_inv, expand(pw))

    ve = expand(vp)
    u = _bdot(t_inv, expand(from_s[:, :L] + _bdot(l_ak, ve)))
    y = from_s[:, L:] + _bdot(jnp.concatenate([m_rb, m_rk], axis=2),
                              jnp.concatenate([expand(u), ve], axis=1))
    cross = _bdot_tn(jnp.concatenate([u, vp], axis=1), jnp.concatenate([bb * p_rem, kp * p_rem], axis=1))
    s_ref[...] = (s * p_all + diag_blocks(cross)).reshape(s_ref.shape)

    mean = lane_sums(y) * (1.0 / N)
    cy = y - mean
    var = lane_sums(cy * cy) * (1.0 / N)
    rk3 = per_group(jnp.concatenate([rk_ref[...]] * nb, axis=1))
    bonus = lane_sums(rp * kp * rk3) * vp
    for g in groups:
        i, q = divmod(g, d // Q)
        sl = slice(q * Q, (q + 1) * Q)
        yn = cy[g] * lax.rsqrt(var[g] + GN_EPS) * lnw_ref[:, sl] + lnb_ref[:, sl]
        yo_ref[i, :, sl] = (yn + bonus[g]).astype(yo_ref.dtype)


def _rwkv_scan(r, k, v, kk, a, w, s0_groups, rk_row, lnw_row, lnb_row):
    b, tp, d = r.shape
    q = RWKV_GROUP * RWKV_N
    nb = _scan_batch(b, most=8)
    tok = pl.BlockSpec((nb, CHUNK, d), lambda i, c: (i, c, 0))
    st = pl.BlockSpec((nb, d // q, RWKV_N, q), lambda i, c: (i, 0, 0, 0))
    return pl.pallas_call(
        _rwkv_scan_kernel,
        grid=(b // nb, tp // CHUNK),
        in_specs=[tok] * 6 + [st, _const_spec((1, d)), _const_spec((1, d)), _const_spec((1, d))],
        out_specs=[tok, st],
        out_shape=[jax.ShapeDtypeStruct((b, tp, d), BF16),
                   jax.ShapeDtypeStruct((b, d // q, RWKV_N, q), F32)],
        compiler_params=_params(2),
        name="rwkv_scan",
    )(r, k, v, kk, a, w, s0_groups, rk_row, lnw_row, lnb_row)


def _pad_cols(w, n):
    return jnp.pad(w, ((0, 0), (0, n - w.shape[1])))


def _pad_rows(w, n):
    return jnp.pad(w, ((0, n - w.shape[0]), (0, 0)))


def _groups_from_state(s):
    b, h, n, _ = s.shape
    g = RWKV_GROUP
    return jnp.swapaxes(s.reshape(b, h // g, g, n, n), 2, 3).reshape(b, h // g, n, g * n)


def _state_from_groups(sg):
    b, hq, n, gn = sg.shape
    g = RWKV_GROUP
    return jnp.swapaxes(sg.reshape(b, hq, n, g, n), 2, 3).reshape(b, hq * g, n, n)


def _run_trunk(h, conv_bufs, gdn_states, shift_bufs, rwkv_states, wts):
    b, tp, d = h.shape
    rows = b * tp
    depth = wts["norm_mix"].shape[0]
    new_conv, new_gdn, new_shift, new_rwkv = [], [], [], []
    for i in range(depth):
        j = i // 2
        fn_row = wts["norm_final"] if i == depth - 1 else None
        if i % 2 == 0:
            g = wts["gdn"][j]
            cbuf8 = jnp.pad(conv_bufs[j], ((0, 0), (SUBLANES - (CONV_W - 1), 0), (0, 0)))
            q, k, v, z, ab, cst = _gdn_proj(h, wts["norm_mix"][i:i + 1], g["w_in"], g["conv_w"], cbuf8)
            o, s_new = _gdn_scan(q, k, v, ab, gdn_states[j], g["a_log"], g["dt_bias"], g["o_norm"])
            new_conv.append(cst[:, SUBLANES - (CONV_W - 1):])
            new_gdn.append(s_new)
            h2 = _mix_ffn(o.reshape(rows, d), z.reshape(rows, d), h.reshape(rows, d), g["w_out"],
                          wts["norm_ffn"][i:i + 1], wts["ffn_gate"][i], wts["ffn_up"][i], wts["ffn_down"][i],
                          fn_row, silu_gate=True, name="gdn_out_ffn")
        else:
            rw = wts["rwkv"][j]
            vec = jnp.concatenate([wts["norm_mix"][i:i + 1], rw["vec"]], axis=0)
            r, k, v, kk, a, w, gate, last_hn = _rwkv_proj(
                h, shift_bufs[j], vec, rw["mu"], rw["w_r"], rw["w_k"], rw["w_v"],
                rw["w1"], rw["w2"], rw["a1"], rw["a2"], rw["g1"], rw["g2"])
            yo, sp_new = _rwkv_scan(r, k, v, kk, a, w, _groups_from_state(rwkv_states[j]),
                                    rw["r_k"], rw["ln_w"], rw["ln_b"])
            new_shift.append(last_hn)
            new_rwkv.append(_state_from_groups(sp_new))
            h2 = _mix_ffn(yo.reshape(rows, d), gate.reshape(rows, d), h.reshape(rows, d), rw["w_o"],
                          wts["norm_ffn"][i:i + 1], wts["ffn_gate"][i], wts["ffn_up"][i], wts["ffn_down"][i],
                          fn_row, silu_gate=False, name="rwkv_out_ffn")
        h = h2.reshape(b, tp, d)
    return h, jnp.stack(new_conv), jnp.stack(new_gdn), jnp.stack(new_shift), jnp.stack(new_rwkv)


def _prepare_weights(norm_mix, norm_ffn, norm_final, gdn_w_in, gdn_conv_w, gdn_a_log, gdn_dt_bias, gdn_o_norm,
                     gdn_w_out, rwkv_mu, rwkv_w0, rwkv_w1, rwkv_w2, rwkv_a0, rwkv_a1, rwkv_a2, rwkv_g1, rwkv_g2,
                     rwkv_k_k, rwkv_k_a, rwkv_r_k, rwkv_w_r, rwkv_w_k, rwkv_w_v, rwkv_w_o, rwkv_ln_w,
                     rwkv_ln_b, ffn_w_gate, ffn_w_up, ffn_w_down):
    d = norm_mix.shape[1]
    gdn = []
    for j in range(gdn_w_in.shape[0]):
        w = gdn_w_in[j]
        qkvz = w[:, :GDN_QKV + GDN_KDIM]
        a_cols = _pad_cols(w[:, GDN_QKV + GDN_KDIM:GDN_QKV + GDN_KDIM + GDN_HEADS], LANES)
        b_cols = _pad_cols(w[:, GDN_QKV + GDN_KDIM + GDN_HEADS:], LANES)
        gdn.append(dict(
            w_in=jnp.concatenate([qkvz, a_cols, b_cols], axis=1).astype(BF16),
            conv_w=gdn_conv_w[j],
            a_log=_pad_cols(gdn_a_log[j][None], LANES),
            dt_bias=_pad_cols(gdn_dt_bias[j][None], LANES),
            o_norm=gdn_o_norm[j][None],
            w_out=gdn_w_out[j].astype(BF16),
        ))
    rwkv = []
    for j in range(rwkv_mu.shape[0]):
        lw = -(-rwkv_w1.shape[2] // LANES) * LANES
        la = -(-rwkv_a1.shape[2] // LANES) * LANES
        lg = -(-rwkv_g1.shape[2] // LANES) * LANES
        rwkv.append(dict(
            vec=jnp.stack([rwkv_w0[j], rwkv_a0[j], rwkv_k_k[j], rwkv_k_a[j],
                           jnp.zeros_like(rwkv_w0[j]), jnp.zeros_like(rwkv_w0[j]), jnp.zeros_like(rwkv_w0[j])]),
            mu=_pad_rows(rwkv_mu[j], SUBLANES),
            w_r=rwkv_w_r[j].astype(BF16), w_k=rwkv_w_k[j].astype(BF16), w_v=rwkv_w_v[j].astype(BF16),
            w_o=rwkv_w_o[j].astype(BF16),
            w1=_pad_cols(rwkv_w1[j], lw).astype(BF16), w2=_pad_rows(rwkv_w2[j], lw).astype(BF16),
            a1=_pad_cols(rwkv_a1[j], la).astype(BF16), a2=_pad_rows(rwkv_a2[j], la).astype(BF16),
            g1=_pad_cols(rwkv_g1[j], lg).astype(BF16), g2=_pad_rows(rwkv_g2[j], lg).astype(BF16),
            r_k=rwkv_r_k[j].reshape(1, d), ln_w=rwkv_ln_w[j][None], ln_b=rwkv_ln_b[j][None],
        ))
    return dict(norm_mix=norm_mix, norm_ffn=norm_ffn, norm_final=norm_final[None], gdn=gdn, rwkv=rwkv,
                ffn_gate=ffn_w_gate.astype(BF16), ffn_up=ffn_w_up.astype(BF16), ffn_down=ffn_w_down.astype(BF16))


def kernel(x_prompt, x_sample, cache_gdn_conv, state_gdn, cache_rwkv_shift, state_rwkv, meta_tokens, norm_mix, norm_ffn, norm_final, gdn_w_in, gdn_conv_w, gdn_a_log, gdn_dt_bias, gdn_o_norm, gdn_w_out, rwkv_mu, rwkv_w0, rwkv_w1, rwkv_w2, rwkv_a0, rwkv_a1, rwkv_a2, rwkv_g1, rwkv_g2, rwkv_k_k, rwkv_k_a, rwkv_r_k, rwkv_w_r, rwkv_w_k, rwkv_w_v, rwkv_w_o, rwkv_ln_w, rwkv_ln_b, ffn_w_gate, ffn_w_up, ffn_w_down):
    wts = _prepare_weights(norm_mix, norm_ffn, norm_final, gdn_w_in, gdn_conv_w, gdn_a_log, gdn_dt_bias,
                           gdn_o_norm, gdn_w_out, rwkv_mu, rwkv_w0, rwkv_w1, rwkv_w2, rwkv_a0, rwkv_a1, rwkv_a2,
                           rwkv_g1, rwkv_g2, rwkv_k_k, rwkv_k_a, rwkv_r_k, rwkv_w_r, rwkv_w_k, rwkv_w_v,
                           rwkv_w_o, rwkv_ln_w, rwkv_ln_b, ffn_w_gate, ffn_w_up, ffn_w_down)
    n_gdn = gdn_w_in.shape[0]
    n_rwkv = rwkv_mu.shape[0]
    b, seq, d = x_prompt.shape
    dt = x_prompt.dtype

    lead_pad = (-N_META) % CHUNK
    lead = jnp.concatenate([jnp.zeros((b, lead_pad, d), dt),
                            jnp.broadcast_to(meta_tokens.astype(dt)[None], (b, N_META, d))], axis=1)
    zeros_like_states = (jnp.zeros((n_gdn, b, CONV_W - 1, GDN_QKV), dt),
                         jnp.zeros((n_gdn, b, GDN_HEADS, GDN_DK, GDN_DV), dt),
                         jnp.zeros((n_rwkv, b, 1, d), dt),
                         jnp.zeros((n_rwkv, b, d // RWKV_N, RWKV_N, RWKV_N), dt))
    sample_states = (cache_gdn_conv, state_gdn, cache_rwkv_shift, state_rwkv)

    bs = x_sample.shape[0]
    if x_sample.shape[1:] == lead.shape[1:]:
        both = _run_trunk(jnp.concatenate([x_sample, lead], axis=0),
                          *[jnp.concatenate([s, z], axis=1) for s, z in zip(sample_states, zeros_like_states)],
                          wts)
        y_sample, s_conv, s_gdn, s_shift, s_rwkv = [both[0][:bs]] + [t[:, :bs] for t in both[1:]]
        lead_states = [t[:, bs:] for t in both[1:]]
    else:
        y_sample, s_conv, s_gdn, s_shift, s_rwkv = _run_trunk(x_sample, *sample_states, wts)
        lead_states = _run_trunk(lead, *zeros_like_states, wts)[1:]

    y_prompt, p_conv, p_gdn, p_shift, p_rwkv = _run_trunk(x_prompt, *lead_states, wts)

    return (y_prompt, y_sample, p_conv, p_gdn, p_shift, p_rwkv, s_conv, s_gdn, s_shift, s_rwkv)
```

```python
import functools

import jax
import jax.numpy as jnp
from jax import lax
from jax.experimental import pallas as pl
from jax.experimental.pallas import tpu as pltpu

F32 = jnp.float32
BF16 = jnp.bfloat16

CHUNK = 64
N_META = 16
EPS = 1e-6
L2_EPS = 1e-6
GN_EPS = 64e-5

GDN_HEADS = 8
GDN_DK = 128
GDN_DV = 128
GDN_KDIM = GDN_HEADS * GDN_DK
GDN_QKV = 3 * GDN_KDIM
CONV_W = 4
RWKV_N = 64
RWKV_GROUP = 4
DECAY_SCALE = 0.6065306597126334

LANES = 128
SUBLANES = 8
VMEM_LIMIT = 56 * 1024 * 1024


def _dot(a, b):
    return jnp.dot(a.astype(BF16), b.astype(BF16), preferred_element_type=F32)


def _bdot(a, b):
    return lax.dot_general(a.astype(BF16), b.astype(BF16), (((2,), (1,)), ((0,), (0,))),
                           preferred_element_type=F32)


def _bdot_nt(a, b):
    return lax.dot_general(a.astype(BF16), b.astype(BF16), (((2,), (2,)), ((0,), (0,))),
                           preferred_element_type=F32)


def _bdot_tn(a, b):
    return lax.dot_general(a.astype(BF16), b.astype(BF16), (((1,), (1,)), ((0,), (0,))),
                           preferred_element_type=F32)


def _split_bf16(x, parts):
    out = []
    for _ in range(parts - 1):
        hi = x.astype(BF16)
        out.append(hi)
        x = x - hi.astype(F32)
    out.append(x.astype(BF16))
    return out


def _dot_exact_lhs(m01, x, parts=3):
    return sum(_dot(m01, t) for t in _split_bf16(x, parts))


def _sigmoid(x):
    return 0.5 + 0.5 * jnp.tanh(0.5 * x)


def _silu(x):
    h = 0.5 * x
    return h + h * jnp.tanh(h)


def _rms(x, g, eps=EPS):
    return x * lax.rsqrt(jnp.mean(x * x, axis=-1, keepdims=True) + eps) * g


def _neumann_cat(x, eye, size, expand):
    rows = x.shape[1]
    t = eye + x
    p = _bdot(x, expand(x))
    n = 4
    while n < size:
        both = _bdot(jnp.concatenate([p, t], axis=1), expand(p))
        t = t + both[:, rows:]
        p = both[:, :rows]
        n *= 2
    return t + _bdot(t, expand(p))


def _pick_tile(n, candidates):
    for c in candidates:
        if n % c == 0:
            return c
    raise ValueError(f"no tile for {n}")


def _params(grid_rank):
    return pltpu.CompilerParams(dimension_semantics=("arbitrary",) * grid_rank, vmem_limit_bytes=VMEM_LIMIT)


def _const_spec(shape):
    nd = len(shape)
    return pl.BlockSpec(shape, lambda *_: (0,) * nd, pipeline_mode=pl.Buffered(1))


def _gdn_proj_kernel(x_ref, g_ref, w_ref, cw_ref, cbuf_ref, q_o, k_o, v_o, z_o, ab_o, cst_o, xs_ref):
    t = pl.program_id(1)
    tt = x_ref.shape[0]
    hist = SUBLANES

    @pl.when(t == 0)
    def _():
        xs_ref[...] = cbuf_ref[...]

    xn = _rms(x_ref[...], g_ref[...]).astype(BF16)

    for out, c0 in ((q_o, 0), (k_o, GDN_KDIM), (v_o, 2 * GDN_KDIM)):
        cols = slice(c0, c0 + GDN_KDIM)
        proj = _dot(xn, w_ref[:, cols])
        win = jnp.concatenate([xs_ref[:, cols], proj], axis=0)
        xs_ref[:, cols] = proj[tt - hist:tt]
        cst_o[:, cols] = proj[tt - hist:tt]
        win1 = pltpu.roll(win, 1, 0)
        near = cw_ref[3:4, cols] * win + cw_ref[2:3, cols] * win1
        far = cw_ref[1:2, cols] * win + cw_ref[0:1, cols] * win1
        out[...] = (near + pltpu.roll(far, 2, 0))[hist:].astype(out.dtype)
    z_o[...] = _dot(xn, w_ref[:, GDN_QKV:GDN_QKV + GDN_KDIM]).astype(z_o.dtype)
    ab_o[...] = _dot(xn, w_ref[:, GDN_QKV + GDN_KDIM:])


def _gdn_proj(h3, g_row, w_bf16, conv_w, cbuf8):
    b, tp, d = h3.shape
    n = w_bf16.shape[1]
    tt = _pick_tile(tp, (256, 192, 128, 64))
    tok = lambda width: pl.BlockSpec((None, tt, width), lambda i, t: (i, t, 0))
    hist_spec = pl.BlockSpec((None, SUBLANES, GDN_QKV), lambda i, t: (i, 0, 0))
    return pl.pallas_call(
        _gdn_proj_kernel,
        grid=(b, tp // tt),
        in_specs=[tok(d), _const_spec((1, d)), _const_spec((d, n)), _const_spec((CONV_W, GDN_QKV)), hist_spec],
        out_specs=[tok(GDN_KDIM), tok(GDN_KDIM), tok(GDN_KDIM), tok(GDN_KDIM), tok(2 * LANES), hist_spec],
        out_shape=[jax.ShapeDtypeStruct((b, tp, GDN_KDIM), BF16)] * 4
        + [jax.ShapeDtypeStruct((b, tp, 2 * LANES), F32), jax.ShapeDtypeStruct((b, SUBLANES, GDN_QKV), F32)],
        scratch_shapes=[pltpu.VMEM((SUBLANES, GDN_QKV), F32)],
        compiler_params=_params(2),
        name="gdn_in_proj",
    )(h3, g_row, w_bf16, conv_w, cbuf8)


def _gdn_scan_kernel(q_ref, k_ref, v_ref, ab_ref, s0_ref, alog_ref, dtb_ref, onorm_ref, o_ref, s_ref):
    c = pl.program_id(1)
    L = CHUNK
    D = GDN_DK
    HP = GDN_HEADS // 2

    @pl.when(c == 0)
    def _():
        s_ref[...] = s0_ref[...]

    row = lax.broadcasted_iota(jnp.int32, (L, L), 0)
    col = lax.broadcasted_iota(jnp.int32, (L, L), 1)
    tril = (row >= col).astype(BF16)
    t_idx = lax.broadcasted_iota(jnp.int32, (L, 2 * L), 0)
    lane = lax.broadcasted_iota(jnp.int32, (L, 2 * L), 1)
    first = lane < L
    s_idx = lane % L
    causal_c = t_idx >= s_idx
    strict_c = t_idx > s_idx
    eye_c = (t_idx == s_idx).astype(F32)

    nb = q_ref.shape[0]
    a_in = jnp.concatenate([ab_ref[i, :, :LANES] + dtb_ref[...] for i in range(nb)], axis=1)
    b_in = jnp.concatenate([ab_ref[i, :, LANES:] for i in range(nb)], axis=1)
    neg_a = jnp.concatenate([-jnp.exp(alog_ref[...])] * nb, axis=1)
    g = neg_a * jax.nn.softplus(a_in)
    beta = _sigmoid(b_in)
    G = _dot_exact_lhs(tril, g)
    Gt = G.T

    pairs = [(i, p) for i in range(nb) for p in range(HP)]

    def per_pair(ref):
        return jnp.stack([ref[i, :, 2 * p * D:2 * (p + 1) * D] for i, p in pairs])

    def halves(x):
        return x[..., :D], x[..., D:]

    def per_half(fn, x):
        xa, xb = halves(x)
        return jnp.concatenate([fn(xa), fn(xb)], axis=-1)

    def head_cols(x2d):
        return [jnp.stack([x2d[:, i * LANES + 2 * p + j:i * LANES + 2 * p + j + 1] for i, p in pairs])
                for j in range(2)]

    def wide(cols):
        return jnp.concatenate([jnp.broadcast_to(c, c.shape[:2] + (D,)) for c in cols], axis=-1)

    def cat(cols):
        return jnp.where(first, cols[0], cols[1])

    def expand(x):
        xa, xb = halves(x.astype(BF16))
        zero = jnp.zeros_like(xa)
        return jnp.concatenate([jnp.concatenate([xa, zero], axis=-1),
                                jnp.concatenate([zero, xb], axis=-1)], axis=1)

    def expand_cat(x):
        xb = x.astype(BF16)
        zero = jnp.zeros_like(xb)
        return jnp.concatenate([jnp.where(first, xb, zero), jnp.where(first, zero, xb)], axis=1)

    def l2n(x):
        return x * lax.rsqrt(jnp.sum(x * x, axis=-1, keepdims=True) + L2_EPS)

    q = per_half(l2n, _silu(per_pair(q_ref).astype(F32))) * (GDN_DK ** -0.5)
    kf = per_half(l2n, _silu(per_pair(k_ref).astype(F32)))
    v = _silu(per_pair(v_ref).astype(F32))
    k = kf.astype(BF16)

    b_cols = head_cols(beta)
    g_cols = head_cols(G)
    g_last = [c[:, L - 1:L, :] for c in g_cols]
    g_row = jnp.stack([jnp.concatenate([Gt[i * LANES + 2 * p + j:i * LANES + 2 * p + j + 1, :] for j in range(2)],
                                       axis=1) for i, p in pairs])
    b_wide = wide(b_cols)
    eg_wide = wide([jnp.exp(c) for c in g_cols])
    egr_wide = wide([jnp.exp(gl - c) for gl, c in zip(g_last, g_cols)])
    egl_wide = wide([jnp.exp(gl) for gl in g_last])

    dlt = cat(g_cols) - g_row
    dec = jnp.where(causal_c, jnp.exp(jnp.where(causal_c, dlt, 0.0)), 0.0)
    kq_k = _bdot_nt(jnp.concatenate([k, q.astype(BF16)], axis=1), expand(k))
    a_mat = jnp.where(strict_c, cat(b_cols) * kq_k[:, :L] * dec, 0.0)
    qk = kq_k[:, L:] * dec
    t_inv = _neumann_cat(-a_mat, eye_c, L, expand_cat)

    s = s_ref[...].reshape(len(pairs), D, 2 * D)
    ks_qs = _bdot(jnp.concatenate([k, (q * eg_wide).astype(BF16)], axis=1), expand(s))
    v_new = _bdot(t_inv, expand(b_wide * (v - eg_wide * ks_qs[:, :L])))
    o = ks_qs[:, L:] + _bdot(qk, expand(v_new))
    cross = _bdot_tn(kf * egr_wide, v_new)
    s_ref[...] = (s * egl_wide + jnp.concatenate([cross[:, :D, :D], cross[:, D:, D:]], axis=-1)).reshape(s_ref.shape)

    def o_norm(x):
        return x * lax.rsqrt(jnp.mean(x * x, axis=-1, keepdims=True) + EPS) * onorm_ref[...]

    o = per_half(o_norm, o)
    for n, (i, p) in enumerate(pairs):
        o_ref[i, :, 2 * p * D:2 * (p + 1) * D] = o[n].astype(o_ref.dtype)


def _scan_batch(b, most=4):
    return next(n for n in (8, 4, 2, 1) if n <= most and b % n == 0)


def _gdn_scan(q, k, v, ab, s0, alog_row, dtb_row, onorm_row):
    b, tp, _ = q.shape
    nb = _scan_batch(b)
    tok = pl.BlockSpec((nb, CHUNK, GDN_KDIM), lambda i, c: (i, c, 0))
    st = pl.BlockSpec((nb, GDN_HEADS // 2, GDN_DK, 2 * GDN_DV), lambda i, c: (i, 0, 0, 0))
    return pl.pallas_call(
        _gdn_scan_kernel,
        grid=(b // nb, tp // CHUNK),
        in_specs=[tok, tok, tok, pl.BlockSpec((nb, CHUNK, 2 * LANES), lambda i, c: (i, c, 0)), st,
                  _const_spec((1, LANES)), _const_spec((1, LANES)), _const_spec((1, GDN_DV))],
        out_specs=[tok, st],
        out_shape=[jax.ShapeDtypeStruct((b, tp, GDN_KDIM), BF16),
                   jax.ShapeDtypeStruct((b, GDN_HEADS // 2, GDN_DK, 2 * GDN_DV), F32)],
        compiler_params=_params(2),
        name="gdn_scan",
    )(q, k, v, ab, s0, alog_row, dtb_row, onorm_row)


def _mix_ffn_kernel(*refs, silu_gate, final_norm):
    if final_norm:
        y_ref, gm_ref, h_ref, wo_ref, nf_ref, wg_ref, wu_ref, wd_ref, fn_ref, o_ref = refs
    else:
        y_ref, gm_ref, h_ref, wo_ref, nf_ref, wg_ref, wu_ref, wd_ref, o_ref = refs
    gm = gm_ref[...].astype(F32)
    if silu_gate:
        gm = _silu(gm)
    h1 = h_ref[...] + _dot(y_ref[...].astype(F32) * gm, wo_ref[...])
    n = _rms(h1, nf_ref[...])
    act = _silu(_dot(n, wg_ref[...])) * _dot(n, wu_ref[...])
    h2 = h1 + _dot(act, wd_ref[...])
    if final_norm:
        h2 = _rms(h2, fn_ref[...])
    o_ref[...] = h2


def _mix_ffn(y2d, gm2d, h2d, wo, nf_row, wg, wu, wd, fn_row, *, silu_gate, name):
    rows, d = h2d.shape
    dff = wg.shape[1]
    tm = _pick_tile(rows, (512, 256, 128, 64))
    final_norm = fn_row is not None
    tok = pl.BlockSpec((tm, d), lambda i: (i, 0))
    in_specs = [tok, tok, tok, _const_spec((d, d)), _const_spec((1, d)),
                _const_spec((d, dff)), _const_spec((d, dff)), _const_spec((dff, d))]
    args = [y2d, gm2d, h2d, wo, nf_row, wg, wu, wd]
    if final_norm:
        in_specs.append(_const_spec((1, d)))
        args.append(fn_row)
    return pl.pallas_call(
        functools.partial(_mix_ffn_kernel, silu_gate=silu_gate, final_norm=final_norm),
        grid=(rows // tm,),
        in_specs=in_specs,
        out_specs=tok,
        out_shape=jax.ShapeDtypeStruct((rows, d), F32),
        compiler_params=_params(1),
        name=name,
    )(*args)


def _rwkv_proj_kernel(h_ref, sb_ref, vec_ref, mu_ref, wr_ref, wk_ref, wv_ref,
                      w1_ref, w2_ref, a1_ref, a2_ref, g1_ref, g2_ref,
                      r_o, k_o, v_o, kk_o, a_o, w_o, g_o, last_o, hs_ref):
    t = pl.program_id(1)
    tt = h_ref.shape[0]
    hist = SUBLANES
    hn = _rms(h_ref[...], vec_ref[0:1, :])

    @pl.when(t == 0)
    def _():
        hs_ref[...] = jnp.broadcast_to(sb_ref[...], hs_ref.shape)

    prev = pltpu.roll(jnp.concatenate([hs_ref[...], hn], axis=0), 1, 0)[hist:]
    hs_ref[hist - 1:hist, :] = hn[tt - 1:tt, :]
    last_o[...] = hn[tt - 1:tt, :]
    xx = prev - hn

    def mix(i):
        return (hn + xx * mu_ref[i:i + 1, :]).astype(BF16)

    r = _dot(mix(0), wr_ref[...])
    k = _dot(mix(2), wk_ref[...])
    v = _dot(mix(3), wv_ref[...])
    wl = vec_ref[1:2, :] + _dot(jnp.tanh(_dot(mix(1), w1_ref[...])), w2_ref[...])
    a = _sigmoid(vec_ref[2:3, :] + _dot(_dot(mix(4), a1_ref[...]), a2_ref[...]))
    gate = _dot(_sigmoid(_dot(mix(5), g1_ref[...])), g2_ref[...])
    r_o[...] = r.astype(r_o.dtype)
    k_o[...] = (k * (1.0 + (a - 1.0) * vec_ref[4:5, :])).astype(k_o.dtype)
    v_o[...] = v.astype(v_o.dtype)
    kk_o[...] = (k * vec_ref[3:4, :]).astype(kk_o.dtype)
    a_o[...] = a.astype(a_o.dtype)
    w_o[...] = -DECAY_SCALE * _sigmoid(wl)
    g_o[...] = gate.astype(g_o.dtype)


def _rwkv_proj(h3, sbuf, vec, mu8, wr, wk, wv, w1, w2, a1, a2, g1, g2):
    b, tp, d = h3.shape
    tt = _pick_tile(tp, (512, 256, 192, 128, 64))
    tok = pl.BlockSpec((None, tt, d), lambda i, t: (i, t, 0))
    row = pl.BlockSpec((None, 1, d), lambda i, t: (i, 0, 0))
    weights = [wr, wk, wv, w1, w2, a1, a2, g1, g2]
    tok_shape = lambda dt: jax.ShapeDtypeStruct((b, tp, d), dt)
    return pl.pallas_call(
        _rwkv_proj_kernel,
        grid=(b, tp // tt),
        in_specs=[tok, row, _const_spec(vec.shape), _const_spec(mu8.shape)] + [_const_spec(w.shape) for w in weights],
        out_specs=[tok] * 7 + [row],
        out_shape=[tok_shape(BF16)] * 5 + [tok_shape(F32), tok_shape(BF16), jax.ShapeDtypeStruct((b, 1, d), F32)],
        scratch_shapes=[pltpu.VMEM((SUBLANES, d), F32)],
        compiler_params=_params(2),
        name="rwkv_proj",
    )(h3, sbuf, vec, mu8, *weights)


def _rwkv_scan_kernel(r_ref, k_ref, v_ref, kk_ref, a_ref, w_ref, s0_ref, rk_ref, lnw_ref, lnb_ref,
                      yo_ref, s_ref):
    c = pl.program_id(1)
    L = CHUNK
    N = RWKV_N
    G = RWKV_GROUP
    Q = G * N
    nb, _, d = r_ref.shape
    groups = range(nb * d // Q)

    def cat_batch(ref):
        return jnp.concatenate([ref[i] for i in range(nb)], axis=1)

    @pl.when(c == 0)
    def _():
        s_ref[...] = s0_ref[...]

    row = lax.broadcasted_iota(jnp.int32, (L, L), 0)
    col = lax.broadcasted_iota(jnp.int32, (L, L), 1)
    tril = (row >= col).astype(BF16)
    t_idx = lax.broadcasted_iota(jnp.int32, (L, Q), 0)
    lane = lax.broadcasted_iota(jnp.int32, (L, Q), 1)
    s_idx = lane % N
    strict_c = t_idx > s_idx
    incl_c = t_idx >= s_idx
    eye_c = (t_idx == s_idx).astype(F32)
    ones_blk = (lax.broadcasted_iota(jnp.int32, (Q, Q), 0) // N
                == lax.broadcasted_iota(jnp.int32, (Q, Q), 1) // N).astype(BF16)
    head_lanes = [lax.broadcasted_iota(jnp.int32, (N, Q), 1) // N == g for g in range(G)]

    def per_group(x2d):
        return jnp.stack([x2d[:, q * Q:(q + 1) * Q] for q in groups])

    def expand(x3):
        xb = x3.astype(BF16)
        zero = jnp.zeros(xb.shape[:2] + (LANES,), BF16)
        slot_of_lane = lax.broadcasted_iota(jnp.int32, xb.shape[1:2] + (LANES,), 1) // N
        blocks = []
        for g in range(G):
            col, slot = divmod(g * N, LANES)
            piece = jnp.where(slot_of_lane == slot // N, xb[:, :, col * LANES:(col + 1) * LANES], zero)
            blocks.append(jnp.concatenate([piece if c == col else zero for c in range(Q // LANES)], axis=2))
        return jnp.concatenate(blocks, axis=1)

    def diag_blocks(x3):
        return sum(jnp.where(head_lanes[g], x3[:, g * N:(g + 1) * N, :], 0.0) for g in range(G))

    def lane_sums(x3):
        n, rws, _ = x3.shape
        return _dot(x3.reshape(n * rws, Q), ones_blk).reshape(n, rws, Q)

    w_all = cat_batch(w_ref)
    cw_all = _dot_exact_lhs(tril, w_all)
    cwl = cw_all[L - 1:L, :]
    p_in = per_group(jnp.exp(cw_all))
    p_ex = per_group(jnp.exp(cw_all - w_all))
    p_inv = per_group(jnp.exp(-cw_all))
    p_rem = per_group(jnp.exp(cwl - cw_all))
    dec_all = jnp.exp(cwl)
    p_all = jnp.stack([dec_all[:, q * Q:(q + 1) * Q] for q in groups])

    kkr = per_group(cat_batch(kk_ref).astype(F32))
    kkn = kkr * lax.rsqrt(lane_sums(kkr * kkr) + L2_EPS)
    rp = per_group(cat_batch(r_ref).astype(F32))
    kp = per_group(cat_batch(k_ref).astype(F32))
    vp = per_group(cat_batch(v_ref).astype(F32))
    bb = kkn * per_group(cat_batch(a_ref).astype(F32))

    lhs = jnp.concatenate([-kkn * p_ex, rp * p_in], axis=1).astype(BF16)
    m = _bdot_nt(lhs, jnp.concatenate([expand(bb * p_inv), expand(kp * p_inv)], axis=1))
    l_ab = jnp.where(strict_c, m[:, :L, :Q], 0.0)
    l_ak = jnp.where(strict_c, m[:, :L, Q:], 0.0)
    m_rb = jnp.where(incl_c, m[:, L:, :Q], 0.0)
    m_rk = jnp.where(incl_c, m[:, L:, Q:], 0.0)

    s = s_ref[...].reshape(len(groups), N, Q)
    from_s = _bdot_nt(lhs, expand(s))

    t_inv = _neumann_cat(l_ab, eye_c, L, expand)

    ve = expand(vp)
    u = _bdot(t_inv, expand(from_s[:, :L] + _bdot(l_ak, ve)))
    y = from_s[:, L:] + _bdot(jnp.concatenate([m_rb, m_rk], axis=2),
                              jnp.concatenate([expand(u), ve], axis=1))
    cross = _bdot_tn(jnp.concatenate([u, vp], axis=1), jnp.concatenate([bb * p_rem, kp * p_rem], axis=1))
    s_ref[...] = (s * p_all + diag_blocks(cross)).reshape(s_ref.shape)

    mean = lane_sums(y) * (1.0 / N)
    cy = y - mean
    var = lane_sums(cy * cy) * (1.0 / N)
    rk3 = per_group(jnp.concatenate([rk_ref[...]] * nb, axis=1))
    bonus = lane_sums(rp * kp * rk3) * vp
    for g in groups:
        i, q = divmod(g, d // Q)
        sl = slice(q * Q, (q + 1) * Q)
        yn = cy[g] * lax.rsqrt(var[g] + GN_EPS) * lnw_ref[:, sl] + lnb_ref[:, sl]
        yo_ref[i, :, sl] = (yn + bonus[g]).astype(yo_ref.dtype)


def _rwkv_scan(r, k, v, kk, a, w, s0_groups, rk_row, lnw_row, lnb_row):
    b, tp, d = r.shape
    q = RWKV_GROUP * RWKV_N
    nb = _scan_batch(b, most=8)
    tok = pl.BlockSpec((nb, CHUNK, d), lambda i, c: (i, c, 0))
    st = pl.BlockSpec((nb, d // q, RWKV_N, q), lambda i, c: (i, 0, 0, 0))
    return pl.pallas_call(
        _rwkv_scan_kernel,
        grid=(b // nb, tp // CHUNK),
        in_specs=[tok] * 6 + [st, _const_spec((1, d)), _const_spec((1, d)), _const_spec((1, d))],
        out_specs=[tok, st],
        out_shape=[jax.ShapeDtypeStruct((b, tp, d), BF16),
                   jax.ShapeDtypeStruct((b, d // q, RWKV_N, q), F32)],
        compiler_params=_params(2),
        name="rwkv_scan",
    )(r, k, v, kk, a, w, s0_groups, rk_row, lnw_row, lnb_row)


def _pad_cols(w, n):
    return jnp.pad(w, ((0, 0), (0, n - w.shape[1])))


def _pad_rows(w, n):
    return jnp.pad(w, ((0, n - w.shape[0]), (0, 0)))


def _groups_from_state(s, g):
    b, h, r, c = s.shape
    return jnp.swapaxes(s.reshape(b, h // g, g, r, c), 2, 3).reshape(b, h // g, r, g * c)


def _state_from_groups(sg, g):
    b, hq, r, gc = sg.shape
    return jnp.swapaxes(sg.reshape(b, hq, r, g, gc // g), 2, 3).reshape(b, hq * g, r, gc // g)


def _run_trunk(h, conv_bufs, gdn_states, shift_bufs, rwkv_states, wts):
    b, tp, d = h.shape
    rows = b * tp
    depth = wts["norm_mix"].shape[0]
    new_conv, new_gdn, new_shift, new_rwkv = [], [], [], []
    for i in range(depth):
        j = i // 2
        fn_row = wts["norm_final"] if i == depth - 1 else None
        if i % 2 == 0:
            g = wts["gdn"][j]
            cbuf8 = jnp.pad(conv_bufs[j], ((0, 0), (SUBLANES - (CONV_W - 1), 0), (0, 0)))
            q, k, v, z, ab, cst = _gdn_proj(h, wts["norm_mix"][i:i + 1], g["w_in"], g["conv_w"], cbuf8)
            o, s_new = _gdn_scan(q, k, v, ab, _groups_from_state(gdn_states[j], 2), g["a_log"], g["dt_bias"],
                                 g["o_norm"])
            new_conv.append(cst[:, SUBLANES - (CONV_W - 1):])
            new_gdn.append(_state_from_groups(s_new, 2))
            h2 = _mix_ffn(o.reshape(rows, d), z.reshape(rows, d), h.reshape(rows, d), g["w_out"],
                          wts["norm_ffn"][i:i + 1], wts["ffn_gate"][i], wts["ffn_up"][i], wts["ffn_down"][i],
                          fn_row, silu_gate=True, name="gdn_out_ffn")
        else:
            rw = wts["rwkv"][j]
            vec = jnp.concatenate([wts["norm_mix"][i:i + 1], rw["vec"]], axis=0)
            r, k, v, kk, a, w, gate, last_hn = _rwkv_proj(
                h, shift_bufs[j], vec, rw["mu"], rw["w_r"], rw["w_k"], rw["w_v"],
                rw["w1"], rw["w2"], rw["a1"], rw["a2"], rw["g1"], rw["g2"])
            yo, sp_new = _rwkv_scan(r, k, v, kk, a, w, _groups_from_state(rwkv_states[j], RWKV_GROUP),
                                    rw["r_k"], rw["ln_w"], rw["ln_b"])
            new_shift.append(last_hn)
            new_rwkv.append(_state_from_groups(sp_new, RWKV_GROUP))
            h2 = _mix_ffn(yo.reshape(rows, d), gate.reshape(rows, d), h.reshape(rows, d), rw["w_o"],
                          wts["norm_ffn"][i:i + 1], wts["ffn_gate"][i], wts["ffn_up"][i], wts["ffn_down"][i],
                          fn_row, silu_gate=False, name="rwkv_out_ffn")
        h = h2.reshape(b, tp, d)
    return h, jnp.stack(new_conv), jnp.stack(new_gdn), jnp.stack(new_shift), jnp.stack(new_rwkv)


def _prepare_weights(norm_mix, norm_ffn, norm_final, gdn_w_in, gdn_conv_w, gdn_a_log, gdn_dt_bias, gdn_o_norm,
                     gdn_w_out, rwkv_mu, rwkv_w0, rwkv_w1, rwkv_w2, rwkv_a0, rwkv_a1, rwkv_a2, rwkv_g1, rwkv_g2,
                     rwkv_k_k, rwkv_k_a, rwkv_r_k, rwkv_w_r, rwkv_w_k, rwkv_w_v, rwkv_w_o, rwkv_ln_w,
                     rwkv_ln_b, ffn_w_gate, ffn_w_up, ffn_w_down):
    d = norm_mix.shape[1]
    gdn = []
    for j in range(gdn_w_in.shape[0]):
        w = gdn_w_in[j]
        qkvz = w[:, :GDN_QKV + GDN_KDIM]
        a_cols = _pad_cols(w[:, GDN_QKV + GDN_KDIM:GDN_QKV + GDN_KDIM + GDN_HEADS], LANES)
        b_cols = _pad_cols(w[:, GDN_QKV + GDN_KDIM + GDN_HEADS:], LANES)
        gdn.append(dict(
            w_in=jnp.concatenate([qkvz, a_cols, b_cols], axis=1).astype(BF16),
            conv_w=gdn_conv_w[j],
            a_log=_pad_cols(gdn_a_log[j][None], LANES),
            dt_bias=_pad_cols(gdn_dt_bias[j][None], LANES),
            o_norm=gdn_o_norm[j][None],
            w_out=gdn_w_out[j].astype(BF16),
        ))
    rwkv = []
    for j in range(rwkv_mu.shape[0]):
        lw = -(-rwkv_w1.shape[2] // LANES) * LANES
        la = -(-rwkv_a1.shape[2] // LANES) * LANES
        lg = -(-rwkv_g1.shape[2] // LANES) * LANES
        rwkv.append(dict(
            vec=jnp.stack([rwkv_w0[j], rwkv_a0[j], rwkv_k_k[j], rwkv_k_a[j],
                           jnp.zeros_like(rwkv_w0[j]), jnp.zeros_like(rwkv_w0[j]), jnp.zeros_like(rwkv_w0[j])]),
            mu=_pad_rows(rwkv_mu[j], SUBLANES),
            w_r=rwkv_w_r[j].astype(BF16), w_k=rwkv_w_k[j].astype(BF16), w_v=rwkv_w_v[j].astype(BF16),
            w_o=rwkv_w_o[j].astype(BF16),
            w1=_pad_cols(rwkv_w1[j], lw).astype(BF16), w2=_pad_rows(rwkv_w2[j], lw).astype(BF16),
            a1=_pad_cols(rwkv_a1[j], la).astype(BF16), a2=_pad_rows(rwkv_a2[j], la).astype(BF16),
            g1=_pad_cols(rwkv_g1[j], lg).astype(BF16), g2=_pad_rows(rwkv_g2[j], lg).astype(BF16),
            r_k=rwkv_r_k[j].reshape(1, d), ln_w=rwkv_ln_w[j][None], ln_b=rwkv_ln_b[j][None],
        ))
    return dict(norm_mix=norm_mix, norm_ffn=norm_ffn, norm_final=norm_final[None], gdn=gdn, rwkv=rwkv,
                ffn_gate=ffn_w_gate.astype(BF16), ffn_up=ffn_w_up.astype(BF16), ffn_down=ffn_w_down.astype(BF16))


def kernel(x_prompt, x_sample, cache_gdn_conv, state_gdn, cache_rwkv_shift, state_rwkv, meta_tokens, norm_mix, norm_ffn, norm_final, gdn_w_in, gdn_conv_w, gdn_a_log, gdn_dt_bias, gdn_o_norm, gdn_w_out, rwkv_mu, rwkv_w0, rwkv_w1, rwkv_w2, rwkv_a0, rwkv_a1, rwkv_a2, rwkv_g1, rwkv_g2, rwkv_k_k, rwkv_k_a, rwkv_r_k, rwkv_w_r, rwkv_w_k, rwkv_w_v, rwkv_w_o, rwkv_ln_w, rwkv_ln_b, ffn_w_gate, ffn_w_up, ffn_w_down):
    wts = _prepare_weights(norm_mix, norm_ffn, norm_final, gdn_w_in, gdn_conv_w, gdn_a_log, gdn_dt_bias,
                           gdn_o_norm, gdn_w_out, rwkv_mu, rwkv_w0, rwkv_w1, rwkv_w2, rwkv_a0, rwkv_a1, rwkv_a2,
                           rwkv_g1, rwkv_g2, rwkv_k_k, rwkv_k_a, rwkv_r_k, rwkv_w_r, rwkv_w_k, rwkv_w_v,
                           rwkv_w_o, rwkv_ln_w, rwkv_ln_b, ffn_w_gate, ffn_w_up, ffn_w_down)
    n_gdn = gdn_w_in.shape[0]
    n_rwkv = rwkv_mu.shape[0]
    b, seq, d = x_prompt.shape
    dt = x_prompt.dtype

    lead_pad = (-N_META) % CHUNK
    lead = jnp.concatenate([jnp.zeros((b, lead_pad, d), dt),
                            jnp.broadcast_to(meta_tokens.astype(dt)[None], (b, N_META, d))], axis=1)
    zeros_like_states = (jnp.zeros((n_gdn, b, CONV_W - 1, GDN_QKV), dt),
                         jnp.zeros((n_gdn, b, GDN_HEADS, GDN_DK, GDN_DV), dt),
                         jnp.zeros((n_rwkv, b, 1, d), dt),
                         jnp.zeros((n_rwkv, b, d // RWKV_N, RWKV_N, RWKV_N), dt))
    sample_states = (cache_gdn_conv, state_gdn, cache_rwkv_shift, state_rwkv)

    bs = x_sample.shape[0]
    if x_sample.shape[1:] == lead.shape[1:]:
        both = _run_trunk(jnp.concatenate([x_sample, lead], axis=0),
                          *[jnp.concatenate([s, z], axis=1) for s, z in zip(sample_states, zeros_like_states)],
                          wts)
        y_sample, s_conv, s_gdn, s_shift, s_rwkv = [both[0][:bs]] + [t[:, :bs] for t in both[1:]]
        lead_states = [t[:, bs:] for t in both[1:]]
    else:
        y_sample, s_conv, s_gdn, s_shift, s_rwkv = _run_trunk(x_sample, *sample_states, wts)
        lead_states = _run_trunk(lead, *zeros_like_states, wts)[1:]

    y_prompt, p_conv, p_gdn, p_shift, p_rwkv = _run_trunk(x_prompt, *lead_states, wts)

    return (y_prompt, y_sample, p_conv, p_gdn, p_shift, p_rwkv, s_conv, s_gdn, s_shift, s_rwkv)
```

```python
import functools

import jax
import jax.numpy as jnp
from jax import lax
from jax.experimental import pallas as pl
from jax.experimental.pallas import tpu as pltpu

F32 = jnp.float32
BF16 = jnp.bfloat16

CHUNK = 64
N_META = 16
EPS = 1e-6
L2_EPS = 1e-6
GN_EPS = 64e-5

GDN_HEADS = 8
GDN_DK = 128
GDN_DV = 128
GDN_KDIM = GDN_HEADS * GDN_DK
GDN_QKV = 3 * GDN_KDIM
CONV_W = 4
RWKV_N = 64
RWKV_GROUP = 4
DECAY_SCALE = 0.6065306597126334

LANES = 128
SUBLANES = 8
VMEM_LIMIT = 56 * 1024 * 1024


def _dot(a, b):
    return jnp.dot(a.astype(BF16), b.astype(BF16), preferred_element_type=F32)


def _bdot(a, b):
    return lax.dot_general(a.astype(BF16), b.astype(BF16), (((2,), (1,)), ((0,), (0,))),
                           preferred_element_type=F32)


def _bdot_nt(a, b):
    return lax.dot_general(a.astype(BF16), b.astype(BF16), (((2,), (2,)), ((0,), (0,))),
                           preferred_element_type=F32)


def _bdot_tn(a, b):
    return lax.dot_general(a.astype(BF16), b.astype(BF16), (((1,), (1,)), ((0,), (0,))),
                           preferred_element_type=F32)


def _split_bf16(x, parts):
    out = []
    for _ in range(parts - 1):
        hi = x.astype(BF16)
        out.append(hi)
        x = x - hi.astype(F32)
    out.append(x.astype(BF16))
    return out


def _dot_exact_lhs(m01, x, parts=3):
    return sum(_dot(m01, t) for t in _split_bf16(x, parts))


def _sigmoid(x):
    return 0.5 + 0.5 * jnp.tanh(0.5 * x)


def _silu(x):
    h = 0.5 * x
    return h + h * jnp.tanh(h)


def _rms(x, g, eps=EPS):
    return x * lax.rsqrt(jnp.mean(x * x, axis=-1, keepdims=True) + eps) * g


def _neumann_cat(x, eye, size, expand):
    rows = x.shape[1]
    t = eye + x
    p = _bdot(x, expand(x))
    n = 4
    while n < size:
        both = _bdot(jnp.concatenate([p, t], axis=1), expand(p))
        t = t + both[:, rows:]
        p = both[:, :rows]
        n *= 2
    return t + _bdot(t, expand(p))


def _pick_tile(n, candidates):
    for c in candidates:
        if n % c == 0:
            return c
    raise ValueError(f"no tile for {n}")


def _params(grid_rank):
    return pltpu.CompilerParams(dimension_semantics=("arbitrary",) * grid_rank, vmem_limit_bytes=VMEM_LIMIT)


def _const_spec(shape):
    nd = len(shape)
    return pl.BlockSpec(shape, lambda *_: (0,) * nd, pipeline_mode=pl.Buffered(1))


def _gdn_proj_kernel(x_ref, g_ref, w_ref, cw_ref, cbuf_ref, q_o, k_o, v_o, z_o, ab_o, cst_o, xs_ref):
    t = pl.program_id(1)
    tt = x_ref.shape[0]
    hist = SUBLANES

    @pl.when(t == 0)
    def _():
        xs_ref[...] = cbuf_ref[...]

    xn = _rms(x_ref[...], g_ref[...]).astype(BF16)

    for out, c0 in ((q_o, 0), (k_o, GDN_KDIM), (v_o, 2 * GDN_KDIM)):
        cols = slice(c0, c0 + GDN_KDIM)
        proj = _dot(xn, w_ref[:, cols])
        win = jnp.concatenate([xs_ref[:, cols], proj], axis=0)
        xs_ref[:, cols] = proj[tt - hist:tt]
        cst_o[:, cols] = proj[tt - hist:tt]
        win1 = pltpu.roll(win, 1, 0)
        near = cw_ref[3:4, cols] * win + cw_ref[2:3, cols] * win1
        far = cw_ref[1:2, cols] * win + cw_ref[0:1, cols] * win1
        out[...] = (near + pltpu.roll(far, 2, 0))[hist:].astype(out.dtype)
    z_o[...] = _dot(xn, w_ref[:, GDN_QKV:GDN_QKV + GDN_KDIM]).astype(z_o.dtype)
    ab_o[...] = _dot(xn, w_ref[:, GDN_QKV + GDN_KDIM:])


def _gdn_proj(h3, g_row, w_bf16, conv_w, cbuf8):
    b, tp, d = h3.shape
    n = w_bf16.shape[1]
    tt = _pick_tile(tp, (256, 192, 128, 64))
    tok = lambda width: pl.BlockSpec((None, tt, width), lambda i, t: (i, t, 0))
    hist_spec = pl.BlockSpec((None, SUBLANES, GDN_QKV), lambda i, t: (i, 0, 0))
    return pl.pallas_call(
        _gdn_proj_kernel,
        grid=(b, tp // tt),
        in_specs=[tok(d), _const_spec((1, d)), _const_spec((d, n)), _const_spec((CONV_W, GDN_QKV)), hist_spec],
        out_specs=[tok(GDN_KDIM), tok(GDN_KDIM), tok(GDN_KDIM), tok(GDN_KDIM), tok(2 * LANES), hist_spec],
        out_shape=[jax.ShapeDtypeStruct((b, tp, GDN_KDIM), BF16)] * 4
        + [jax.ShapeDtypeStruct((b, tp, 2 * LANES), F32), jax.ShapeDtypeStruct((b, SUBLANES, GDN_QKV), F32)],
        scratch_shapes=[pltpu.VMEM((SUBLANES, GDN_QKV), F32)],
        compiler_params=_params(2),
        name="gdn_in_proj",
    )(h3, g_row, w_bf16, conv_w, cbuf8)


def _gdn_scan_kernel(q_ref, k_ref, v_ref, ab_ref, s0_ref, alog_ref, dtb_ref, onorm_ref, o_ref, s_ref):
    c = pl.program_id(1)
    L = CHUNK
    D = GDN_DK
    HP = GDN_HEADS // 2

    @pl.when(c == 0)
    def _():
        s_ref[...] = s0_ref[...]

    row = lax.broadcasted_iota(jnp.int32, (L, L), 0)
    col = lax.broadcasted_iota(jnp.int32, (L, L), 1)
    tril = (row >= col).astype(BF16)
    t_idx = lax.broadcasted_iota(jnp.int32, (L, 2 * L), 0)
    lane = lax.broadcasted_iota(jnp.int32, (L, 2 * L), 1)
    first = lane < L
    s_idx = lane % L
    causal_c = t_idx >= s_idx
    strict_c = t_idx > s_idx
    eye_c = (t_idx == s_idx).astype(F32)

    nb = q_ref.shape[0]
    a_in = jnp.concatenate([ab_ref[i, :, :LANES] + dtb_ref[...] for i in range(nb)], axis=1)
    b_in = jnp.concatenate([ab_ref[i, :, LANES:] for i in range(nb)], axis=1)
    neg_a = jnp.concatenate([-jnp.exp(alog_ref[...])] * nb, axis=1)
    g = neg_a * jax.nn.softplus(a_in)
    beta = _sigmoid(b_in)
    G = _dot_exact_lhs(tril, g)
    Gt = G.T

    pairs = [(i, p) for i in range(nb) for p in range(HP)]

    def per_pair(ref):
        return jnp.stack([ref[i, :, 2 * p * D:2 * (p + 1) * D] for i, p in pairs])

    def halves(x):
        return x[..., :D], x[..., D:]

    def per_half(fn, x):
        xa, xb = halves(x)
        return jnp.concatenate([fn(xa), fn(xb)], axis=-1)

    def head_cols(x2d):
        return [jnp.stack([x2d[:, i * LANES + 2 * p + j:i * LANES + 2 * p + j + 1] for i, p in pairs])
                for j in range(2)]

    def wide(cols):
        return jnp.concatenate([jnp.broadcast_to(c, c.shape[:2] + (D,)) for c in cols], axis=-1)

    def cat(cols):
        return jnp.where(first, cols[0], cols[1])

    def expand(x):
        xa, xb = halves(x.astype(BF16))
        zero = jnp.zeros_like(xa)
        return jnp.concatenate([jnp.concatenate([xa, zero], axis=-1),
                                jnp.concatenate([zero, xb], axis=-1)], axis=1)

    def expand_cat(x):
        xb = x.astype(BF16)
        zero = jnp.zeros_like(xb)
        return jnp.concatenate([jnp.where(first, xb, zero), jnp.where(first, zero, xb)], axis=1)

    def l2n(x):
        return x * lax.rsqrt(jnp.sum(x * x, axis=-1, keepdims=True) + L2_EPS)

    q = per_half(l2n, _silu(per_pair(q_ref).astype(F32))) * (GDN_DK ** -0.5)
    kf = per_half(l2n, _silu(per_pair(k_ref).astype(F32)))
    v = _silu(per_pair(v_ref).astype(F32))
    k = kf.astype(BF16)

    b_cols = head_cols(beta)
    g_cols = head_cols(G)
    g_last = [c[:, L - 1:L, :] for c in g_cols]
    g_row = jnp.stack([jnp.concatenate([Gt[i * LANES + 2 * p + j:i * LANES + 2 * p + j + 1, :] for j in range(2)],
                                       axis=1) for i, p in pairs])
    b_wide = wide(b_cols)
    eg_wide = wide([jnp.exp(c) for c in g_cols])
    egr_wide = wide([jnp.exp(gl - c) for gl, c in zip(g_last, g_cols)])
    egl_wide = wide([jnp.exp(gl) for gl in g_last])

    dlt = cat(g_cols) - g_row
    dec = jnp.where(causal_c, jnp.exp(jnp.where(causal_c, dlt, 0.0)), 0.0)
    kq_k = _bdot_nt(jnp.concatenate([k, q.astype(BF16)], axis=1), expand(k))
    a_mat = jnp.where(strict_c, cat(b_cols) * kq_k[:, :L] * dec, 0.0)
    qk = kq_k[:, L:] * dec
    t_inv = _neumann_cat(-a_mat, eye_c, L, expand_cat)

    s = s_ref[...].reshape(len(pairs), D, 2 * D)
    ks_qs = _bdot(jnp.concatenate([k, (q * eg_wide).astype(BF16)], axis=1), expand(s))
    v_new = _bdot(t_inv, expand(b_wide * (v - eg_wide * ks_qs[:, :L])))
    o = ks_qs[:, L:] + _bdot(qk, expand(v_new))
    cross = _bdot_tn(kf * egr_wide, v_new)
    s_ref[...] = (s * egl_wide + jnp.concatenate([cross[:, :D, :D], cross[:, D:, D:]], axis=-1)).reshape(s_ref.shape)

    def o_norm(x):
        return x * lax.rsqrt(jnp.mean(x * x, axis=-1, keepdims=True) + EPS) * onorm_ref[...]

    o = per_half(o_norm, o)
    for n, (i, p) in enumerate(pairs):
        o_ref[i, :, 2 * p * D:2 * (p + 1) * D] = o[n].astype(o_ref.dtype)


def _scan_batch(b, most=4):
    return next(n for n in (8, 4, 2, 1) if n <= most and b % n == 0)


def _gdn_scan(q, k, v, ab, s0, alog_row, dtb_row, onorm_row):
    b, tp, _ = q.shape
    nb = _scan_batch(b)
    tok = pl.BlockSpec((nb, CHUNK, GDN_KDIM), lambda i, c: (i, c, 0))
    st = pl.BlockSpec((nb, GDN_HEADS // 2, GDN_DK, 2 * GDN_DV), lambda i, c: (i, 0, 0, 0))
    return pl.pallas_call(
        _gdn_scan_kernel,
        grid=(b // nb, tp // CHUNK),
        in_specs=[tok, tok, tok, pl.BlockSpec((nb, CHUNK, 2 * LANES), lambda i, c: (i, c, 0)), st,
                  _const_spec((1, LANES)), _const_spec((1, LANES)), _const_spec((1, GDN_DV))],
        out_specs=[tok, st],
        out_shape=[jax.ShapeDtypeStruct((b, tp, GDN_KDIM), BF16),
                   jax.ShapeDtypeStruct((b, GDN_HEADS // 2, GDN_DK, 2 * GDN_DV), F32)],
        compiler_params=_params(2),
        name="gdn_scan",
    )(q, k, v, ab, s0, alog_row, dtb_row, onorm_row)


def _mix_ffn_kernel(*refs, silu_gate, final_norm):
    if final_norm:
        y_ref, gm_ref, h_ref, wo_ref, nf_ref, wg_ref, wu_ref, wd_ref, fn_ref, o_ref = refs
    else:
        y_ref, gm_ref, h_ref, wo_ref, nf_ref, wg_ref, wu_ref, wd_ref, o_ref = refs
    gm = gm_ref[...].astype(F32)
    if silu_gate:
        gm = _silu(gm)
    h1 = h_ref[...] + _dot(y_ref[...].astype(F32) * gm, wo_ref[...])
    n = _rms(h1, nf_ref[...])
    act = _silu(_dot(n, wg_ref[...])) * _dot(n, wu_ref[...])
    h2 = h1 + _dot(act, wd_ref[...])
    if final_norm:
        h2 = _rms(h2, fn_ref[...])
    o_ref[...] = h2


def _mix_ffn(y2d, gm2d, h2d, wo, nf_row, wg, wu, wd, fn_row, *, silu_gate, name):
    rows, d = h2d.shape
    dff = wg.shape[1]
    tm = _pick_tile(rows, (512, 256, 128, 64))
    final_norm = fn_row is not None
    tok = pl.BlockSpec((tm, d), lambda i: (i, 0))
    in_specs = [tok, tok, tok, _const_spec((d, d)), _const_spec((1, d)),
                _const_spec((d, dff)), _const_spec((d, dff)), _const_spec((dff, d))]
    args = [y2d, gm2d, h2d, wo, nf_row, wg, wu, wd]
    if final_norm:
        in_specs.append(_const_spec((1, d)))
        args.append(fn_row)
    return pl.pallas_call(
        functools.partial(_mix_ffn_kernel, silu_gate=silu_gate, final_norm=final_norm),
        grid=(rows // tm,),
        in_specs=in_specs,
        out_specs=tok,
        out_shape=jax.ShapeDtypeStruct((rows, d), F32),
        compiler_params=_params(1),
        name=name,
    )(*args)


def _rwkv_proj_kernel(h_ref, sb_ref, vec_ref, mu_ref, wr_ref, wk_ref, wv_ref,
                      w1_ref, w2_ref, a1_ref, a2_ref, g1_ref, g2_ref,
                      r_o, k_o, v_o, kk_o, a_o, w_o, g_o, last_o, hs_ref):
    t = pl.program_id(1)
    tt = h_ref.shape[0]
    hist = SUBLANES
    hn = _rms(h_ref[...], vec_ref[0:1, :])

    @pl.when(t == 0)
    def _():
        hs_ref[...] = jnp.broadcast_to(sb_ref[...], hs_ref.shape)

    prev = pltpu.roll(jnp.concatenate([hs_ref[...], hn], axis=0), 1, 0)[hist:]
    hs_ref[hist - 1:hist, :] = hn[tt - 1:tt, :]
    last_o[...] = hn[tt - 1:tt, :]
    xx = prev - hn

    def mix(i):
        return (hn + xx * mu_ref[i:i + 1, :]).astype(BF16)

    r = _dot(mix(0), wr_ref[...])
    k = _dot(mix(2), wk_ref[...])
    v = _dot(mix(3), wv_ref[...])
    wl = vec_ref[1:2, :] + _dot(jnp.tanh(_dot(mix(1), w1_ref[...])), w2_ref[...])
    a = _sigmoid(vec_ref[2:3, :] + _dot(_dot(mix(4), a1_ref[...]), a2_ref[...]))
    gate = _dot(_sigmoid(_dot(mix(5), g1_ref[...])), g2_ref[...])
    r_o[...] = r.astype(r_o.dtype)
    k_o[...] = (k * (1.0 + (a - 1.0) * vec_ref[4:5, :])).astype(k_o.dtype)
    v_o[...] = v.astype(v_o.dtype)
    kk_o[...] = (k * vec_ref[3:4, :]).astype(kk_o.dtype)
    a_o[...] = a.astype(a_o.dtype)
    w_o[...] = -DECAY_SCALE * _sigmoid(wl)
    g_o[...] = gate.astype(g_o.dtype)


def _rwkv_proj(h3, sbuf, vec, mu8, wr, wk, wv, w1, w2, a1, a2, g1, g2):
    b, tp, d = h3.shape
    tt = _pick_tile(tp, (512, 256, 192, 128, 64))
    tok = pl.BlockSpec((None, tt, d), lambda i, t: (i, t, 0))
    row = pl.BlockSpec((None, 1, d), lambda i, t: (i, 0, 0))
    weights = [wr, wk, wv, w1, w2, a1, a2, g1, g2]
    tok_shape = lambda dt: jax.ShapeDtypeStruct((b, tp, d), dt)
    return pl.pallas_call(
        _rwkv_proj_kernel,
        grid=(b, tp // tt),
        in_specs=[tok, row, _const_spec(vec.shape), _const_spec(mu8.shape)] + [_const_spec(w.shape) for w in weights],
        out_specs=[tok] * 7 + [row],
        out_shape=[tok_shape(BF16)] * 5 + [tok_shape(F32), tok_shape(BF16), jax.ShapeDtypeStruct((b, 1, d), F32)],
        scratch_shapes=[pltpu.VMEM((SUBLANES, d), F32)],
        compiler_params=_params(2),
        name="rwkv_proj",
    )(h3, sbuf, vec, mu8, *weights)


def _rwkv_scan_kernel(r_ref, k_ref, v_ref, kk_ref, a_ref, w_ref, s0_ref, rk_ref, lnw_ref, lnb_ref,
                      yo_ref, s_ref):
    c = pl.program_id(1)
    L = CHUNK
    N = RWKV_N
    G = RWKV_GROUP
    Q = G * N
    nb, _, d = r_ref.shape
    groups = range(nb * d // Q)

    def cat_batch(ref):
        return jnp.concatenate([ref[i] for i in range(nb)], axis=1)

    @pl.when(c == 0)
    def _():
        s_ref[...] = s0_ref[...]

    row = lax.broadcasted_iota(jnp.int32, (L, L), 0)
    col = lax.broadcasted_iota(jnp.int32, (L, L), 1)
    tril = (row >= col).astype(BF16)
    t_idx = lax.broadcasted_iota(jnp.int32, (L, Q), 0)
    lane = lax.broadcasted_iota(jnp.int32, (L, Q), 1)
    s_idx = lane % N
    strict_c = t_idx > s_idx
    incl_c = t_idx >= s_idx
    eye_c = (t_idx == s_idx).astype(F32)
    ones_blk = (lax.broadcasted_iota(jnp.int32, (Q, Q), 0) // N
                == lax.broadcasted_iota(jnp.int32, (Q, Q), 1) // N).astype(BF16)
    head_lanes = [lax.broadcasted_iota(jnp.int32, (N, Q), 1) // N == g for g in range(G)]

    def per_group(x2d):
        return jnp.stack([x2d[:, q * Q:(q + 1) * Q] for q in groups])

    def expand(x3):
        xb = x3.astype(BF16)
        zero = jnp.zeros(xb.shape[:2] + (LANES,), BF16)
        slot_of_lane = lax.broadcasted_iota(jnp.int32, xb.shape[1:2] + (LANES,), 1) // N
        blocks = []
        for g in range(G):
            col, slot = divmod(g * N, LANES)
            piece = jnp.where(slot_of_lane == slot // N, xb[:, :, col * LANES:(col + 1) * LANES], zero)
            blocks.append(jnp.concatenate([piece if c == col else zero for c in range(Q // LANES)], axis=2))
        return jnp.concatenate(blocks, axis=1)

    def diag_blocks(x3):
        return sum(jnp.where(head_lanes[g], x3[:, g * N:(g + 1) * N, :], 0.0) for g in range(G))

    def lane_sums(x3):
        n, rws, _ = x3.shape
        return _dot(x3.reshape(n * rws, Q), ones_blk).reshape(n, rws, Q)

    w_all = cat_batch(w_ref)
    cw_all = _dot_exact_lhs(tril, w_all)
    cwl = cw_all[L - 1:L, :]
    p_in = per_group(jnp.exp(cw_all))
    p_ex = per_group(jnp.exp(cw_all - w_all))
    p_inv = per_group(jnp.exp(-cw_all))
    p_rem = per_group(jnp.exp(cwl - cw_all))
    dec_all = jnp.exp(cwl)
    p_all = jnp.stack([dec_all[:, q * Q:(q + 1) * Q] for q in groups])

    kkr = per_group(cat_batch(kk_ref).astype(F32))
    kkn = kkr * lax.rsqrt(lane_sums(kkr * kkr) + L2_EPS)
    rp = per_group(cat_batch(r_ref).astype(F32))
    kp = per_group(cat_batch(k_ref).astype(F32))
    vp = per_group(cat_batch(v_ref).astype(F32))
    bb = kkn * per_group(cat_batch(a_ref).astype(F32))

    lhs = jnp.concatenate([-kkn * p_ex, rp * p_in], axis=1).astype(BF16)
    m = _bdot_nt(lhs, jnp.concatenate([expand(bb * p_inv), expand(kp * p_inv)], axis=1))
    l_ab = jnp.where(strict_c, m[:, :L, :Q], 0.0)
    l_ak = jnp.where(strict_c, m[:, :L, Q:], 0.0)
    m_rb = jnp.where(incl_c, m[:, L:, :Q], 0.0)
    m_rk = jnp.where(incl_c, m[:, L:, Q:], 0.0)

    s = s_ref[...].reshape(len(groups), N, Q)
    from_s = _bdot_nt(lhs, expand(s))

    t_inv = _neumann_cat(l_ab, eye_c, L, expand)

    from_v = _bdot(jnp.concatenate([l_ak, m_rk], axis=1), expand(vp))
    u = _bdot(t_inv, expand(from_s[:, :L] + from_v[:, :L]))
    y = from_s[:, L:] + from_v[:, L:] + _bdot(m_rb, expand(u))
    cross = _bdot_tn(jnp.concatenate([u, vp], axis=1), jnp.concatenate([bb * p_rem, kp * p_rem], axis=1))
    s_ref[...] = (s * p_all + diag_blocks(cross)).reshape(s_ref.shape)

    mean = lane_sums(y) * (1.0 / N)
    cy = y - mean
    var = lane_sums(cy * cy) * (1.0 / N)
    rk3 = per_group(jnp.concatenate([rk_ref[...]] * nb, axis=1))
    bonus = lane_sums(rp * kp * rk3) * vp
    for g in groups:
        i, q = divmod(g, d // Q)
        sl = slice(q * Q, (q + 1) * Q)
        yn = cy[g] * lax.rsqrt(var[g] + GN_EPS) * lnw_ref[:, sl] + lnb_ref[:, sl]
        yo_ref[i, :, sl] = (yn + bonus[g]).astype(yo_ref.dtype)


def _rwkv_scan(r, k, v, kk, a, w, s0_groups, rk_row, lnw_row, lnb_row):
    b, tp, d = r.shape
    q = RWKV_GROUP * RWKV_N
    nb = _scan_batch(b, most=8)
    tok = pl.BlockSpec((nb, CHUNK, d), lambda i, c: (i, c, 0))
    st = pl.BlockSpec((nb, d // q, RWKV_N, q), lambda i, c: (i, 0, 0, 0))
    return pl.pallas_call(
        _rwkv_scan_kernel,
        grid=(b // nb, tp // CHUNK),
        in_specs=[tok] * 6 + [st, _const_spec((1, d)), _const_spec((1, d)), _const_spec((1, d))],
        out_specs=[tok, st],
        out_shape=[jax.ShapeDtypeStruct((b, tp, d), BF16),
                   jax.ShapeDtypeStruct((b, d // q, RWKV_N, q), F32)],
        compiler_params=_params(2),
        name="rwkv_scan",
    )(r, k, v, kk, a, w, s0_groups, rk_row, lnw_row, lnb_row)


def _pad_cols(w, n):
    return jnp.pad(w, ((0, 0), (0, n - w.shape[1])))


def _pad_rows(w, n):
    return jnp.pad(w, ((0, n - w.shape[0]), (0, 0)))


def _groups_from_state(s, g):
    b, h, r, c = s.shape
    return jnp.swapaxes(s.reshape(b, h // g, g, r, c), 2, 3).reshape(b, h // g, r, g * c)


def _state_from_groups(sg, g):
    b, hq, r, gc = sg.shape
    return jnp.swapaxes(sg.reshape(b, hq, r, g, gc // g), 2, 3).reshape(b, hq * g, r, gc // g)


def _run_trunk(h, conv_bufs, gdn_states, shift_bufs, rwkv_states, wts):
    b, tp, d = h.shape
    rows = b * tp
    depth = wts["norm_mix"].shape[0]
    new_conv, new_gdn, new_shift, new_rwkv = [], [], [], []
    for i in range(depth):
        j = i // 2
        fn_row = wts["norm_final"] if i == depth - 1 else None
        if i % 2 == 0:
            g = wts["gdn"][j]
            cbuf8 = jnp.pad(conv_bufs[j], ((0, 0), (SUBLANES - (CONV_W - 1), 0), (0, 0)))
            q, k, v, z, ab, cst = _gdn_proj(h, wts["norm_mix"][i:i + 1], g["w_in"], g["conv_w"], cbuf8)
            o, s_new = _gdn_scan(q, k, v, ab, _groups_from_state(gdn_states[j], 2), g["a_log"], g["dt_bias"],
                                 g["o_norm"])
            new_conv.append(cst[:, SUBLANES - (CONV_W - 1):])
            new_gdn.append(_state_from_groups(s_new, 2))
            h2 = _mix_ffn(o.reshape(rows, d), z.reshape(rows, d), h.reshape(rows, d), g["w_out"],
                          wts["norm_ffn"][i:i + 1], wts["ffn_gate"][i], wts["ffn_up"][i], wts["ffn_down"][i],
                          fn_row, silu_gate=True, name="gdn_out_ffn")
        else:
            rw = wts["rwkv"][j]
            vec = jnp.concatenate([wts["norm_mix"][i:i + 1], rw["vec"]], axis=0)
            r, k, v, kk, a, w, gate, last_hn = _rwkv_proj(
                h, shift_bufs[j], vec, rw["mu"], rw["w_r"], rw["w_k"], rw["w_v"],
                rw["w1"], rw["w2"], rw["a1"], rw["a2"], rw["g1"], rw["g2"])
            yo, sp_new = _rwkv_scan(r, k, v, kk, a, w, _groups_from_state(rwkv_states[j], RWKV_GROUP),
                                    rw["r_k"], rw["ln_w"], rw["ln_b"])
            new_shift.append(last_hn)
            new_rwkv.append(_state_from_groups(sp_new, RWKV_GROUP))
            h2 = _mix_ffn(yo.reshape(rows, d), gate.reshape(rows, d), h.reshape(rows, d), rw["w_o"],
                          wts["norm_ffn"][i:i + 1], wts["ffn_gate"][i], wts["ffn_up"][i], wts["ffn_down"][i],
                          fn_row, silu_gate=False, name="rwkv_out_ffn")
        h = h2.reshape(b, tp, d)
    return h, jnp.stack(new_conv), jnp.stack(new_gdn), jnp.stack(new_shift), jnp.stack(new_rwkv)


def _prepare_weights(norm_mix, norm_ffn, norm_final, gdn_w_in, gdn_conv_w, gdn_a_log, gdn_dt_bias, gdn_o_norm,
                     gdn_w_out, rwkv_mu, rwkv_w0, rwkv_w1, rwkv_w2, rwkv_a0, rwkv_a1, rwkv_a2, rwkv_g1, rwkv_g2,
                     rwkv_k_k, rwkv_k_a, rwkv_r_k, rwkv_w_r, rwkv_w_k, rwkv_w_v, rwkv_w_o, rwkv_ln_w,
                     rwkv_ln_b, ffn_w_gate, ffn_w_up, ffn_w_down):
    d = norm_mix.shape[1]
    gdn = []
    for j in range(gdn_w_in.shape[0]):
        w = gdn_w_in[j]
        qkvz = w[:, :GDN_QKV + GDN_KDIM]
        a_cols = _pad_cols(w[:, GDN_QKV + GDN_KDIM:GDN_QKV + GDN_KDIM + GDN_HEADS], LANES)
        b_cols = _pad_cols(w[:, GDN_QKV + GDN_KDIM + GDN_HEADS:], LANES)
        gdn.append(dict(
            w_in=jnp.concatenate([qkvz, a_cols, b_cols], axis=1).astype(BF16),
            conv_w=gdn_conv_w[j],
            a_log=_pad_cols(gdn_a_log[j][None], LANES),
            dt_bias=_pad_cols(gdn_dt_bias[j][None], LANES),
            o_norm=gdn_o_norm[j][None],
            w_out=gdn_w_out[j].astype(BF16),
        ))
    rwkv = []
    for j in range(rwkv_mu.shape[0]):
        lw = -(-rwkv_w1.shape[2] // LANES) * LANES
        la = -(-rwkv_a1.shape[2] // LANES) * LANES
        lg = -(-rwkv_g1.shape[2] // LANES) * LANES
        rwkv.append(dict(
            vec=jnp.stack([rwkv_w0[j], rwkv_a0[j], rwkv_k_k[j], rwkv_k_a[j],
                           jnp.zeros_like(rwkv_w0[j]), jnp.zeros_like(rwkv_w0[j]), jnp.zeros_like(rwkv_w0[j])]),
            mu=_pad_rows(rwkv_mu[j], SUBLANES),
            w_r=rwkv_w_r[j].astype(BF16), w_k=rwkv_w_k[j].astype(BF16), w_v=rwkv_w_v[j].astype(BF16),
            w_o=rwkv_w_o[j].astype(BF16),
            w1=_pad_cols(rwkv_w1[j], lw).astype(BF16), w2=_pad_rows(rwkv_w2[j], lw).astype(BF16),
            a1=_pad_cols(rwkv_a1[j], la).astype(BF16), a2=_pad_rows(rwkv_a2[j], la).astype(BF16),
            g1=_pad_cols(rwkv_g1[j], lg).astype(BF16), g2=_pad_rows(rwkv_g2[j], lg).astype(BF16),
            r_k=rwkv_r_k[j].reshape(1, d), ln_w=rwkv_ln_w[j][None], ln_b=rwkv_ln_b[j][None],
        ))
    return dict(norm_mix=norm_mix, norm_ffn=norm_ffn, norm_final=norm_final[None], gdn=gdn, rwkv=rwkv,
                ffn_gate=ffn_w_gate.astype(BF16), ffn_up=ffn_w_up.astype(BF16), ffn_down=ffn_w_down.astype(BF16))


def kernel(x_prompt, x_sample, cache_gdn_conv, state_gdn, cache_rwkv_shift, state_rwkv, meta_tokens, norm_mix, norm_ffn, norm_final, gdn_w_in, gdn_conv_w, gdn_a_log, gdn_dt_bias, gdn_o_norm, gdn_w_out, rwkv_mu, rwkv_w0, rwkv_w1, rwkv_w2, rwkv_a0, rwkv_a1, rwkv_a2, rwkv_g1, rwkv_g2, rwkv_k_k, rwkv_k_a, rwkv_r_k, rwkv_w_r, rwkv_w_k, rwkv_w_v, rwkv_w_o, rwkv_ln_w, rwkv_ln_b, ffn_w_gate, ffn_w_up, ffn_w_down):
    wts = _prepare_weights(norm_mix, norm_ffn, norm_final, gdn_w_in, gdn_conv_w, gdn_a_log, gdn_dt_bias,
                           gdn_o_norm, gdn_w_out, rwkv_mu, rwkv_w0, rwkv_w1, rwkv_w2, rwkv_a0, rwkv_a1, rwkv_a2,
                           rwkv_g1, rwkv_g2, rwkv_k_k, rwkv_k_a, rwkv_r_k, rwkv_w_r, rwkv_w_k, rwkv_w_v,
                           rwkv_w_o, rwkv_ln_w, rwkv_ln_b, ffn_w_gate, ffn_w_up, ffn_w_down)
    n_gdn = gdn_w_in.shape[0]
    n_rwkv = rwkv_mu.shape[0]
    b, seq, d = x_prompt.shape
    dt = x_prompt.dtype

    lead_pad = (-N_META) % CHUNK
    lead = jnp.concatenate([jnp.zeros((b, lead_pad, d), dt),
                            jnp.broadcast_to(meta_tokens.astype(dt)[None], (b, N_META, d))], axis=1)
    zeros_like_states = (jnp.zeros((n_gdn, b, CONV_W - 1, GDN_QKV), dt),
                         jnp.zeros((n_gdn, b, GDN_HEADS, GDN_DK, GDN_DV), dt),
                         jnp.zeros((n_rwkv, b, 1, d), dt),
                         jnp.zeros((n_rwkv, b, d // RWKV_N, RWKV_N, RWKV_N), dt))
    sample_states = (cache_gdn_conv, state_gdn, cache_rwkv_shift, state_rwkv)

    bs = x_sample.shape[0]
    if x_sample.shape[1:] == lead.shape[1:]:
        both = _run_trunk(jnp.concatenate([x_sample, lead], axis=0),
                          *[jnp.concatenate([s, z], axis=1) for s, z in zip(sample_states, zeros_like_states)],
                          wts)
        y_sample, s_conv, s_gdn, s_shift, s_rwkv = [both[0][:bs]] + [t[:, :bs] for t in both[1:]]
        lead_states = [t[:, bs:] for t in both[1:]]
    else:
        y_sample, s_conv, s_gdn, s_shift, s_rwkv = _run_trunk(x_sample, *sample_states, wts)
        lead_states = _run_trunk(lead, *zeros_like_states, wts)[1:]

    y_prompt, p_conv, p_gdn, p_shift, p_rwkv = _run_trunk(x_prompt, *lead_states, wts)

    return (y_prompt, y_sample, p_conv, p_gdn, p_shift, p_rwkv, s_conv, s_gdn, s_shift, s_rwkv)
```

```python
import functools

import jax
import jax.numpy as jnp
from jax import lax
from jax.experimental import pallas as pl
from jax.experimental.pallas import tpu as pltpu

F32 = jnp.float32
BF16 = jnp.bfloat16

CHUNK = 64
N_META = 16
EPS = 1e-6
L2_EPS = 1e-6
GN_EPS = 64e-5

GDN_HEADS = 8
GDN_DK = 128
GDN_DV = 128
GDN_KDIM = GDN_HEADS * GDN_DK
GDN_QKV = 3 * GDN_KDIM
CONV_W = 4
RWKV_N = 64
RWKV_GROUP = 4
DECAY_SCALE = 0.6065306597126334

LANES = 128
SUBLANES = 8
VMEM_LIMIT = 56 * 1024 * 1024


def _dot(a, b):
    return jnp.dot(a.astype(BF16), b.astype(BF16), preferred_element_type=F32)


def _bdot(a, b):
    return lax.dot_general(a.astype(BF16), b.astype(BF16), (((2,), (1,)), ((0,), (0,))),
                           preferred_element_type=F32)


def _bdot_nt(a, b):
    return lax.dot_general(a.astype(BF16), b.astype(BF16), (((2,), (2,)), ((0,), (0,))),
                           preferred_element_type=F32)


def _bdot_tn(a, b):
    return lax.dot_general(a.astype(BF16), b.astype(BF16), (((1,), (1,)), ((0,), (0,))),
                           preferred_element_type=F32)


def _split_bf16(x, parts):
    out = []
    for _ in range(parts - 1):
        hi = x.astype(BF16)
        out.append(hi)
        x = x - hi.astype(F32)
    out.append(x.astype(BF16))
    return out


def _dot_exact_lhs(m01, x, parts=3):
    return sum(_dot(m01, t) for t in _split_bf16(x, parts))


def _sigmoid(x):
    return 0.5 + 0.5 * jnp.tanh(0.5 * x)


def _silu(x):
    h = 0.5 * x
    return h + h * jnp.tanh(h)


def _rms(x, g, eps=EPS):
    return x * lax.rsqrt(jnp.mean(x * x, axis=-1, keepdims=True) + eps) * g


def _unit_tri_inverse_cat(x, t_idx, s_idx, size, expand):
    def off_blocks(b):
        return ((t_idx // (2 * b) == s_idx // (2 * b)) & (t_idx % (2 * b) >= b) & (s_idx % (2 * b) < b))

    t = (t_idx == s_idx).astype(F32) + jnp.where(off_blocks(1), x, 0.0)
    b = 2
    while b < size:
        m = _bdot(jnp.where(off_blocks(b), x, 0.0), expand(t))
        t = t + _bdot(t, expand(m))
        b *= 2
    return t


def _pick_tile(n, candidates):
    for c in candidates:
        if n % c == 0:
            return c
    raise ValueError(f"no tile for {n}")


def _params(grid_rank):
    return pltpu.CompilerParams(dimension_semantics=("arbitrary",) * grid_rank, vmem_limit_bytes=VMEM_LIMIT)


def _const_spec(shape):
    nd = len(shape)
    return pl.BlockSpec(shape, lambda *_: (0,) * nd, pipeline_mode=pl.Buffered(1))


def _gdn_proj_kernel(x_ref, g_ref, w_ref, cw_ref, cbuf_ref, q_o, k_o, v_o, z_o, ab_o, cst_o, xs_ref):
    t = pl.program_id(1)
    tt = x_ref.shape[0]
    hist = SUBLANES

    @pl.when(t == 0)
    def _():
        xs_ref[...] = cbuf_ref[...]

    xn = _rms(x_ref[...], g_ref[...]).astype(BF16)

    for out, c0 in ((q_o, 0), (k_o, GDN_KDIM), (v_o, 2 * GDN_KDIM)):
        cols = slice(c0, c0 + GDN_KDIM)
        proj = _dot(xn, w_ref[:, cols])
        win = jnp.concatenate([xs_ref[:, cols], proj], axis=0)
        xs_ref[:, cols] = proj[tt - hist:tt]
        cst_o[:, cols] = proj[tt - hist:tt]
        win1 = pltpu.roll(win, 1, 0)
        near = cw_ref[3:4, cols] * win + cw_ref[2:3, cols] * win1
        far = cw_ref[1:2, cols] * win + cw_ref[0:1, cols] * win1
        out[...] = (near + pltpu.roll(far, 2, 0))[hist:].astype(out.dtype)
    z_o[...] = _dot(xn, w_ref[:, GDN_QKV:GDN_QKV + GDN_KDIM]).astype(z_o.dtype)
    ab_o[...] = _dot(xn, w_ref[:, GDN_QKV + GDN_KDIM:])


def _gdn_proj(h3, g_row, w_bf16, conv_w, cbuf8):
    b, tp, d = h3.shape
    n = w_bf16.shape[1]
    tt = _pick_tile(tp, (256, 192, 128, 64))
    tok = lambda width: pl.BlockSpec((None, tt, width), lambda i, t: (i, t, 0))
    hist_spec = pl.BlockSpec((None, SUBLANES, GDN_QKV), lambda i, t: (i, 0, 0))
    return pl.pallas_call(
        _gdn_proj_kernel,
        grid=(b, tp // tt),
        in_specs=[tok(d), _const_spec((1, d)), _const_spec((d, n)), _const_spec((CONV_W, GDN_QKV)), hist_spec],
        out_specs=[tok(GDN_KDIM), tok(GDN_KDIM), tok(GDN_KDIM), tok(GDN_KDIM), tok(2 * LANES), hist_spec],
        out_shape=[jax.ShapeDtypeStruct((b, tp, GDN_KDIM), BF16)] * 4
        + [jax.ShapeDtypeStruct((b, tp, 2 * LANES), F32), jax.ShapeDtypeStruct((b, SUBLANES, GDN_QKV), F32)],
        scratch_shapes=[pltpu.VMEM((SUBLANES, GDN_QKV), F32)],
        compiler_params=_params(2),
        name="gdn_in_proj",
    )(h3, g_row, w_bf16, conv_w, cbuf8)


def _gdn_scan_kernel(q_ref, k_ref, v_ref, ab_ref, s0_ref, alog_ref, dtb_ref, onorm_ref, o_ref, s_ref):
    c = pl.program_id(1)
    L = CHUNK
    D = GDN_DK
    HP = GDN_HEADS // 2

    @pl.when(c == 0)
    def _():
        s_ref[...] = s0_ref[...]

    row = lax.broadcasted_iota(jnp.int32, (L, L), 0)
    col = lax.broadcasted_iota(jnp.int32, (L, L), 1)
    tril = (row >= col).astype(BF16)
    t_idx = lax.broadcasted_iota(jnp.int32, (L, 2 * L), 0)
    lane = lax.broadcasted_iota(jnp.int32, (L, 2 * L), 1)
    first = lane < L
    s_idx = lane % L
    causal_c = t_idx >= s_idx
    strict_c = t_idx > s_idx

    nb = q_ref.shape[0]
    a_in = jnp.concatenate([ab_ref[i, :, :LANES] + dtb_ref[...] for i in range(nb)], axis=1)
    b_in = jnp.concatenate([ab_ref[i, :, LANES:] for i in range(nb)], axis=1)
    neg_a = jnp.concatenate([-jnp.exp(alog_ref[...])] * nb, axis=1)
    g = neg_a * jax.nn.softplus(a_in)
    beta = _sigmoid(b_in)
    G = _dot_exact_lhs(tril, g)
    Gt = G.T

    pairs = [(i, p) for i in range(nb) for p in range(HP)]

    def per_pair(ref):
        return jnp.stack([ref[i, :, 2 * p * D:2 * (p + 1) * D] for i, p in pairs])

    def halves(x):
        return x[..., :D], x[..., D:]

    def per_half(fn, x):
        xa, xb = halves(x)
        return jnp.concatenate([fn(xa), fn(xb)], axis=-1)

    def head_cols(x2d):
        return [jnp.stack([x2d[:, i * LANES + 2 * p + j:i * LANES + 2 * p + j + 1] for i, p in pairs])
                for j in range(2)]

    def wide(cols):
        return jnp.concatenate([jnp.broadcast_to(c, c.shape[:2] + (D,)) for c in cols], axis=-1)

    def cat(cols):
        return jnp.where(first, cols[0], cols[1])

    def expand(x):
        xa, xb = halves(x.astype(BF16))
        zero = jnp.zeros_like(xa)
        return jnp.concatenate([jnp.concatenate([xa, zero], axis=-1),
                                jnp.concatenate([zero, xb], axis=-1)], axis=1)

    def expand_cat(x):
        xb = x.astype(BF16)
        zero = jnp.zeros_like(xb)
        return jnp.concatenate([jnp.where(first, xb, zero), jnp.where(first, zero, xb)], axis=1)

    def l2n(x):
        return x * lax.rsqrt(jnp.sum(x * x, axis=-1, keepdims=True) + L2_EPS)

    q = per_half(l2n, _silu(per_pair(q_ref).astype(F32))) * (GDN_DK ** -0.5)
    kf = per_half(l2n, _silu(per_pair(k_ref).astype(F32)))
    v = _silu(per_pair(v_ref).astype(F32))
    k = kf.astype(BF16)

    b_cols = head_cols(beta)
    g_cols = head_cols(G)
    g_last = [c[:, L - 1:L, :] for c in g_cols]
    g_row = jnp.stack([jnp.concatenate([Gt[i * LANES + 2 * p + j:i * LANES + 2 * p + j + 1, :] for j in range(2)],
                                       axis=1) for i, p in pairs])
    b_wide = wide(b_cols)
    eg_wide = wide([jnp.exp(c) for c in g_cols])
    egr_wide = wide([jnp.exp(gl - c) for gl, c in zip(g_last, g_cols)])
    egl_wide = wide([jnp.exp(gl) for gl in g_last])

    dlt = cat(g_cols) - g_row
    dec = jnp.where(causal_c, jnp.exp(jnp.where(causal_c, dlt, 0.0)), 0.0)
    kq_k = _bdot_nt(jnp.concatenate([k, q.astype(BF16)], axis=1), expand(k))
    a_mat = jnp.where(strict_c, cat(b_cols) * kq_k[:, :L] * dec, 0.0)
    qk = kq_k[:, L:] * dec
    t_inv = _unit_tri_inverse_cat(-a_mat, t_idx, s_idx, L, expand_cat)

    s = s_ref[...].reshape(len(pairs), D, 2 * D)
    ks_qs = _bdot(jnp.concatenate([k, (q * eg_wide).astype(BF16)], axis=1), expand(s))
    v_new = _bdot(t_inv, expand(b_wide * (v - eg_wide * ks_qs[:, :L])))
    o = ks_qs[:, L:] + _bdot(qk, expand(v_new))
    cross = _bdot_tn(kf * egr_wide, v_new)
    s_ref[...] = (s * egl_wide + jnp.concatenate([cross[:, :D, :D], cross[:, D:, D:]], axis=-1)).reshape(s_ref.shape)

    def o_norm(x):
        return x * lax.rsqrt(jnp.mean(x * x, axis=-1, keepdims=True) + EPS) * onorm_ref[...]

    o = per_half(o_norm, o)
    for n, (i, p) in enumerate(pairs):
        o_ref[i, :, 2 * p * D:2 * (p + 1) * D] = o[n].astype(o_ref.dtype)


def _scan_batch(b, most=4):
    return next(n for n in (8, 4, 2, 1) if n <= most and b % n == 0)


def _gdn_scan(q, k, v, ab, s0, alog_row, dtb_row, onorm_row):
    b, tp, _ = q.shape
    nb = _scan_batch(b)
    tok = pl.BlockSpec((nb, CHUNK, GDN_KDIM), lambda i, c: (i, c, 0))
    st = pl.BlockSpec((nb, GDN_HEADS // 2, GDN_DK, 2 * GDN_DV), lambda i, c: (i, 0, 0, 0))
    return pl.pallas_call(
        _gdn_scan_kernel,
        grid=(b // nb, tp // CHUNK),
        in_specs=[tok, tok, tok, pl.BlockSpec((nb, CHUNK, 2 * LANES), lambda i, c: (i, c, 0)), st,
                  _const_spec((1, LANES)), _const_spec((1, LANES)), _const_spec((1, GDN_DV))],
        out_specs=[tok, st],
        out_shape=[jax.ShapeDtypeStruct((b, tp, GDN_KDIM), BF16),
                   jax.ShapeDtypeStruct((b, GDN_HEADS // 2, GDN_DK, 2 * GDN_DV), F32)],
        compiler_params=_params(2),
        name="gdn_scan",
    )(q, k, v, ab, s0, alog_row, dtb_row, onorm_row)


def _mix_ffn_kernel(*refs, silu_gate, final_norm):
    if final_norm:
        y_ref, gm_ref, h_ref, wo_ref, nf_ref, wg_ref, wu_ref, wd_ref, fn_ref, o_ref = refs
    else:
        y_ref, gm_ref, h_ref, wo_ref, nf_ref, wg_ref, wu_ref, wd_ref, o_ref = refs
    gm = gm_ref[...].astype(F32)
    if silu_gate:
        gm = _silu(gm)
    h1 = h_ref[...] + _dot(y_ref[...].astype(F32) * gm, wo_ref[...])
    n = _rms(h1, nf_ref[...])
    act = _silu(_dot(n, wg_ref[...])) * _dot(n, wu_ref[...])
    h2 = h1 + _dot(act, wd_ref[...])
    if final_norm:
        h2 = _rms(h2, fn_ref[...])
    o_ref[...] = h2


def _mix_ffn(y2d, gm2d, h2d, wo, nf_row, wg, wu, wd, fn_row, *, silu_gate, name):
    rows, d = h2d.shape
    dff = wg.shape[1]
    tm = _pick_tile(rows, (512, 256, 128, 64))
    final_norm = fn_row is not None
    tok = pl.BlockSpec((tm, d), lambda i: (i, 0))
    in_specs = [tok, tok, tok, _const_spec((d, d)), _const_spec((1, d)),
                _const_spec((d, dff)), _const_spec((d, dff)), _const_spec((dff, d))]
    args = [y2d, gm2d, h2d, wo, nf_row, wg, wu, wd]
    if final_norm:
        in_specs.append(_const_spec((1, d)))
        args.append(fn_row)
    return pl.pallas_call(
        functools.partial(_mix_ffn_kernel, silu_gate=silu_gate, final_norm=final_norm),
        grid=(rows // tm,),
        in_specs=in_specs,
        out_specs=tok,
        out_shape=jax.ShapeDtypeStruct((rows, d), F32),
        compiler_params=_params(1),
        name=name,
    )(*args)


def _rwkv_proj_kernel(h_ref, sb_ref, vec_ref, mu_ref, wr_ref, wk_ref, wv_ref,
                      w1_ref, w2_ref, a1_ref, a2_ref, g1_ref, g2_ref,
                      r_o, k_o, v_o, kk_o, a_o, w_o, g_o, last_o, hs_ref):
    t = pl.program_id(1)
    tt = h_ref.shape[0]
    hist = SUBLANES
    hn = _rms(h_ref[...], vec_ref[0:1, :])

    @pl.when(t == 0)
    def _():
        hs_ref[...] = jnp.broadcast_to(sb_ref[...], hs_ref.shape)

    prev = pltpu.roll(jnp.concatenate([hs_ref[...], hn], axis=0), 1, 0)[hist:]
    hs_ref[hist - 1:hist, :] = hn[tt - 1:tt, :]
    last_o[...] = hn[tt - 1:tt, :]
    xx = prev - hn

    def mix(i):
        return (hn + xx * mu_ref[i:i + 1, :]).astype(BF16)

    r = _dot(mix(0), wr_ref[...])
    k = _dot(mix(2), wk_ref[...])
    v = _dot(mix(3), wv_ref[...])
    wl = vec_ref[1:2, :] + _dot(jnp.tanh(_dot(mix(1), w1_ref[...])), w2_ref[...])
    a = _sigmoid(vec_ref[2:3, :] + _dot(_dot(mix(4), a1_ref[...]), a2_ref[...]))
    gate = _dot(_sigmoid(_dot(mix(5), g1_ref[...])), g2_ref[...])
    r_o[...] = r.astype(r_o.dtype)
    k_o[...] = (k * (1.0 + (a - 1.0) * vec_ref[4:5, :])).astype(k_o.dtype)
    v_o[...] = v.astype(v_o.dtype)
    kk_o[...] = (k * vec_ref[3:4, :]).astype(kk_o.dtype)
    a_o[...] = a.astype(a_o.dtype)
    w_o[...] = -DECAY_SCALE * _sigmoid(wl)
    g_o[...] = gate.astype(g_o.dtype)


def _rwkv_proj(h3, sbuf, vec, mu8, wr, wk, wv, w1, w2, a1, a2, g1, g2):
    b, tp, d = h3.shape
    tt = _pick_tile(tp, (512, 256, 192, 128, 64))
    tok = pl.BlockSpec((None, tt, d), lambda i, t: (i, t, 0))
    row = pl.BlockSpec((None, 1, d), lambda i, t: (i, 0, 0))
    weights = [wr, wk, wv, w1, w2, a1, a2, g1, g2]
    tok_shape = lambda dt: jax.ShapeDtypeStruct((b, tp, d), dt)
    return pl.pallas_call(
        _rwkv_proj_kernel,
        grid=(b, tp // tt),
        in_specs=[tok, row, _const_spec(vec.shape), _const_spec(mu8.shape)] + [_const_spec(w.shape) for w in weights],
        out_specs=[tok] * 7 + [row],
        out_shape=[tok_shape(BF16)] * 5 + [tok_shape(F32), tok_shape(BF16), jax.ShapeDtypeStruct((b, 1, d), F32)],
        scratch_shapes=[pltpu.VMEM((SUBLANES, d), F32)],
        compiler_params=_params(2),
        name="rwkv_proj",
    )(h3, sbuf, vec, mu8, *weights)


def _rwkv_scan_kernel(r_ref, k_ref, v_ref, kk_ref, a_ref, w_ref, s0_ref, rk_ref, lnw_ref, lnb_ref,
                      yo_ref, s_ref):
    c = pl.program_id(1)
    L = CHUNK
    N = RWKV_N
    G = RWKV_GROUP
    Q = G * N
    nb, _, d = r_ref.shape
    groups = range(nb * d // Q)

    def cat_batch(ref):
        return jnp.concatenate([ref[i] for i in range(nb)], axis=1)

    @pl.when(c == 0)
    def _():
        s_ref[...] = s0_ref[...]

    row = lax.broadcasted_iota(jnp.int32, (L, L), 0)
    col = lax.broadcasted_iota(jnp.int32, (L, L), 1)
    tril = (row >= col).astype(BF16)
    t_idx = lax.broadcasted_iota(jnp.int32, (L, Q), 0)
    lane = lax.broadcasted_iota(jnp.int32, (L, Q), 1)
    s_idx = lane % N
    strict_c = t_idx > s_idx
    incl_c = t_idx >= s_idx
    ones_blk = (lax.broadcasted_iota(jnp.int32, (Q, Q), 0) // N
                == lax.broadcasted_iota(jnp.int32, (Q, Q), 1) // N).astype(BF16)
    head_lanes = [lax.broadcasted_iota(jnp.int32, (N, Q), 1) // N == g for g in range(G)]

    def per_group(x2d):
        return jnp.stack([x2d[:, q * Q:(q + 1) * Q] for q in groups])

    def expand(x3):
        xb = x3.astype(BF16)
        zero = jnp.zeros(xb.shape[:2] + (LANES,), BF16)
        slot_of_lane = lax.broadcasted_iota(jnp.int32, xb.shape[1:2] + (LANES,), 1) // N
        blocks = []
        for g in range(G):
            col, slot = divmod(g * N, LANES)
            piece = jnp.where(slot_of_lane == slot // N, xb[:, :, col * LANES:(col + 1) * LANES], zero)
            blocks.append(jnp.concatenate([piece if c == col else zero for c in range(Q // LANES)], axis=2))
        return jnp.concatenate(blocks, axis=1)

    def diag_blocks(x3):
        return sum(jnp.where(head_lanes[g], x3[:, g * N:(g + 1) * N, :], 0.0) for g in range(G))

    def lane_sums(x3):
        n, rws, _ = x3.shape
        return _dot(x3.reshape(n * rws, Q), ones_blk).reshape(n, rws, Q)

    w_all = cat_batch(w_ref)
    cw_all = _dot_exact_lhs(tril, w_all)
    cwl = cw_all[L - 1:L, :]
    p_in = per_group(jnp.exp(cw_all))
    p_ex = per_group(jnp.exp(cw_all - w_all))
    p_inv = per_group(jnp.exp(-cw_all))
    p_rem = per_group(jnp.exp(cwl - cw_all))
    dec_all = jnp.exp(cwl)
    p_all = jnp.stack([dec_all[:, q * Q:(q + 1) * Q] for q in groups])

    kkr = per_group(cat_batch(kk_ref).astype(F32))
    kkn = kkr * lax.rsqrt(lane_sums(kkr * kkr) + L2_EPS)
    rp = per_group(cat_batch(r_ref).astype(F32))
    kp = per_group(cat_batch(k_ref).astype(F32))
    vp = per_group(cat_batch(v_ref).astype(F32))
    bb = kkn * per_group(cat_batch(a_ref).astype(F32))

    lhs = jnp.concatenate([-kkn * p_ex, rp * p_in], axis=1).astype(BF16)
    m = _bdot_nt(lhs, jnp.concatenate([expand(bb * p_inv), expand(kp * p_inv)], axis=1))
    l_ab = jnp.where(strict_c, m[:, :L, :Q], 0.0)
    l_ak = jnp.where(strict_c, m[:, :L, Q:], 0.0)
    m_rb = jnp.where(incl_c, m[:, L:, :Q], 0.0)
    m_rk = jnp.where(incl_c, m[:, L:, Q:], 0.0)

    s = s_ref[...].reshape(len(groups), N, Q)
    from_s = _bdot_nt(lhs, expand(s))

    t_inv = _unit_tri_inverse_cat(l_ab, t_idx, s_idx, L, expand)

    from_v = _bdot(jnp.concatenate([l_ak, m_rk], axis=1), expand(vp))
    u = _bdot(t_inv, expand(from_s[:, :L] + from_v[:, :L]))
    y = from_s[:, L:] + from_v[:, L:] + _bdot(m_rb, expand(u))
    cross = _bdot_tn(jnp.concatenate([u, vp], axis=1), jnp.concatenate([bb * p_rem, kp * p_rem], axis=1))
    s_ref[...] = (s * p_all + diag_blocks(cross)).reshape(s_ref.shape)

    mean = lane_sums(y) * (1.0 / N)
    cy = y - mean
    var = lane_sums(cy * cy) * (1.0 / N)
    rk3 = per_group(jnp.concatenate([rk_ref[...]] * nb, axis=1))
    bonus = lane_sums(rp * kp * rk3) * vp
    for g in groups:
        i, q = divmod(g, d // Q)
        sl = slice(q * Q, (q + 1) * Q)
        yn = cy[g] * lax.rsqrt(var[g] + GN_EPS) * lnw_ref[:, sl] + lnb_ref[:, sl]
        yo_ref[i, :, sl] = (yn + bonus[g]).astype(yo_ref.dtype)


def _rwkv_scan(r, k, v, kk, a, w, s0_groups, rk_row, lnw_row, lnb_row):
    b, tp, d = r.shape
    q = RWKV_GROUP * RWKV_N
    nb = _scan_batch(b, most=8)
    tok = pl.BlockSpec((nb, CHUNK, d), lambda i, c: (i, c, 0))
    st = pl.BlockSpec((nb, d // q, RWKV_N, q), lambda i, c: (i, 0, 0, 0))
    return pl.pallas_call(
        _rwkv_scan_kernel,
        grid=(b // nb, tp // CHUNK),
        in_specs=[tok] * 6 + [st, _const_spec((1, d)), _const_spec((1, d)), _const_spec((1, d))],
        out_specs=[tok, st],
        out_shape=[jax.ShapeDtypeStruct((b, tp, d), BF16),
                   jax.ShapeDtypeStruct((b, d // q, RWKV_N, q), F32)],
        compiler_params=_params(2),
        name="rwkv_scan",
    )(r, k, v, kk, a, w, s0_groups, rk_row, lnw_row, lnb_row)


def _pad_cols(w, n):
    return jnp.pad(w, ((0, 0), (0, n - w.shape[1])))


def _pad_rows(w, n):
    return jnp.pad(w, ((0, n - w.shape[0]), (0, 0)))


def _groups_from_state(s, g):
    b, h, r, c = s.shape
    return jnp.swapaxes(s.reshape(b, h // g, g, r, c), 2, 3).reshape(b, h // g, r, g * c)


def _state_from_groups(sg, g):
    b, hq, r, gc = sg.shape
    return jnp.swapaxes(sg.reshape(b, hq, r, g, gc // g), 2, 3).reshape(b, hq * g, r, gc // g)


def _run_trunk(h, conv_bufs, gdn_states, shift_bufs, rwkv_states, wts):
    b, tp, d = h.shape
    rows = b * tp
    depth = wts["norm_mix"].shape[0]
    new_conv, new_gdn, new_shift, new_rwkv = [], [], [], []
    for i in range(depth):
        j = i // 2
        fn_row = wts["norm_final"] if i == depth - 1 else None
        if i % 2 == 0:
            g = wts["gdn"][j]
            cbuf8 = jnp.pad(conv_bufs[j], ((0, 0), (SUBLANES - (CONV_W - 1), 0), (0, 0)))
            q, k, v, z, ab, cst = _gdn_proj(h, wts["norm_mix"][i:i + 1], g["w_in"], g["conv_w"], cbuf8)
            o, s_new = _gdn_scan(q, k, v, ab, _groups_from_state(gdn_states[j], 2), g["a_log"], g["dt_bias"],
                                 g["o_norm"])
            new_conv.append(cst[:, SUBLANES - (CONV_W - 1):])
            new_gdn.append(_state_from_groups(s_new, 2))
            h2 = _mix_ffn(o.reshape(rows, d), z.reshape(rows, d), h.reshape(rows, d), g["w_out"],
                          wts["norm_ffn"][i:i + 1], wts["ffn_gate"][i], wts["ffn_up"][i], wts["ffn_down"][i],
                          fn_row, silu_gate=True, name="gdn_out_ffn")
        else:
            rw = wts["rwkv"][j]
            vec = jnp.concatenate([wts["norm_mix"][i:i + 1], rw["vec"]], axis=0)
            r, k, v, kk, a, w, gate, last_hn = _rwkv_proj(
                h, shift_bufs[j], vec, rw["mu"], rw["w_r"], rw["w_k"], rw["w_v"],
                rw["w1"], rw["w2"], rw["a1"], rw["a2"], rw["g1"], rw["g2"])
            yo, sp_new = _rwkv_scan(r, k, v, kk, a, w, _groups_from_state(rwkv_states[j], RWKV_GROUP),
                                    rw["r_k"], rw["ln_w"], rw["ln_b"])
            new_shift.append(last_hn)
            new_rwkv.append(_state_from_groups(sp_new, RWKV_GROUP))
            h2 = _mix_ffn(yo.reshape(rows, d), gate.reshape(rows, d), h.reshape(rows, d), rw["w_o"],
                          wts["norm_ffn"][i:i + 1], wts["ffn_gate"][i], wts["ffn_up"][i], wts["ffn_down"][i],
                          fn_row, silu_gate=False, name="rwkv_out_ffn")
        h = h2.reshape(b, tp, d)
    return h, jnp.stack(new_conv), jnp.stack(new_gdn), jnp.stack(new_shift), jnp.stack(new_rwkv)


def _prepare_weights(norm_mix, norm_ffn, norm_final, gdn_w_in, gdn_conv_w, gdn_a_log, gdn_dt_bias, gdn_o_norm,
                     gdn_w_out, rwkv_mu, rwkv_w0, rwkv_w1, rwkv_w2, rwkv_a0, rwkv_a1, rwkv_a2, rwkv_g1, rwkv_g2,
                     rwkv_k_k, rwkv_k_a, rwkv_r_k, rwkv_w_r, rwkv_w_k, rwkv_w_v, rwkv_w_o, rwkv_ln_w,
                     rwkv_ln_b, ffn_w_gate, ffn_w_up, ffn_w_down):
    d = norm_mix.shape[1]
    gdn = []
    for j in range(gdn_w_in.shape[0]):
        w = gdn_w_in[j]
        qkvz = w[:, :GDN_QKV + GDN_KDIM]
        a_cols = _pad_cols(w[:, GDN_QKV + GDN_KDIM:GDN_QKV + GDN_KDIM + GDN_HEADS], LANES)
        b_cols = _pad_cols(w[:, GDN_QKV + GDN_KDIM + GDN_HEADS:], LANES)
        gdn.append(dict(
            w_in=jnp.concatenate([qkvz, a_cols, b_cols], axis=1).astype(BF16),
            conv_w=gdn_conv_w[j],
            a_log=_pad_cols(gdn_a_log[j][None], LANES),
            dt_bias=_pad_cols(gdn_dt_bias[j][None], LANES),
            o_norm=gdn_o_norm[j][None],
            w_out=gdn_w_out[j].astype(BF16),
        ))
    rwkv = []
    for j in range(rwkv_mu.shape[0]):
        lw = -(-rwkv_w1.shape[2] // LANES) * LANES
        la = -(-rwkv_a1.shape[2] // LANES) * LANES
        lg = -(-rwkv_g1.shape[2] // LANES) * LANES
        rwkv.append(dict(
            vec=jnp.stack([rwkv_w0[j], rwkv_a0[j], rwkv_k_k[j], rwkv_k_a[j],
                           jnp.zeros_like(rwkv_w0[j]), jnp.zeros_like(rwkv_w0[j]), jnp.zeros_like(rwkv_w0[j])]),
            mu=_pad_rows(rwkv_mu[j], SUBLANES),
            w_r=rwkv_w_r[j].astype(BF16), w_k=rwkv_w_k[j].astype(BF16), w_v=rwkv_w_v[j].astype(BF16),
            w_o=rwkv_w_o[j].astype(BF16),
            w1=_pad_cols(rwkv_w1[j], lw).astype(BF16), w2=_pad_rows(rwkv_w2[j], lw).astype(BF16),
            a1=_pad_cols(rwkv_a1[j], la).astype(BF16), a2=_pad_rows(rwkv_a2[j], la).astype(BF16),
            g1=_pad_cols(rwkv_g1[j], lg).astype(BF16), g2=_pad_rows(rwkv_g2[j], lg).astype(BF16),
            r_k=rwkv_r_k[j].reshape(1, d), ln_w=rwkv_ln_w[j][None], ln_b=rwkv_ln_b[j][None],
        ))
    return dict(norm_mix=norm_mix, norm_ffn=norm_ffn, norm_final=norm_final[None], gdn=gdn, rwkv=rwkv,
                ffn_gate=ffn_w_gate.astype(BF16), ffn_up=ffn_w_up.astype(BF16), ffn_down=ffn_w_down.astype(BF16))


def kernel(x_prompt, x_sample, cache_gdn_conv, state_gdn, cache_rwkv_shift, state_rwkv, meta_tokens, norm_mix, norm_ffn, norm_final, gdn_w_in, gdn_conv_w, gdn_a_log, gdn_dt_bias, gdn_o_norm, gdn_w_out, rwkv_mu, rwkv_w0, rwkv_w1, rwkv_w2, rwkv_a0, rwkv_a1, rwkv_a2, rwkv_g1, rwkv_g2, rwkv_k_k, rwkv_k_a, rwkv_r_k, rwkv_w_r, rwkv_w_k, rwkv_w_v, rwkv_w_o, rwkv_ln_w, rwkv_ln_b, ffn_w_gate, ffn_w_up, ffn_w_down):
    wts = _prepare_weights(norm_mix, norm_ffn, norm_final, gdn_w_in, gdn_conv_w, gdn_a_log, gdn_dt_bias,
                           gdn_o_norm, gdn_w_out, rwkv_mu, rwkv_w0, rwkv_w1, rwkv_w2, rwkv_a0, rwkv_a1, rwkv_a2,
                           rwkv_g1, rwkv_g2, rwkv_k_k, rwkv_k_a, rwkv_r_k, rwkv_w_r, rwkv_w_k, rwkv_w_v,
                           rwkv_w_o, rwkv_ln_w, rwkv_ln_b, ffn_w_gate, ffn_w_up, ffn_w_down)
    n_gdn = gdn_w_in.shape[0]
    n_rwkv = rwkv_mu.shape[0]
    b, seq, d = x_prompt.shape
    dt = x_prompt.dtype

    lead_pad = (-N_META) % CHUNK
    lead = jnp.concatenate([jnp.zeros((b, lead_pad, d), dt),
                            jnp.broadcast_to(meta_tokens.astype(dt)[None], (b, N_META, d))], axis=1)
    zeros_like_states = (jnp.zeros((n_gdn, b, CONV_W - 1, GDN_QKV), dt),
                         jnp.zeros((n_gdn, b, GDN_HEADS, GDN_DK, GDN_DV), dt),
                         jnp.zeros((n_rwkv, b, 1, d), dt),
                         jnp.zeros((n_rwkv, b, d // RWKV_N, RWKV_N, RWKV_N), dt))
    sample_states = (cache_gdn_conv, state_gdn, cache_rwkv_shift, state_rwkv)

    bs = x_sample.shape[0]
    if x_sample.shape[1:] == lead.shape[1:]:
        both = _run_trunk(jnp.concatenate([x_sample, lead], axis=0),
                          *[jnp.concatenate([s, z], axis=1) for s, z in zip(sample_states, zeros_like_states)],
                          wts)
        y_sample, s_conv, s_gdn, s_shift, s_rwkv = [both[0][:bs]] + [t[:, :bs] for t in both[1:]]
        lead_states = [t[:, bs:] for t in both[1:]]
    else:
        y_sample, s_conv, s_gdn, s_shift, s_rwkv = _run_trunk(x_sample, *sample_states, wts)
        lead_states = _run_trunk(lead, *zeros_like_states, wts)[1:]

    y_prompt, p_conv, p_gdn, p_shift, p_rwkv = _run_trunk(x_prompt, *lead_states, wts)

    return (y_prompt, y_sample, p_conv, p_gdn, p_shift, p_rwkv, s_conv, s_gdn, s_shift, s_rwkv)
```

```python
import functools

import jax
import jax.numpy as jnp
from jax import lax
from jax.experimental import pallas as pl
from jax.experimental.pallas import tpu as pltpu

F32 = jnp.float32
BF16 = jnp.bfloat16

CHUNK = 64
N_META = 16
EPS = 1e-6
L2_EPS = 1e-6
GN_EPS = 64e-5

GDN_HEADS = 8
GDN_DK = 128
GDN_DV = 128
GDN_KDIM = GDN_HEADS * GDN_DK
GDN_QKV = 3 * GDN_KDIM
CONV_W = 4
RWKV_N = 64
RWKV_GROUP = 2
DECAY_SCALE = 0.6065306597126334

LANES = 128
SUBLANES = 8
VMEM_LIMIT = 56 * 1024 * 1024


def _dot(a, b):
    return jnp.dot(a.astype(BF16), b.astype(BF16), preferred_element_type=F32)


def _bdot(a, b):
    return lax.dot_general(a.astype(BF16), b.astype(BF16), (((2,), (1,)), ((0,), (0,))),
                           preferred_element_type=F32)


def _bdot_nt(a, b):
    return lax.dot_general(a.astype(BF16), b.astype(BF16), (((2,), (2,)), ((0,), (0,))),
                           preferred_element_type=F32)


def _bdot_tn(a, b):
    return lax.dot_general(a.astype(BF16), b.astype(BF16), (((1,), (1,)), ((0,), (0,))),
                           preferred_element_type=F32)


def _split_bf16(x, parts):
    out = []
    for _ in range(parts - 1):
        hi = x.astype(BF16)
        out.append(hi)
        x = x - hi.astype(F32)
    out.append(x.astype(BF16))
    return out


def _dot_exact_lhs(m01, x, parts=3):
    return sum(_dot(m01, t) for t in _split_bf16(x, parts))


def _sigmoid(x):
    return 0.5 + 0.5 * jnp.tanh(0.5 * x)


def _silu(x):
    h = 0.5 * x
    return h + h * jnp.tanh(h)


def _rms(x, g, eps=EPS):
    return x * lax.rsqrt(jnp.mean(x * x, axis=-1, keepdims=True) + eps) * g


def _unit_tri_inverse_cat(x, t_idx, s_idx, size, expand):
    def off_blocks(b):
        return ((t_idx // (2 * b) == s_idx // (2 * b)) & (t_idx % (2 * b) >= b) & (s_idx % (2 * b) < b))

    t = (t_idx == s_idx).astype(F32) + jnp.where(off_blocks(1), x, 0.0)
    b = 2
    while b < size:
        m = _bdot(jnp.where(off_blocks(b), x, 0.0), expand(t))
        t = t + _bdot(t, expand(m))
        b *= 2
    return t


def _pick_tile(n, candidates):
    for c in candidates:
        if n % c == 0:
            return c
    raise ValueError(f"no tile for {n}")


def _params(grid_rank):
    return pltpu.CompilerParams(dimension_semantics=("arbitrary",) * grid_rank, vmem_limit_bytes=VMEM_LIMIT)


def _const_spec(shape):
    nd = len(shape)
    return pl.BlockSpec(shape, lambda *_: (0,) * nd, pipeline_mode=pl.Buffered(1))


def _gdn_proj_kernel(x_ref, g_ref, w_ref, cw_ref, cbuf_ref, q_o, k_o, v_o, z_o, ab_o, cst_o, xs_ref):
    t = pl.program_id(1)
    tt = x_ref.shape[0]
    hist = SUBLANES

    @pl.when(t == 0)
    def _():
        xs_ref[...] = cbuf_ref[...]

    xn = _rms(x_ref[...], g_ref[...]).astype(BF16)

    for out, c0 in ((q_o, 0), (k_o, GDN_KDIM), (v_o, 2 * GDN_KDIM)):
        cols = slice(c0, c0 + GDN_KDIM)
        proj = _dot(xn, w_ref[:, cols])
        win = jnp.concatenate([xs_ref[:, cols], proj], axis=0)
        xs_ref[:, cols] = proj[tt - hist:tt]
        cst_o[:, cols] = proj[tt - hist:tt]
        win1 = pltpu.roll(win, 1, 0)
        near = cw_ref[3:4, cols] * win + cw_ref[2:3, cols] * win1
        far = cw_ref[1:2, cols] * win + cw_ref[0:1, cols] * win1
        out[...] = (near + pltpu.roll(far, 2, 0))[hist:].astype(out.dtype)
    z_o[...] = _dot(xn, w_ref[:, GDN_QKV:GDN_QKV + GDN_KDIM]).astype(z_o.dtype)
    ab_o[...] = _dot(xn, w_ref[:, GDN_QKV + GDN_KDIM:])


def _gdn_proj(h3, g_row, w_bf16, conv_w, cbuf8):
    b, tp, d = h3.shape
    n = w_bf16.shape[1]
    tt = _pick_tile(tp, (256, 192, 128, 64))
    tok = lambda width: pl.BlockSpec((None, tt, width), lambda i, t: (i, t, 0))
    hist_spec = pl.BlockSpec((None, SUBLANES, GDN_QKV), lambda i, t: (i, 0, 0))
    return pl.pallas_call(
        _gdn_proj_kernel,
        grid=(b, tp // tt),
        in_specs=[tok(d), _const_spec((1, d)), _const_spec((d, n)), _const_spec((CONV_W, GDN_QKV)), hist_spec],
        out_specs=[tok(GDN_KDIM), tok(GDN_KDIM), tok(GDN_KDIM), tok(GDN_KDIM), tok(2 * LANES), hist_spec],
        out_shape=[jax.ShapeDtypeStruct((b, tp, GDN_KDIM), BF16)] * 4
        + [jax.ShapeDtypeStruct((b, tp, 2 * LANES), F32), jax.ShapeDtypeStruct((b, SUBLANES, GDN_QKV), F32)],
        scratch_shapes=[pltpu.VMEM((SUBLANES, GDN_QKV), F32)],
        compiler_params=_params(2),
        name="gdn_in_proj",
    )(h3, g_row, w_bf16, conv_w, cbuf8)


def _gdn_scan_kernel(q_ref, k_ref, v_ref, ab_ref, s0_ref, alog_ref, dtb_ref, onorm_ref, o_ref, s_ref):
    c = pl.program_id(1)
    L = CHUNK
    D = GDN_DK
    HP = GDN_HEADS // 2

    @pl.when(c == 0)
    def _():
        s_ref[...] = s0_ref[...]

    row = lax.broadcasted_iota(jnp.int32, (L, L), 0)
    col = lax.broadcasted_iota(jnp.int32, (L, L), 1)
    tril = (row >= col).astype(BF16)
    t_idx = lax.broadcasted_iota(jnp.int32, (L, 2 * L), 0)
    lane = lax.broadcasted_iota(jnp.int32, (L, 2 * L), 1)
    first = lane < L
    s_idx = lane % L
    causal_c = t_idx >= s_idx
    strict_c = t_idx > s_idx

    nb = q_ref.shape[0]
    a_in = jnp.concatenate([ab_ref[i, :, :LANES] + dtb_ref[...] for i in range(nb)], axis=1)
    b_in = jnp.concatenate([ab_ref[i, :, LANES:] for i in range(nb)], axis=1)
    neg_a = jnp.concatenate([-jnp.exp(alog_ref[...])] * nb, axis=1)
    g = neg_a * jax.nn.softplus(a_in)
    beta = _sigmoid(b_in)
    G = _dot_exact_lhs(tril, g)
    Gt = G.T

    pairs = [(i, p) for i in range(nb) for p in range(HP)]

    def per_pair(ref):
        return jnp.stack([ref[i, :, 2 * p * D:2 * (p + 1) * D] for i, p in pairs])

    def halves(x):
        return x[..., :D], x[..., D:]

    def per_half(fn, x):
        xa, xb = halves(x)
        return jnp.concatenate([fn(xa), fn(xb)], axis=-1)

    def head_cols(x2d):
        return [jnp.stack([x2d[:, i * LANES + 2 * p + j:i * LANES + 2 * p + j + 1] for i, p in pairs])
                for j in range(2)]

    def wide(cols):
        return jnp.concatenate([jnp.broadcast_to(c, c.shape[:2] + (D,)) for c in cols], axis=-1)

    def cat(cols):
        return jnp.where(first, cols[0], cols[1])

    def expand(x):
        xa, xb = halves(x.astype(BF16))
        zero = jnp.zeros_like(xa)
        return jnp.concatenate([jnp.concatenate([xa, zero], axis=-1),
                                jnp.concatenate([zero, xb], axis=-1)], axis=1)

    def expand_cat(x):
        xb = x.astype(BF16)
        zero = jnp.zeros_like(xb)
        return jnp.concatenate([jnp.where(first, xb, zero), jnp.where(first, zero, xb)], axis=1)

    def l2n(x):
        return x * lax.rsqrt(jnp.sum(x * x, axis=-1, keepdims=True) + L2_EPS)

    q = per_half(l2n, _silu(per_pair(q_ref).astype(F32))) * (GDN_DK ** -0.5)
    kf = per_half(l2n, _silu(per_pair(k_ref).astype(F32)))
    v = _silu(per_pair(v_ref).astype(F32))
    k = kf.astype(BF16)

    b_cols = head_cols(beta)
    g_cols = head_cols(G)
    g_last = [c[:, L - 1:L, :] for c in g_cols]
    g_row = jnp.stack([jnp.concatenate([Gt[i * LANES + 2 * p + j:i * LANES + 2 * p + j + 1, :] for j in range(2)],
                                       axis=1) for i, p in pairs])
    b_wide = wide(b_cols)
    eg_wide = wide([jnp.exp(c) for c in g_cols])
    egr_wide = wide([jnp.exp(gl - c) for gl, c in zip(g_last, g_cols)])
    egl_wide = wide([jnp.exp(gl) for gl in g_last])

    dlt = cat(g_cols) - g_row
    dec = jnp.where(causal_c, jnp.exp(jnp.where(causal_c, dlt, 0.0)), 0.0)
    kq_k = _bdot_nt(jnp.concatenate([k, q.astype(BF16)], axis=1), expand(k))
    a_mat = jnp.where(strict_c, cat(b_cols) * kq_k[:, :L] * dec, 0.0)
    qk = kq_k[:, L:] * dec
    t_inv = _unit_tri_inverse_cat(-a_mat, t_idx, s_idx, L, expand_cat)

    s = s_ref[...].reshape(len(pairs), D, 2 * D)
    ks_qs = _bdot(jnp.concatenate([k, (q * eg_wide).astype(BF16)], axis=1), expand(s))
    v_new = _bdot(t_inv, expand(b_wide * (v - eg_wide * ks_qs[:, :L])))
    o = ks_qs[:, L:] + _bdot(qk, expand(v_new))
    cross = _bdot_tn(kf * egr_wide, v_new)
    s_ref[...] = (s * egl_wide + jnp.concatenate([cross[:, :D, :D], cross[:, D:, D:]], axis=-1)).reshape(s_ref.shape)

    def o_norm(x):
        return x * lax.rsqrt(jnp.mean(x * x, axis=-1, keepdims=True) + EPS) * onorm_ref[...]

    o = per_half(o_norm, o)
    for n, (i, p) in enumerate(pairs):
        o_ref[i, :, 2 * p * D:2 * (p + 1) * D] = o[n].astype(o_ref.dtype)


def _scan_batch(b, most=4):
    return next(n for n in (8, 4, 2, 1) if n <= most and b % n == 0)


def _gdn_scan(q, k, v, ab, s0, alog_row, dtb_row, onorm_row):
    b, tp, _ = q.shape
    nb = _scan_batch(b)
    tok = pl.BlockSpec((nb, CHUNK, GDN_KDIM), lambda i, c: (i, c, 0))
    st = pl.BlockSpec((nb, GDN_HEADS // 2, GDN_DK, 2 * GDN_DV), lambda i, c: (i, 0, 0, 0))
    return pl.pallas_call(
        _gdn_scan_kernel,
        grid=(b // nb, tp // CHUNK),
        in_specs=[tok, tok, tok, pl.BlockSpec((nb, CHUNK, 2 * LANES), lambda i, c: (i, c, 0)), st,
                  _const_spec((1, LANES)), _const_spec((1, LANES)), _const_spec((1, GDN_DV))],
        out_specs=[tok, st],
        out_shape=[jax.ShapeDtypeStruct((b, tp, GDN_KDIM), BF16),
                   jax.ShapeDtypeStruct((b, GDN_HEADS // 2, GDN_DK, 2 * GDN_DV), F32)],
        compiler_params=_params(2),
        name="gdn_scan",
    )(q, k, v, ab, s0, alog_row, dtb_row, onorm_row)


def _mix_ffn_kernel(*refs, silu_gate, final_norm):
    if final_norm:
        y_ref, gm_ref, h_ref, wo_ref, nf_ref, wg_ref, wu_ref, wd_ref, fn_ref, o_ref = refs
    else:
        y_ref, gm_ref, h_ref, wo_ref, nf_ref, wg_ref, wu_ref, wd_ref, o_ref = refs
    gm = gm_ref[...].astype(F32)
    if silu_gate:
        gm = _silu(gm)
    h1 = h_ref[...] + _dot(y_ref[...].astype(F32) * gm, wo_ref[...])
    n = _rms(h1, nf_ref[...])
    act = _silu(_dot(n, wg_ref[...])) * _dot(n, wu_ref[...])
    h2 = h1 + _dot(act, wd_ref[...])
    if final_norm:
        h2 = _rms(h2, fn_ref[...])
    o_ref[...] = h2


def _mix_ffn(y2d, gm2d, h2d, wo, nf_row, wg, wu, wd, fn_row, *, silu_gate, name):
    rows, d = h2d.shape
    dff = wg.shape[1]
    tm = _pick_tile(rows, (512, 256, 128, 64))
    final_norm = fn_row is not None
    tok = pl.BlockSpec((tm, d), lambda i: (i, 0))
    in_specs = [tok, tok, tok, _const_spec((d, d)), _const_spec((1, d)),
                _const_spec((d, dff)), _const_spec((d, dff)), _const_spec((dff, d))]
    args = [y2d, gm2d, h2d, wo, nf_row, wg, wu, wd]
    if final_norm:
        in_specs.append(_const_spec((1, d)))
        args.append(fn_row)
    return pl.pallas_call(
        functools.partial(_mix_ffn_kernel, silu_gate=silu_gate, final_norm=final_norm),
        grid=(rows // tm,),
        in_specs=in_specs,
        out_specs=tok,
        out_shape=jax.ShapeDtypeStruct((rows, d), F32),
        compiler_params=_params(1),
        name=name,
    )(*args)


def _rwkv_proj_kernel(h_ref, sb_ref, vec_ref, mu_ref, wr_ref, wk_ref, wv_ref,
                      w1_ref, w2_ref, a1_ref, a2_ref, g1_ref, g2_ref,
                      r_o, k_o, v_o, kk_o, a_o, w_o, g_o, last_o, hs_ref):
    t = pl.program_id(1)
    tt = h_ref.shape[0]
    hist = SUBLANES
    hn = _rms(h_ref[...], vec_ref[0:1, :])

    @pl.when(t == 0)
    def _():
        hs_ref[...] = jnp.broadcast_to(sb_ref[...], hs_ref.shape)

    prev = pltpu.roll(jnp.concatenate([hs_ref[...], hn], axis=0), 1, 0)[hist:]
    hs_ref[hist - 1:hist, :] = hn[tt - 1:tt, :]
    last_o[...] = hn[tt - 1:tt, :]
    xx = prev - hn

    def mix(i):
        return (hn + xx * mu_ref[i:i + 1, :]).astype(BF16)

    r = _dot(mix(0), wr_ref[...])
    k = _dot(mix(2), wk_ref[...])
    v = _dot(mix(3), wv_ref[...])
    wl = vec_ref[1:2, :] + _dot(jnp.tanh(_dot(mix(1), w1_ref[...])), w2_ref[...])
    a = _sigmoid(vec_ref[2:3, :] + _dot(_dot(mix(4), a1_ref[...]), a2_ref[...]))
    gate = _dot(_sigmoid(_dot(mix(5), g1_ref[...])), g2_ref[...])
    r_o[...] = r.astype(r_o.dtype)
    k_o[...] = (k * (1.0 + (a - 1.0) * vec_ref[4:5, :])).astype(k_o.dtype)
    v_o[...] = v.astype(v_o.dtype)
    kk_o[...] = (k * vec_ref[3:4, :]).astype(kk_o.dtype)
    a_o[...] = a.astype(a_o.dtype)
    w_o[...] = -DECAY_SCALE * _sigmoid(wl)
    g_o[...] = gate.astype(g_o.dtype)


def _rwkv_proj(h3, sbuf, vec, mu8, wr, wk, wv, w1, w2, a1, a2, g1, g2):
    b, tp, d = h3.shape
    tt = _pick_tile(tp, (512, 256, 192, 128, 64))
    tok = pl.BlockSpec((None, tt, d), lambda i, t: (i, t, 0))
    row = pl.BlockSpec((None, 1, d), lambda i, t: (i, 0, 0))
    weights = [wr, wk, wv, w1, w2, a1, a2, g1, g2]
    tok_shape = lambda dt: jax.ShapeDtypeStruct((b, tp, d), dt)
    return pl.pallas_call(
        _rwkv_proj_kernel,
        grid=(b, tp // tt),
        in_specs=[tok, row, _const_spec(vec.shape), _const_spec(mu8.shape)] + [_const_spec(w.shape) for w in weights],
        out_specs=[tok] * 7 + [row],
        out_shape=[tok_shape(BF16)] * 5 + [tok_shape(F32), tok_shape(BF16), jax.ShapeDtypeStruct((b, 1, d), F32)],
        scratch_shapes=[pltpu.VMEM((SUBLANES, d), F32)],
        compiler_params=_params(2),
        name="rwkv_proj",
    )(h3, sbuf, vec, mu8, *weights)


def _rwkv_scan_kernel(r_ref, k_ref, v_ref, kk_ref, a_ref, w_ref, s0_ref, rk_ref, lnw_ref, lnb_ref,
                      yo_ref, s_ref):
    c = pl.program_id(1)
    L = CHUNK
    N = RWKV_N
    G = RWKV_GROUP
    Q = G * N
    nb, _, d = r_ref.shape
    groups = range(nb * d // Q)

    def cat_batch(ref):
        return jnp.concatenate([ref[i] for i in range(nb)], axis=1)

    @pl.when(c == 0)
    def _():
        s_ref[...] = s0_ref[...]

    row = lax.broadcasted_iota(jnp.int32, (L, L), 0)
    col = lax.broadcasted_iota(jnp.int32, (L, L), 1)
    tril = (row >= col).astype(BF16)
    t_idx = lax.broadcasted_iota(jnp.int32, (L, Q), 0)
    lane = lax.broadcasted_iota(jnp.int32, (L, Q), 1)
    s_idx = lane % N
    strict_c = t_idx > s_idx
    incl_c = t_idx >= s_idx
    ones_blk = (lax.broadcasted_iota(jnp.int32, (Q, Q), 0) // N
                == lax.broadcasted_iota(jnp.int32, (Q, Q), 1) // N).astype(BF16)
    head_lanes = [lax.broadcasted_iota(jnp.int32, (N, Q), 1) // N == g for g in range(G)]

    def per_group(x2d):
        return jnp.stack([x2d[:, q * Q:(q + 1) * Q] for q in groups])

    def expand(x3):
        xb = x3.astype(BF16)
        zero = jnp.zeros(xb.shape[:2] + (LANES,), BF16)
        slot_of_lane = lax.broadcasted_iota(jnp.int32, xb.shape[1:2] + (LANES,), 1) // N
        blocks = []
        for g in range(G):
            col, slot = divmod(g * N, LANES)
            piece = jnp.where(slot_of_lane == slot // N, xb[:, :, col * LANES:(col + 1) * LANES], zero)
            blocks.append(jnp.concatenate([piece if c == col else zero for c in range(Q // LANES)], axis=2))
        return jnp.concatenate(blocks, axis=1)

    def diag_blocks(x3):
        return sum(jnp.where(head_lanes[g], x3[:, g * N:(g + 1) * N, :], 0.0) for g in range(G))

    def lane_sums(x3):
        n, rws, _ = x3.shape
        return _dot(x3.reshape(n * rws, Q), ones_blk).reshape(n, rws, Q)

    w_all = cat_batch(w_ref)
    cw_all = _dot_exact_lhs(tril, w_all)
    cwl = cw_all[L - 1:L, :]
    p_in = per_group(jnp.exp(cw_all))
    p_ex = per_group(jnp.exp(cw_all - w_all))
    p_inv = per_group(jnp.exp(-cw_all))
    p_rem = per_group(jnp.exp(cwl - cw_all))
    dec_all = jnp.exp(cwl)
    p_all = jnp.stack([dec_all[:, q * Q:(q + 1) * Q] for q in groups])

    kkr = per_group(cat_batch(kk_ref).astype(F32))
    kkn = kkr * lax.rsqrt(lane_sums(kkr * kkr) + L2_EPS)
    rp = per_group(cat_batch(r_ref).astype(F32))
    kp = per_group(cat_batch(k_ref).astype(F32))
    vp = per_group(cat_batch(v_ref).astype(F32))
    bb = kkn * per_group(cat_batch(a_ref).astype(F32))

    lhs = jnp.concatenate([-kkn * p_ex, rp * p_in], axis=1).astype(BF16)
    m = _bdot_nt(lhs, jnp.concatenate([expand(bb * p_inv), expand(kp * p_inv)], axis=1))
    l_ab = jnp.where(strict_c, m[:, :L, :Q], 0.0)
    l_ak = jnp.where(strict_c, m[:, :L, Q:], 0.0)
    m_rb = jnp.where(incl_c, m[:, L:, :Q], 0.0)
    m_rk = jnp.where(incl_c, m[:, L:, Q:], 0.0)

    s = s_ref[...].reshape(len(groups), N, Q)
    from_s = _bdot_nt(lhs, expand(s))

    t_inv = _unit_tri_inverse_cat(l_ab, t_idx, s_idx, L, expand)

    from_v = _bdot(jnp.concatenate([l_ak, m_rk], axis=1), expand(vp))
    u = _bdot(t_inv, expand(from_s[:, :L] + from_v[:, :L]))
    y = from_s[:, L:] + from_v[:, L:] + _bdot(m_rb, expand(u))
    cross = _bdot_tn(jnp.concatenate([u, vp], axis=1), jnp.concatenate([bb * p_rem, kp * p_rem], axis=1))
    s_ref[...] = (s * p_all + diag_blocks(cross)).reshape(s_ref.shape)

    mean = lane_sums(y) * (1.0 / N)
    cy = y - mean
    var = lane_sums(cy * cy) * (1.0 / N)
    rk3 = per_group(jnp.concatenate([rk_ref[...]] * nb, axis=1))
    bonus = lane_sums(rp * kp * rk3) * vp
    for g in groups:
        i, q = divmod(g, d // Q)
        sl = slice(q * Q, (q + 1) * Q)
        yn = cy[g] * lax.rsqrt(var[g] + GN_EPS) * lnw_ref[:, sl] + lnb_ref[:, sl]
        yo_ref[i, :, sl] = (yn + bonus[g]).astype(yo_ref.dtype)


def _rwkv_scan(r, k, v, kk, a, w, s0_groups, rk_row, lnw_row, lnb_row):
    b, tp, d = r.shape
    q = RWKV_GROUP * RWKV_N
    nb = _scan_batch(b, most=8)
    tok = pl.BlockSpec((nb, CHUNK, d), lambda i, c: (i, c, 0))
    st = pl.BlockSpec((nb, d // q, RWKV_N, q), lambda i, c: (i, 0, 0, 0))
    return pl.pallas_call(
        _rwkv_scan_kernel,
        grid=(b // nb, tp // CHUNK),
        in_specs=[tok] * 6 + [st, _const_spec((1, d)), _const_spec((1, d)), _const_spec((1, d))],
        out_specs=[tok, st],
        out_shape=[jax.ShapeDtypeStruct((b, tp, d), BF16),
                   jax.ShapeDtypeStruct((b, d // q, RWKV_N, q), F32)],
        compiler_params=_params(2),
        name="rwkv_scan",
    )(r, k, v, kk, a, w, s0_groups, rk_row, lnw_row, lnb_row)


def _pad_cols(w, n):
    return jnp.pad(w, ((0, 0), (0, n - w.shape[1])))


def _pad_rows(w, n):
    return jnp.pad(w, ((0, n - w.shape[0]), (0, 0)))


def _groups_from_state(s, g):
    b, h, r, c = s.shape
    return jnp.swapaxes(s.reshape(b, h // g, g, r, c), 2, 3).reshape(b, h // g, r, g * c)


def _state_from_groups(sg, g):
    b, hq, r, gc = sg.shape
    return jnp.swapaxes(sg.reshape(b, hq, r, g, gc // g), 2, 3).reshape(b, hq * g, r, gc // g)


def _run_trunk(h, conv_bufs, gdn_states, shift_bufs, rwkv_states, wts):
    b, tp, d = h.shape
    rows = b * tp
    depth = wts["norm_mix"].shape[0]
    new_conv, new_gdn, new_shift, new_rwkv = [], [], [], []
    for i in range(depth):
        j = i // 2
        fn_row = wts["norm_final"] if i == depth - 1 else None
        if i % 2 == 0:
            g = wts["gdn"][j]
            cbuf8 = jnp.pad(conv_bufs[j], ((0, 0), (SUBLANES - (CONV_W - 1), 0), (0, 0)))
            q, k, v, z, ab, cst = _gdn_proj(h, wts["norm_mix"][i:i + 1], g["w_in"], g["conv_w"], cbuf8)
            o, s_new = _gdn_scan(q, k, v, ab, _groups_from_state(gdn_states[j], 2), g["a_log"], g["dt_bias"],
                                 g["o_norm"])
            new_conv.append(cst[:, SUBLANES - (CONV_W - 1):])
            new_gdn.append(_state_from_groups(s_new, 2))
            h2 = _mix_ffn(o.reshape(rows, d), z.reshape(rows, d), h.reshape(rows, d), g["w_out"],
                          wts["norm_ffn"][i:i + 1], wts["ffn_gate"][i], wts["ffn_up"][i], wts["ffn_down"][i],
                          fn_row, silu_gate=True, name="gdn_out_ffn")
        else:
            rw = wts["rwkv"][j]
            vec = jnp.concatenate([wts["norm_mix"][i:i + 1], rw["vec"]], axis=0)
            r, k, v, kk, a, w, gate, last_hn = _rwkv_proj(
                h, shift_bufs[j], vec, rw["mu"], rw["w_r"], rw["w_k"], rw["w_v"],
                rw["w1"], rw["w2"], rw["a1"], rw["a2"], rw["g1"], rw["g2"])
            yo, sp_new = _rwkv_scan(r, k, v, kk, a, w, _groups_from_state(rwkv_states[j], RWKV_GROUP),
                                    rw["r_k"], rw["ln_w"], rw["ln_b"])
            new_shift.append(last_hn)
            new_rwkv.append(_state_from_groups(sp_new, RWKV_GROUP))
            h2 = _mix_ffn(yo.reshape(rows, d), gate.reshape(rows, d), h.reshape(rows, d), rw["w_o"],
                          wts["norm_ffn"][i:i + 1], wts["ffn_gate"][i], wts["ffn_up"][i], wts["ffn_down"][i],
                          fn_row, silu_gate=False, name="rwkv_out_ffn")
        h = h2.reshape(b, tp, d)
    return h, jnp.stack(new_conv), jnp.stack(new_gdn), jnp.stack(new_shift), jnp.stack(new_rwkv)


def _prepare_weights(norm_mix, norm_ffn, norm_final, gdn_w_in, gdn_conv_w, gdn_a_log, gdn_dt_bias, gdn_o_norm,
                     gdn_w_out, rwkv_mu, rwkv_w0, rwkv_w1, rwkv_w2, rwkv_a0, rwkv_a1, rwkv_a2, rwkv_g1, rwkv_g2,
                     rwkv_k_k, rwkv_k_a, rwkv_r_k, rwkv_w_r, rwkv_w_k, rwkv_w_v, rwkv_w_o, rwkv_ln_w,
                     rwkv_ln_b, ffn_w_gate, ffn_w_up, ffn_w_down):
    d = norm_mix.shape[1]
    gdn = []
    for j in range(gdn_w_in.shape[0]):
        w = gdn_w_in[j]
        qkvz = w[:, :GDN_QKV + GDN_KDIM]
        a_cols = _pad_cols(w[:, GDN_QKV + GDN_KDIM:GDN_QKV + GDN_KDIM + GDN_HEADS], LANES)
        b_cols = _pad_cols(w[:, GDN_QKV + GDN_KDIM + GDN_HEADS:], LANES)
        gdn.append(dict(
            w_in=jnp.concatenate([qkvz, a_cols, b_cols], axis=1).astype(BF16),
            conv_w=gdn_conv_w[j],
            a_log=_pad_cols(gdn_a_log[j][None], LANES),
            dt_bias=_pad_cols(gdn_dt_bias[j][None], LANES),
            o_norm=gdn_o_norm[j][None],
            w_out=gdn_w_out[j].astype(BF16),
        ))
    rwkv = []
    for j in range(rwkv_mu.shape[0]):
        lw = -(-rwkv_w1.shape[2] // LANES) * LANES
        la = -(-rwkv_a1.shape[2] // LANES) * LANES
        lg = -(-rwkv_g1.shape[2] // LANES) * LANES
        rwkv.append(dict(
            vec=jnp.stack([rwkv_w0[j], rwkv_a0[j], rwkv_k_k[j], rwkv_k_a[j],
                           jnp.zeros_like(rwkv_w0[j]), jnp.zeros_like(rwkv_w0[j]), jnp.zeros_like(rwkv_w0[j])]),
            mu=_pad_rows(rwkv_mu[j], SUBLANES),
            w_r=rwkv_w_r[j].astype(BF16), w_k=rwkv_w_k[j].astype(BF16), w_v=rwkv_w_v[j].astype(BF16),
            w_o=rwkv_w_o[j].astype(BF16),
            w1=_pad_cols(rwkv_w1[j], lw).astype(BF16), w2=_pad_rows(rwkv_w2[j], lw).astype(BF16),
            a1=_pad_cols(rwkv_a1[j], la).astype(BF16), a2=_pad_rows(rwkv_a2[j], la).astype(BF16),
            g1=_pad_cols(rwkv_g1[j], lg).astype(BF16), g2=_pad_rows(rwkv_g2[j], lg).astype(BF16),
            r_k=rwkv_r_k[j].reshape(1, d), ln_w=rwkv_ln_w[j][None], ln_b=rwkv_ln_b[j][None],
        ))
    return dict(norm_mix=norm_mix, norm_ffn=norm_ffn, norm_final=norm_final[None], gdn=gdn, rwkv=rwkv,
                ffn_gate=ffn_w_gate.astype(BF16), ffn_up=ffn_w_up.astype(BF16), ffn_down=ffn_w_down.astype(BF16))


def kernel(x_prompt, x_sample, cache_gdn_conv, state_gdn, cache_rwkv_shift, state_rwkv, meta_tokens, norm_mix, norm_ffn, norm_final, gdn_w_in, gdn_conv_w, gdn_a_log, gdn_dt_bias, gdn_o_norm, gdn_w_out, rwkv_mu, rwkv_w0, rwkv_w1, rwkv_w2, rwkv_a0, rwkv_a1, rwkv_a2, rwkv_g1, rwkv_g2, rwkv_k_k, rwkv_k_a, rwkv_r_k, rwkv_w_r, rwkv_w_k, rwkv_w_v, rwkv_w_o, rwkv_ln_w, rwkv_ln_b, ffn_w_gate, ffn_w_up, ffn_w_down):
    wts = _prepare_weights(norm_mix, norm_ffn, norm_final, gdn_w_in, gdn_conv_w, gdn_a_log, gdn_dt_bias,
                           gdn_o_norm, gdn_w_out, rwkv_mu, rwkv_w0, rwkv_w1, rwkv_w2, rwkv_a0, rwkv_a1, rwkv_a2,
                           rwkv_g1, rwkv_g2, rwkv_k_k, rwkv_k_a, rwkv_r_k, rwkv_w_r, rwkv_w_k, rwkv_w_v,
                           rwkv_w_o, rwkv_ln_w, rwkv_ln_b, ffn_w_gate, ffn_w_up, ffn_w_down)
    n_gdn = gdn_w_in.shape[0]
    n_rwkv = rwkv_mu.shape[0]
    b, seq, d = x_prompt.shape
    dt = x_prompt.dtype

    lead_pad = (-N_META) % CHUNK
    lead = jnp.concatenate([jnp.zeros((b, lead_pad, d), dt),
                            jnp.broadcast_to(meta_tokens.astype(dt)[None], (b, N_META, d))], axis=1)
    zeros_like_states = (jnp.zeros((n_gdn, b, CONV_W - 1, GDN_QKV), dt),
                         jnp.zeros((n_gdn, b, GDN_HEADS, GDN_DK, GDN_DV), dt),
                         jnp.zeros((n_rwkv, b, 1, d), dt),
                         jnp.zeros((n_rwkv, b, d // RWKV_N, RWKV_N, RWKV_N), dt))
    sample_states = (cache_gdn_conv, state_gdn, cache_rwkv_shift, state_rwkv)

    bs = x_sample.shape[0]
    if x_sample.shape[1:] == lead.shape[1:]:
        both = _run_trunk(jnp.concatenate([x_sample, lead], axis=0),
                          *[jnp.concatenate([s, z], axis=1) for s, z in zip(sample_states, zeros_like_states)],
                          wts)
        y_sample, s_conv, s_gdn, s_shift, s_rwkv = [both[0][:bs]] + [t[:, :bs] for t in both[1:]]
        lead_states = [t[:, bs:] for t in both[1:]]
    else:
        y_sample, s_conv, s_gdn, s_shift, s_rwkv = _run_trunk(x_sample, *sample_states, wts)
        lead_states = _run_trunk(lead, *zeros_like_states, wts)[1:]

    y_prompt, p_conv, p_gdn, p_shift, p_rwkv = _run_trunk(x_prompt, *lead_states, wts)

    return (y_prompt, y_sample, p_conv, p_gdn, p_shift, p_rwkv, s_conv, s_gdn, s_shift, s_rwkv)
```
